```python
import math
import functools
import jax, jax.numpy as jnp
from jax import lax
import numpy as np

D_MODEL = 1024
BATCH = 8
SEQ = 4096
DEPTH = 1
DEC_BATCH = 32
DEC_SEQ = 8
PAST_LEN = 16384
PAGE_SIZE = 128

HEAD_DIM = 64
HEADS_PER_GROUP = 4
GROUPS = ((128, 1), (512, 4), (2048, 16))
N_GROUPS = 3
N_ATT_HEADS = N_GROUPS * HEADS_PER_GROUP
ATT_WIDTH = N_ATT_HEADS * HEAD_DIM
ATT_OUT = HEADS_PER_GROUP * HEAD_DIM
ATT_SCALE = HEAD_DIM ** -0.5
NUM_BUCKETS = 32
MAX_DISTANCE = 2048
LRU_WIDTH = 3 * D_MODEL // 4
LRU_BLOCKS = 12
LRU_BLOCK_DIM = LRU_WIDTH // LRU_BLOCKS
CONV_WIDTH = 4
LRU_C = 8.0
N_MEM = 256
MEM_HEADS = 4
MEM_HEAD_DIM = 128
MEM_WIDTH = MEM_HEADS * MEM_HEAD_DIM
MEM_SCALE = MEM_HEAD_DIM ** -0.5
D_FF = 2816
RMS_EPS = 1e-6
IN_COLS = 3 * ATT_WIDTH + 2 * LRU_WIDTH + MEM_WIDTH + 3 * D_MODEL

kernel_name = "hybrid_dilated_rglru_memory_decoder_step"


def rmsnorm(x, g):
    x32 = x.astype(jnp.float32)
    y = x32 * lax.rsqrt(jnp.mean(x32 * x32, axis=-1, keepdims=True) + RMS_EPS)
    return (y * g.astype(jnp.float32)).astype(x.dtype)


def swiglu(x, w_gu, w_down):
    gate, up = jnp.split(x @ w_gu, 2, axis=-1)
    return (jax.nn.silu(gate) * up) @ w_down


def rel_bucket(dist):
    n = jnp.maximum(dist, 0)
    max_exact = NUM_BUCKETS // 2
    nf = jnp.maximum(n, 1).astype(jnp.float32)
    large = max_exact + (jnp.log(nf / max_exact) / math.log(MAX_DISTANCE / max_exact)
                         * (NUM_BUCKETS - max_exact)).astype(jnp.int32)
    large = jnp.minimum(large, NUM_BUCKETS - 1)
    return jnp.where(n < max_exact, n, large)


def split_points():
    sizes = [ATT_WIDTH] * 3 + [LRU_WIDTH] * 2 + [MEM_WIDTH] + [D_MODEL] * 2
    pts, acc = [], 0
    for s in sizes:
        acc += s
        pts.append(acc)
    return pts


def dilated_group_prompt(q, k, v, bias_tab, window, dil):
    B, S, H, E = q.shape
    n_rel = window // dil
    L = S // dil
    nb = -(-L // n_rel)
    Lp = nb * n_rel

    def to_blocks(t):
        t = t.reshape(B, L, dil, H, E)
        t = jnp.pad(t, ((0, 0), (0, Lp - L), (0, 0), (0, 0), (0, 0)))
        return t.reshape(B, nb, n_rel, dil, H, E)

    def with_prev(t):
        prev = jnp.pad(t, ((0, 0), (1, 0), (0, 0), (0, 0), (0, 0), (0, 0)))[:, :nb]
        return jnp.concatenate([prev, t], axis=2)

    qb = to_blocks(q)
    kc = with_prev(to_blocks(k))
    vc = with_prev(to_blocks(v))
    qi = jnp.arange(n_rel)[:, None]
    ki = jnp.arange(2 * n_rel)[None, :]
    delta = qi - ki + n_rel
    band = (delta >= 0) & (delta <= n_rel)
    valid = band[None] & ((jnp.arange(nb)[:, None, None] * n_rel + ki[None] - n_rel) >= 0)
    bias = jnp.transpose(bias_tab[rel_bucket(dil * delta)], (2, 0, 1)).astype(jnp.float32)
    s = jnp.einsum('bnqrhe,bnkrhe->bnrhqk', qb, kc).astype(jnp.float32) * ATT_SCALE + bias
    s = jnp.where(valid[None, :, None, None], s, -jnp.inf)
    m = jnp.max(s, axis=-1, keepdims=True)
    p = jnp.exp(s - m)
    l = jnp.sum(p, axis=-1)
    o = jnp.einsum('bnrhqk,bnkrhe->bnqrhe', p.astype(v.dtype), vc).astype(jnp.float32)
    o = o / jnp.transpose(l, (0, 1, 4, 2, 3))[..., None]
    lse = jnp.transpose(m[..., 0] + jnp.log(l), (0, 1, 4, 2, 3))
    o = o.reshape(B, Lp, dil, H, E)[:, :L].reshape(B, S, H, E)
    lse = lse.reshape(B, Lp, dil, H)[:, :L].reshape(B, S, H)
    return o, lse


def dilated_group_sample(q, k, v, k_buf, v_buf, bias_tab, window, dil):
    Lc = k_buf.shape[1]
    T = q.shape[1]
    n_rel = window // dil
    kc = jnp.concatenate([k_buf, k.astype(k_buf.dtype)], axis=1)
    vc = jnp.concatenate([v_buf, v.astype(v_buf.dtype)], axis=1)
    steps = jnp.arange(n_rel + 1)
    idx = (Lc + jnp.arange(T))[:, None] - dil * steps[None, :]
    valid = idx >= 0
    idx = jnp.maximum(idx, 0)
    kg = kc[:, idx]
    vg = vc[:, idx]
    bias = bias_tab[rel_bucket(dil * steps)].T.astype(jnp.float32)
    s = jnp.einsum('bthe,btmhe->bhtm', q, kg).astype(jnp.float32) * ATT_SCALE + bias[:, None, :]
    s = jnp.where(valid[None, None], s, -jnp.inf)
    m = jnp.max(s, axis=-1, keepdims=True)
    p = jnp.exp(s - m)
    l = jnp.sum(p, axis=-1)
    o = jnp.einsum('bhtm,btmhe->bthe', p.astype(vg.dtype), vg).astype(jnp.float32)
    o = o / jnp.transpose(l, (0, 2, 1))[..., None]
    lse = jnp.transpose(m[..., 0] + jnp.log(l), (0, 2, 1))
    return o, lse, kc[:, T:], vc[:, T:]


def combine_groups(outs, lses):
    wts = jax.nn.softmax(jnp.stack(lses, axis=0), axis=0)
    o = jnp.einsum('gblh,gblhe->blhe', wts, jnp.stack(outs, axis=0))
    return o.reshape(o.shape[0], o.shape[1], ATT_OUT)


def attend_prompt(q, k, v, rel_bias):
    S = q.shape[1]
    outs, lses, states = [], [], []
    for g, (window, dil) in enumerate(GROUPS):
        sl = slice(g * HEADS_PER_GROUP, (g + 1) * HEADS_PER_GROUP)
        o, lse = dilated_group_prompt(q[:, :, sl], k[:, :, sl], v[:, :, sl], rel_bias[:, sl], window, dil)
        outs.append(o)
        lses.append(lse)
        keep = min(window, S)
        states += [k[:, S - keep:, sl], v[:, S - keep:, sl]]
    return combine_groups(outs, lses), states


def attend_sample(q, k, v, rel_bias, bufs):
    outs, lses, states = [], [], []
    for g, (window, dil) in enumerate(GROUPS):
        sl = slice(g * HEADS_PER_GROUP, (g + 1) * HEADS_PER_GROUP)
        o, lse, kb, vb = dilated_group_sample(q[:, :, sl], k[:, :, sl], v[:, :, sl],
                                              bufs[2 * g], bufs[2 * g + 1], rel_bias[:, sl], window, dil)
        outs.append(o)
        lses.append(lse)
        states += [kb, vb]
    return combine_groups(outs, lses), states


def causal_conv(x, buf, w, b):
    xp = jnp.concatenate([buf.astype(x.dtype), x], axis=1)
    y = lax.conv_general_dilated(xp, w[:, None, :].astype(x.dtype), window_strides=(1,), padding='VALID',
                                 dimension_numbers=('NWC', 'WIO', 'NWC'),
                                 feature_group_count=x.shape[-1])
    return y + b, xp[:, xp.shape[1] - (CONV_WIDTH - 1):]


def rglru(x, h0, w_a, b_a, w_i, b_i, lam):
    B, L, R = x.shape
    x32 = x.astype(jnp.float32)
    xb = x32.reshape(B, L, LRU_BLOCKS, LRU_BLOCK_DIM)
    r = jax.nn.sigmoid(jnp.einsum('blne,nef->blnf', xb, w_a.astype(jnp.float32)).reshape(B, L, R) + b_a)
    i = jax.nn.sigmoid(jnp.einsum('blne,nef->blnf', xb, w_i.astype(jnp.float32)).reshape(B, L, R) + b_i)
    log_a = -LRU_C * r * jax.nn.softplus(-lam.astype(jnp.float32))
    a = jnp.exp(log_a)
    u = jnp.sqrt(-jnp.expm1(2.0 * log_a)) * (i * x32)

    def step(h, au):
        a_t, u_t = au
        h = a_t * h + u_t
        return h, h

    h_last, hs = lax.scan(step, h0.astype(jnp.float32), (jnp.swapaxes(a, 0, 1), jnp.swapaxes(u, 0, 1)))
    return jnp.swapaxes(hs, 0, 1).astype(x.dtype), h_last.astype(h0.dtype)


def memory_kv(mem, g, w_kv):
    B, M, _ = mem.shape
    mk, mv = jnp.split(rmsnorm(mem, g) @ w_kv, 2, axis=-1)
    return mk.reshape(B, M, MEM_HEADS, MEM_HEAD_DIM), mv.reshape(B, M, MEM_HEADS, MEM_HEAD_DIM)


def memory_attention(q, mk, mv):
    B, L = q.shape[0], q.shape[1]
    s = jnp.einsum('blhe,bmhe->bhlm', q, mk.astype(q.dtype)).astype(jnp.float32) * MEM_SCALE
    p = jax.nn.softmax(s, axis=-1).astype(q.dtype)
    o = jnp.einsum('bhlm,bmhe->blhe', p, mv.astype(q.dtype))
    return o.reshape(B, L, MEM_WIDTH)


def decoder_layer(x, attend, conv_buf, lru_h0, mk, mv, rel_bias, lp):
    B, L, _ = x.shape
    x = x + 0.5 * swiglu(rmsnorm(x, lp['ffn1_norm']), lp['ffn1_w_gu'], lp['ffn1_w_down'])
    h = rmsnorm(x, lp['mix_norm'])
    z = h @ lp['w_in']
    q, k, v, xr, yr, qm, ga, gb, gc = jnp.split(z, split_points(), axis=-1)
    bg_a, bg_b, bg_c = jnp.split(lp['b_gate'], 3)
    y_att, att_state = attend(q.reshape(B, L, N_ATT_HEADS, HEAD_DIM),
                              k.reshape(B, L, N_ATT_HEADS, HEAD_DIM),
                              v.reshape(B, L, N_ATT_HEADS, HEAD_DIM), rel_bias)
    y_att = y_att.astype(x.dtype)
    xc, conv_new = causal_conv(xr, conv_buf, lp['conv_w'], lp['conv_b'])
    hr, lru_new = rglru(xc, lru_h0, lp['lru_w_a'], lp['lru_b_a'], lp['lru_w_i'], lp['lru_b_i'], lp['lru_lambda'])
    y_rec = jax.nn.gelu(yr) * hr
    y_mem = memory_attention(qm.reshape(B, L, MEM_HEADS, MEM_HEAD_DIM), mk, mv)
    merged = (jax.nn.sigmoid(ga + bg_a) * (y_att @ lp['w_att_o'])
              + jax.nn.sigmoid(gb + bg_b) * (y_rec @ lp['w_rec_o'])
              + jax.nn.sigmoid(gc + bg_c) * (y_mem @ lp['w_mem_o']))
    x = x + merged @ lp['w_out']
    x = x + 0.5 * swiglu(rmsnorm(x, lp['ffn2_norm']), lp['ffn2_w_gu'], lp['ffn2_w_down'])
    return x, att_state, conv_new, lru_new


def setup_inputs(seed: int = 0) -> dict:
    key = jax.random.key(seed)
    keys = list(jax.random.split(key, 48))
    f32 = jnp.float32

    def nrm(shape, scale):
        return scale * jax.random.normal(keys.pop(), shape, f32)

    def gain(shape):
        return 1.0 + 0.01 * jax.random.normal(keys.pop(), shape, f32)

    lc = [min(w, PAST_LEN) for w, _ in GROUPS]
    u = jax.random.uniform(keys.pop(), (DEPTH, LRU_WIDTH), f32, 0.9, 0.999)
    lru_lambda = jnp.log(u) - jnp.log1p(-u)
    return {
        'x_prompt': nrm((BATCH, SEQ, D_MODEL), 1.0),
        'x_sample': nrm((DEC_BATCH, DEC_SEQ, D_MODEL), 1.0),
        'mem_prompt': nrm((BATCH, N_MEM, D_MODEL), 1.0),
        'cache_win_k0': nrm((DEPTH, DEC_BATCH, lc[0], HEADS_PER_GROUP, HEAD_DIM), 1.0),
        'cache_win_v0': nrm((DEPTH, DEC_BATCH, lc[0], HEADS_PER_GROUP, HEAD_DIM), 1.0),
        'cache_win_k1': nrm((DEPTH, DEC_BATCH, lc[1], HEADS_PER_GROUP, HEAD_DIM), 1.0),
        'cache_win_v1': nrm((DEPTH, DEC_BATCH, lc[1], HEADS_PER_GROUP, HEAD_DIM), 1.0),
        'cache_win_k2': nrm((DEPTH, DEC_BATCH, lc[2], HEADS_PER_GROUP, HEAD_DIM), 1.0),
        'cache_win_v2': nrm((DEPTH, DEC_BATCH, lc[2], HEADS_PER_GROUP, HEAD_DIM), 1.0),
        'state_conv': nrm((DEPTH, DEC_BATCH, CONV_WIDTH - 1, LRU_WIDTH), 1.0),
        'state_lru': nrm((DEPTH, DEC_BATCH, LRU_WIDTH), 0.5),
        'cache_mem_k': nrm((DEPTH, DEC_BATCH, N_MEM, MEM_HEADS, MEM_HEAD_DIM), 1.0),
        'cache_mem_v': nrm((DEPTH, DEC_BATCH, N_MEM, MEM_HEADS, MEM_HEAD_DIM), 1.0),
        'rel_bias': nrm((NUM_BUCKETS, N_ATT_HEADS), 0.5),
        'ffn1_norm': gain((DEPTH, D_MODEL)),
        'ffn1_w_gu': nrm((DEPTH, D_MODEL, 2 * D_FF), D_MODEL ** -0.5),
        'ffn1_w_down': nrm((DEPTH, D_FF, D_MODEL), D_FF ** -0.5),
        'mix_norm': gain((DEPTH, D_MODEL)),
        'w_in': nrm((DEPTH, D_MODEL, IN_COLS), D_MODEL ** -0.5),
        'b_gate': nrm((DEPTH, 3 * D_MODEL), 0.02),
        'conv_w': nrm((DEPTH, CONV_WIDTH, LRU_WIDTH), CONV_WIDTH ** -0.5),
        'conv_b': nrm((DEPTH, LRU_WIDTH), 0.02),
        'lru_w_a': nrm((DEPTH, LRU_BLOCKS, LRU_BLOCK_DIM, LRU_BLOCK_DIM), LRU_BLOCK_DIM ** -0.5),
        'lru_b_a': nrm((DEPTH, LRU_WIDTH), 0.02),
        'lru_w_i': nrm((DEPTH, LRU_BLOCKS, LRU_BLOCK_DIM, LRU_BLOCK_DIM), LRU_BLOCK_DIM ** -0.5),
        'lru_b_i': nrm((DEPTH, LRU_WIDTH), 0.02),
        'lru_lambda': lru_lambda,
        'mem_norm': gain((DEPTH, D_MODEL)),
        'w_mem_kv': nrm((DEPTH, D_MODEL, 2 * MEM_WIDTH), D_MODEL ** -0.5),
        'w_att_o': nrm((DEPTH, ATT_OUT, D_MODEL), ATT_OUT ** -0.5),
        'w_rec_o': nrm((DEPTH, LRU_WIDTH, D_MODEL), LRU_WIDTH ** -0.5),
        'w_mem_o': nrm((DEPTH, MEM_WIDTH, D_MODEL), MEM_WIDTH ** -0.5),
        'w_out': nrm((DEPTH, D_MODEL, D_MODEL), D_MODEL ** -0.5),
        'ffn2_norm': gain((DEPTH, D_MODEL)),
        'ffn2_w_gu': nrm((DEPTH, D_MODEL, 2 * D_FF), D_MODEL ** -0.5),
        'ffn2_w_down': nrm((DEPTH, D_FF, D_MODEL), D_FF ** -0.5),
        'final_norm': gain((D_MODEL,)),
    }


def reference(x_prompt, x_sample, mem_prompt,
              cache_win_k0, cache_win_v0, cache_win_k1, cache_win_v1, cache_win_k2, cache_win_v2,
              state_conv, state_lru, cache_mem_k, cache_mem_v,
              rel_bias,
              ffn1_norm, ffn1_w_gu, ffn1_w_down,
              mix_norm, w_in, b_gate,
              conv_w, conv_b, lru_w_a, lru_b_a, lru_w_i, lru_b_i, lru_lambda,
              mem_norm, w_mem_kv,
              w_att_o, w_rec_o, w_mem_o, w_out,
              ffn2_norm, ffn2_w_gu, ffn2_w_down,
              final_norm):
    xp, xs = x_prompt, x_sample
    Bp = xp.shape[0]
    p_att, p_conv, p_lru, p_mk, p_mv = [], [], [], [], []
    s_att, s_conv, s_lru = [], [], []
    for l in range(DEPTH):
        lp = dict(ffn1_norm=ffn1_norm[l], ffn1_w_gu=ffn1_w_gu[l], ffn1_w_down=ffn1_w_down[l],
                  mix_norm=mix_norm[l], w_in=w_in[l], b_gate=b_gate[l],
                  conv_w=conv_w[l], conv_b=conv_b[l], lru_w_a=lru_w_a[l], lru_b_a=lru_b_a[l],
                  lru_w_i=lru_w_i[l], lru_b_i=lru_b_i[l], lru_lambda=lru_lambda[l],
                  w_att_o=w_att_o[l], w_rec_o=w_rec_o[l], w_mem_o=w_mem_o[l], w_out=w_out[l],
                  ffn2_norm=ffn2_norm[l], ffn2_w_gu=ffn2_w_gu[l], ffn2_w_down=ffn2_w_down[l])
        mk, mv = memory_kv(mem_prompt, mem_norm[l], w_mem_kv[l])
        conv0 = jnp.zeros((Bp, CONV_WIDTH - 1, LRU_WIDTH), xp.dtype)
        h0 = jnp.zeros((Bp, LRU_WIDTH), xp.dtype)
        xp, att_st, conv_st, lru_st = decoder_layer(xp, attend_prompt, conv0, h0, mk, mv, rel_bias, lp)
        p_att.append(att_st)
        p_conv.append(conv_st)
        p_lru.append(lru_st)
        p_mk.append(mk)
        p_mv.append(mv)
        bufs = (cache_win_k0[l], cache_win_v0[l], cache_win_k1[l], cache_win_v1[l],
                cache_win_k2[l], cache_win_v2[l])
        xs, att_st, conv_st, lru_st = decoder_layer(xs, functools.partial(attend_sample, bufs=bufs),
                                                    state_conv[l], state_lru[l],
                                                    cache_mem_k[l], cache_mem_v[l], rel_bias, lp)
        s_att.append(att_st)
        s_conv.append(conv_st)
        s_lru.append(lru_st)
    y_prompt = rmsnorm(xp, final_norm)
    y_sample = rmsnorm(xs, final_norm)
    p_win_k0, p_win_v0, p_win_k1, p_win_v1, p_win_k2, p_win_v2 = [
        jnp.stack([st[i] for st in p_att]) for i in range(2 * N_GROUPS)]
    s_win_k0, s_win_v0, s_win_k1, s_win_v1, s_win_k2, s_win_v2 = [
        jnp.stack([st[i] for st in s_att]) for i in range(2 * N_GROUPS)]
    p_conv_out = jnp.stack(p_conv)
    p_lru_out = jnp.stack(p_lru)
    p_mem_k = jnp.stack(p_mk)
    p_mem_v = jnp.stack(p_mv)
    s_conv_out = jnp.stack(s_conv)
    s_lru_out = jnp.stack(s_lru)
    return (y_prompt, y_sample,
            p_win_k0, p_win_v0, p_win_k1, p_win_v1, p_win_k2, p_win_v2,
            p_conv_out, p_lru_out, p_mem_k, p_mem_v,
            s_win_k0, s_win_v0, s_win_k1, s_win_v1, s_win_k2, s_win_v2,
            s_conv_out, s_lru_out)
```

```python
import functools
import math

import numpy as np
import jax
import jax.numpy as jnp
from jax import lax
from jax.experimental import pallas as pl
from jax.experimental.pallas import tpu as pltpu

F32 = jnp.float32
BF16 = jnp.bfloat16

D_MODEL = 1024
D_FF = 2816
HEAD_DIM = 64
HEADS = 4
GROUPS = ((128, 1), (512, 4), (2048, 16))
N_REL = 128
ATT_W = HEADS * HEAD_DIM
ATT_SCALE = HEAD_DIM ** -0.5
NUM_BUCKETS = 32
MAX_DISTANCE = 2048
LRU_W = 768
LRU_BLOCK = 64
LRU_C = 8.0
CONV_W = 4
N_MEM = 256
MEM_HEADS = 4
MEM_HEAD_DIM = 128
MEM_W = MEM_HEADS * MEM_HEAD_DIM
MEM_SCALE = MEM_HEAD_DIM ** -0.5
RMS_EPS = 1e-6
QKV_W = 3 * len(GROUPS) * ATT_W
COL_XR = QKV_W
COL_YR = COL_XR + LRU_W
COL_QM = COL_YR + LRU_W
COL_GATE = COL_QM + MEM_W
IN_COLS = COL_GATE + 3 * D_MODEL

V7X_VMEM_LIMIT = 56 * 1024 * 1024
MXU_TILE = 256
SUBLANES = 8
LANES = 128
FFN_TILE = 512
FFN_CHUNK = 256
PROJ_TILE = 512
ATTN_TILE = 2048
MERGE_TILE = 512
COMBINE_ROWS = 256


def _dot(a, b):
    return jnp.dot(a, b, preferred_element_type=F32)


def _dot_nt(a, b):
    return lax.dot_general(a, b, (((1,), (1,)), ((), ())), preferred_element_type=F32)


def _rms(x, g):
    return x * lax.rsqrt(jnp.mean(x * x, axis=-1, keepdims=True) + RMS_EPS) * g


def _head_masks(rows):
    lane = lax.broadcasted_iota(jnp.int32, (rows, ATT_W), 1)
    return [(lane >= h * HEAD_DIM) & (lane < (h + 1) * HEAD_DIM) for h in range(HEADS)]


def _stack_heads(q, masks):
    return jnp.concatenate([jnp.where(m, q, 0.0) for m in masks], axis=0)


def _unstack_heads(x, masks, rows):
    out = jnp.where(masks[0], x[0:rows], 0.0)
    for h in range(1, HEADS):
        out = out + jnp.where(masks[h], x[h * rows:(h + 1) * rows], 0.0)
    return out


def _linear_scan(a, u, rows):
    row = lax.broadcasted_iota(jnp.int32, (rows, 1), 0)
    d = 1
    while d < rows:
        a_s = pltpu.roll(a, d, 0)
        u_s = pltpu.roll(u, d, 0)
        keep = row >= d
        u = jnp.where(keep, a * u_s + u, u)
        a = jnp.where(keep, a * a_s, a)
        d *= 2
    return a, u


def _conv_lru(ext_ref, rows, h0, yr, cw_ref, cb_ref, wa_ref, wi_ref, ba_ref, bi_ref, lam_ref):
    base = SUBLANES
    xc = cb_ref[...] + cw_ref[CONV_W - 1:CONV_W, :] * ext_ref[base:base + rows, :]
    for j in range(1, CONV_W):
        xc = xc + cw_ref[CONV_W - 1 - j:CONV_W - j, :] * ext_ref[base - j:base - j + rows, :]
    xcb = xc.astype(BF16)
    ra, ia = [], []
    for j in range(LRU_W // MXU_TILE):
        blk = xcb[:, j * MXU_TILE:(j + 1) * MXU_TILE]
        ra.append(_dot(blk, wa_ref[j]))
        ia.append(_dot(blk, wi_ref[j]))
    r = jax.nn.sigmoid(jnp.concatenate(ra, axis=-1) + ba_ref[...])
    i = jax.nn.sigmoid(jnp.concatenate(ia, axis=-1) + bi_ref[...])
    nl = -lam_ref[...]
    softplus = jnp.maximum(nl, 0.0) + jnp.log1p(jnp.exp(-jnp.abs(nl)))
    log_a = (-LRU_C * r) * softplus
    a = jnp.exp(log_a)
    u = jnp.sqrt(-jnp.tanh(log_a) * (a * a + 1.0)) * (i * xc)
    a_c, u_c = _linear_scan(a, u, rows)
    h = a_c * h0 + u_c
    return jax.nn.gelu(yr) * h, h


def _ffn_body(x_ref, g_ref, wgu_ref, wd_ref, *rest, final_norm):
    if final_norm:
        fg_ref, o_ref, act_ref = rest
    else:
        o_ref, act_ref = rest
    x = x_ref[...]
    hn = _rms(x, g_ref[...]).astype(BF16)
    for c in range(D_FF // FFN_CHUNK):
        lo = c * FFN_CHUNK
        gate = _dot(hn, wgu_ref[:, lo:lo + FFN_CHUNK])
        up = _dot(hn, wgu_ref[:, D_FF + lo:D_FF + lo + FFN_CHUNK])
        act_ref[:, lo:lo + FFN_CHUNK] = (gate * jax.nn.sigmoid(gate) * up).astype(BF16)
    y = x + 0.5 * _dot(act_ref[...], wd_ref[...])
    if final_norm:
        y = _rms(y, fg_ref[...])
    o_ref[...] = y


def _norm_mm_body(x_ref, g_ref, w_ref, o_ref):
    hn = _rms(x_ref[...], g_ref[...]).astype(BF16)
    o_ref[...] = _dot(hn, w_ref[...])


def _p_proj_body(x_ref, g_ref, w_ref, cw_ref, cb_ref, wa_ref, wi_ref, ba_ref, bi_ref, lam_ref,
                 q0, q1, q2, k0, k1, k2, v0, v1, v2,
                 pk0, pv0, pk1, pv1, pk2, pv2, yrec_ref, pconv_ref, plru_ref,
                 zs_ref, ext_ref, hc_ref):
    rows = PROJ_TILE
    t = pl.program_id(1)
    hn = _rms(x_ref[0], g_ref[...]).astype(BF16)
    outs = ((q0, q1, q2), (k0, k1, k2), (v0, v1, v2))
    wins = (None, (pk0, pk1, pk2), (pv0, pv1, pv2))
    for kind in range(3):
        for g, (window, dil) in enumerate(GROUPS):
            col = (kind * len(GROUPS) + g) * ATT_W
            z = _dot(hn, w_ref[:, col:col + ATT_W])
            if kind == 0:
                z = z * ATT_SCALE
            else:
                keep = min(window, rows)
                wins[kind][g][0] = z[rows - keep:, :]
            if dil == 1:
                outs[kind][g][0, 0] = z.astype(BF16)
            else:
                for c in range(ATT_W // LANES):
                    zs_ref[c] = z[:, c * LANES:(c + 1) * LANES]
                for r in range(dil):
                    sel = pl.ds(r, rows // dil, stride=dil)
                    outs[kind][g][0, r] = jnp.concatenate(
                        [zs_ref[c, sel, :] for c in range(ATT_W // LANES)], axis=-1).astype(BF16)

    xr = _dot(hn, w_ref[:, COL_XR:COL_XR + LRU_W])
    yr = _dot(hn, w_ref[:, COL_YR:COL_YR + LRU_W])

    @pl.when(t == 0)
    def _():
        ext_ref[0:SUBLANES, :] = jnp.zeros((SUBLANES, LRU_W), F32)
        hc_ref[...] = jnp.zeros((1, LRU_W), F32)

    ext_ref[SUBLANES:SUBLANES + rows, :] = xr
    y, h = _conv_lru(ext_ref, rows, hc_ref[...], yr, cw_ref, cb_ref, wa_ref, wi_ref, ba_ref, bi_ref, lam_ref)
    yrec_ref[0] = y.astype(BF16)
    hc_ref[...] = h[rows - 1:rows, :]
    plru_ref[0] = h[rows - 1:rows, :]
    pconv_ref[0] = ext_ref[SUBLANES + rows - (CONV_W - 1):SUBLANES + rows, :]
    ext_ref[0:SUBLANES, :] = ext_ref[rows:rows + SUBLANES, :]


def _p_attn_body(q0, q1, q2, k0, k1, k2, v0, v1, v2, kp0, kp1, kp2, vp0, vp1, vp2, bias_ref,
                 y_ref, kc0, kc1, kc2, vc0, vc1, vc2, o_scr, l_scr):
    n = pl.program_id(1)
    qs_, ks_, vs_ = (q0, q1, q2), (k0, k1, k2), (v0, v1, v2)
    kps, vps = (kp0, kp1, kp2), (vp0, vp1, vp2)
    kcs, vcs = (kc0, kc1, kc2), (vc0, vc1, vc2)
    masks = _head_masks(N_REL)
    halves = ATT_W // LANES
    col = lax.broadcasted_iota(jnp.int32, (HEADS * N_REL, 2 * N_REL), 1)
    for g, (window, dil) in enumerate(GROUPS):
        nblk = ATTN_TILE // dil // N_REL
        q_ref, kc, vc = qs_[g], kcs[g], vcs[g]
        kc[:, 0:N_REL, :] = kps[g][0]
        kc[:, N_REL:, :] = ks_[g][0]
        vc[:, 0:N_REL, :] = vps[g][0]
        vc[:, N_REL:, :] = vs_[g][0]

        def block(i, carry, g=g, dil=dil, nblk=nblk, q_ref=q_ref, kc=kc, vc=vc):
            r = i // nblk
            j = i % nblk
            row0 = pl.multiple_of(j * N_REL, N_REL)
            q = q_ref[0, r, pl.ds(row0, N_REL), :]
            kk = kc[r, pl.ds(row0, 2 * N_REL), :]
            vv = vc[r, pl.ds(row0, 2 * N_REL), :]
            s = _dot_nt(_stack_heads(q, masks), kk) + bias_ref[g]
            masked_cols = jnp.where(jnp.logical_and(n == 0, j == 0), N_REL, 0)
            s = jnp.where(col < masked_cols, -jnp.inf, s)
            m = jnp.max(s, axis=-1, keepdims=True)
            p = jnp.exp(s - m)
            l = jnp.sum(p, axis=-1, keepdims=True)
            o = _dot(p.astype(BF16), vv) / l
            lse = jnp.broadcast_to(m + jnp.log(l), o.shape)
            start = j * (N_REL * dil) + r
            if dil == 1:
                dst = pl.ds(pl.multiple_of(start, N_REL), N_REL)
            else:
                dst = pl.ds(start, N_REL, stride=dil)
            o_c = _unstack_heads(o, masks, N_REL)
            l_c = _unstack_heads(lse, masks, N_REL)
            for c in range(halves):
                o_scr[g * halves + c, dst, :] = o_c[:, c * LANES:(c + 1) * LANES]
                l_scr[g * halves + c, dst, :] = l_c[:, c * LANES:(c + 1) * LANES]
            return carry

        lax.fori_loop(0, ATTN_TILE // N_REL, block, 0)

    def combine(c, carry):
        sl = pl.ds(pl.multiple_of(c * COMBINE_ROWS, COMBINE_ROWS), COMBINE_ROWS)
        for c in range(halves):
            l0, l1, l2 = (l_scr[g * halves + c, sl, :] for g in range(len(GROUPS)))
            o0, o1, o2 = (o_scr[g * halves + c, sl, :] for g in range(len(GROUPS)))
            m = jnp.maximum(jnp.maximum(l0, l1), l2)
            w0, w1, w2 = jnp.exp(l0 - m), jnp.exp(l1 - m), jnp.exp(l2 - m)
            mix = (w0 * o0 + w1 * o1 + w2 * o2) / (w0 + w1 + w2)
            y_ref[0, sl, c * LANES:(c + 1) * LANES] = mix.astype(BF16)
        return carry

    lax.fori_loop(0, ATTN_TILE // COMBINE_ROWS, combine, 0)


def _mem_attention(qm, mk, mv):
    heads = []
    for h in range(MEM_HEADS):
        sl = slice(h * MEM_HEAD_DIM, (h + 1) * MEM_HEAD_DIM)
        s = _dot_nt(qm[:, sl].astype(BF16), mk[:, sl]) * MEM_SCALE
        m = jnp.max(s, axis=-1, keepdims=True)
        p = jnp.exp(s - m)
        l = jnp.sum(p, axis=-1, keepdims=True)
        heads.append(_dot(p.astype(BF16), mv[:, sl]) / l)
    return jnp.concatenate(heads, axis=-1)


def _merge_body(x_ref, ya_ref, yr_ref, *rest, fused_mem):
    if fused_mem:
        mk_ref, mv_ref, g_ref, wc_ref, bg_ref, wao_ref, wro_ref, wmo_ref, wout_ref, o_ref = rest
    else:
        ym_ref, g_ref, wc_ref, bg_ref, wao_ref, wro_ref, wmo_ref, wout_ref, o_ref = rest
    x = x_ref[0]
    hn = _rms(x, g_ref[...]).astype(BF16)
    if fused_mem:
        qm = _dot(hn, wc_ref[:, 0:MEM_W])
        ym = _mem_attention(qm, mk_ref[0], mv_ref[0]).astype(BF16)
        off = MEM_W
    else:
        ym = ym_ref[0]
        off = 0
    merged = None
    for idx, (y, w_ref) in enumerate(((ya_ref[0], wao_ref), (yr_ref[0], wro_ref), (ym, wmo_ref))):
        lo = idx * D_MODEL
        gate = jax.nn.sigmoid(_dot(hn, wc_ref[:, off + lo:off + lo + D_MODEL]) + bg_ref[:, lo:lo + D_MODEL])
        term = gate * _dot(y, w_ref[...])
        merged = term if merged is None else merged + term
    o_ref[0] = x + _dot(merged.astype(BF16), wout_ref[...])


def _s_mix_body(z_ref, ck0, cv0, ck1, cv1, ck2, cv2, sconv_ref, slru_ref, cmk_ref, cmv_ref,
                bc0, bc1, bc2, bn_ref, cw_ref, cb_ref, wa_ref, wi_ref, ba_ref, bi_ref, lam_ref,
                ya_ref, yrec_ref, ym_ref, ok0, ov0, ok1, ov1, ok2, ov2, oconv_ref, olru_ref,
                ext_ref, *, steps):
    cks, cvs = (ck0, ck1, ck2), (cv0, cv1, cv2)
    oks, ovs = (ok0, ok1, ok2), (ov0, ov1, ov2)
    bcs = (bc0, bc1, bc2)
    masks = _head_masks(steps)
    pad = jnp.zeros((N_REL - steps, ATT_W), F32)
    outs, lses = [], []
    for g, (window, dil) in enumerate(GROUPS):
        q = z_ref[0, :, g * ATT_W:(g + 1) * ATT_W] * ATT_SCALE
        kn = z_ref[0, :, (3 + g) * ATT_W:(4 + g) * ATT_W]
        vn = z_ref[0, :, (6 + g) * ATT_W:(7 + g) * ATT_W]
        qs = _stack_heads(q, masks).astype(BF16)
        kc = cks[g][0]
        vc = cvs[g][0]
        s_c = _dot_nt(qs, kc.astype(BF16)) + bcs[g][...]
        s_n = _dot_nt(qs, jnp.concatenate([kn, pad], axis=0).astype(BF16)) + bn_ref[g]
        m = jnp.maximum(jnp.max(s_c, axis=-1, keepdims=True), jnp.max(s_n, axis=-1, keepdims=True))
        p_c = jnp.exp(s_c - m)
        p_n = jnp.exp(s_n - m)
        l = jnp.sum(p_c, axis=-1, keepdims=True) + jnp.sum(p_n, axis=-1, keepdims=True)
        o = (_dot(p_c.astype(BF16), vc.astype(BF16))
             + _dot(p_n.astype(BF16), jnp.concatenate([vn, pad], axis=0).astype(BF16))) / l
        lse = jnp.broadcast_to(m + jnp.log(l), o.shape)
        outs.append(_unstack_heads(o, masks, steps))
        lses.append(_unstack_heads(lse, masks, steps))
        oks[g][0, 0:window - steps, :] = kc[steps:, :]
        oks[g][0, window - steps:, :] = kn
        ovs[g][0, 0:window - steps, :] = vc[steps:, :]
        ovs[g][0, window - steps:, :] = vn
    m = jnp.maximum(jnp.maximum(lses[0], lses[1]), lses[2])
    ws = [jnp.exp(l - m) for l in lses]
    ya_ref[0] = ((ws[0] * outs[0] + ws[1] * outs[1] + ws[2] * outs[2]) / (ws[0] + ws[1] + ws[2])).astype(BF16)

    ext_ref[0:SUBLANES, :] = jnp.zeros((SUBLANES, LRU_W), F32)
    ext_ref[SUBLANES - (CONV_W - 1):SUBLANES, :] = sconv_ref[0]
    ext_ref[SUBLANES:SUBLANES + steps, :] = z_ref[0, :, COL_XR:COL_XR + LRU_W]
    yr = z_ref[0, :, COL_YR:COL_YR + LRU_W]
    y, h = _conv_lru(ext_ref, steps, slru_ref[0], yr, cw_ref, cb_ref, wa_ref, wi_ref, ba_ref, bi_ref, lam_ref)
    yrec_ref[0] = y.astype(BF16)
    olru_ref[0] = h[steps - 1:steps, :]
    oconv_ref[0] = ext_ref[SUBLANES + steps - (CONV_W - 1):SUBLANES + steps, :]

    qm = z_ref[0, :, COL_QM:COL_QM + MEM_W]
    ym_ref[0] = _mem_attention(qm, cmk_ref[0].astype(BF16), cmv_ref[0].astype(BF16)).astype(BF16)


def _resident(shape):
    zeros = (0,) * len(shape)
    return pl.BlockSpec(shape, lambda *_: zeros, pipeline_mode=pl.Buffered(1))


def _params(*semantics):
    return pltpu.CompilerParams(dimension_semantics=semantics, vmem_limit_bytes=V7X_VMEM_LIMIT)


def _ffn(x, norm_g, w_gu, w_down, final_g=None):
    n = x.shape[0]
    tile = min(FFN_TILE, n)
    row = pl.BlockSpec((tile, D_MODEL), lambda i: (i, 0))
    in_specs = [row, _resident((1, D_MODEL)), _resident(w_gu.shape), _resident(w_down.shape)]
    args = [x, norm_g, w_gu, w_down]
    if final_g is not None:
        in_specs.append(_resident((1, D_MODEL)))
        args.append(final_g)
    return pl.pallas_call(
        functools.partial(_ffn_body, final_norm=final_g is not None),
        grid=(n // tile,),
        in_specs=in_specs,
        out_specs=row,
        out_shape=jax.ShapeDtypeStruct((n, D_MODEL), F32),
        scratch_shapes=[pltpu.VMEM((tile, D_FF), BF16)],
        compiler_params=_params("parallel"),
        name="ffn_final" if final_g is not None else "ffn",
    )(*args)


def _norm_mm(x, norm_g, w, row_tile, col_tile):
    n, cols = x.shape[0], w.shape[1]
    return pl.pallas_call(
        _norm_mm_body,
        grid=(n // row_tile, cols // col_tile),
        in_specs=[pl.BlockSpec((row_tile, D_MODEL), lambda i, j: (i, 0)),
                  _resident((1, D_MODEL)),
                  pl.BlockSpec((D_MODEL, col_tile), lambda i, j: (0, j))],
        out_specs=pl.BlockSpec((row_tile, col_tile), lambda i, j: (i, j)),
        out_shape=jax.ShapeDtypeStruct((n, cols), F32),
        compiler_params=_params("parallel", "parallel"),
        name="norm_mm",
    )(x, norm_g, w)


def _p_proj(x1, norm_g, w_a, lru):
    batch, seq, _ = x1.shape
    tile = PROJ_TILE
    nt = seq // tile
    qkv_shapes, qkv_specs = [], []
    for _kind in range(3):
        for window, dil in GROUPS:
            qkv_shapes.append(jax.ShapeDtypeStruct((batch, dil, seq // dil, ATT_W), BF16))
            qkv_specs.append(pl.BlockSpec((1, dil, tile // dil, ATT_W), lambda b, t: (b, 0, t, 0)))
    win_shapes, win_specs = [], []
    for window, dil in GROUPS:
        keep = min(window, seq)
        blk = min(keep, tile)
        first = nt - keep // blk if keep >= tile else 0
        for _kv in range(2):
            win_shapes.append(jax.ShapeDtypeStruct((batch, keep, ATT_W), F32))
            if keep >= tile:
                win_specs.append(pl.BlockSpec((1, blk, ATT_W),
                                              lambda b, t, first=first: (b, jnp.maximum(t - first, 0), 0)))
            else:
                win_specs.append(pl.BlockSpec((1, blk, ATT_W), lambda b, t: (b, 0, 0)))
    out_shapes = qkv_shapes + win_shapes + [
        jax.ShapeDtypeStruct((batch, seq, LRU_W), BF16),
        jax.ShapeDtypeStruct((batch, CONV_W - 1, LRU_W), F32),
        jax.ShapeDtypeStruct((batch, 1, LRU_W), F32)]
    out_specs = qkv_specs + win_specs + [
        pl.BlockSpec((1, tile, LRU_W), lambda b, t: (b, t, 0)),
        pl.BlockSpec((1, CONV_W - 1, LRU_W), lambda b, t: (b, 0, 0)),
        pl.BlockSpec((1, 1, LRU_W), lambda b, t: (b, 0, 0))]
    in_specs = [pl.BlockSpec((1, tile, D_MODEL), lambda b, t: (b, t, 0)),
                _resident((1, D_MODEL)), _resident(w_a.shape)] + [_resident(a.shape) for a in lru]
    return pl.pallas_call(
        _p_proj_body,
        grid=(batch, nt),
        in_specs=in_specs,
        out_specs=out_specs,
        out_shape=out_shapes,
        scratch_shapes=[pltpu.VMEM((ATT_W // LANES, tile, LANES), F32),
                        pltpu.VMEM((tile + SUBLANES, LRU_W), F32),
                        pltpu.VMEM((1, LRU_W), F32)],
        compiler_params=_params("parallel", "arbitrary"),
        name="p_proj",
    )(x1, norm_g, w_a, *lru)


def _p_attn(qkv, bias):
    batch = qkv[0].shape[0]
    seq = qkv[0].shape[1] * qkv[0].shape[2]
    tile = ATTN_TILE
    cur_specs, prev_specs, scratch = [], [], []
    for window, dil in GROUPS:
        rows = tile // dil
        cur_specs.append(pl.BlockSpec((1, dil, rows, ATT_W), lambda b, n: (b, 0, n, 0)))
        per = rows // N_REL
        prev_specs.append(pl.BlockSpec((1, dil, N_REL, ATT_W),
                                       lambda b, n, per=per: (b, 0, jnp.maximum(n * per - 1, 0), 0)))
        scratch.append(pltpu.VMEM((dil, N_REL + rows, ATT_W), BF16))
    in_specs = cur_specs * 3 + prev_specs * 2 + [_resident(bias.shape)]
    args = list(qkv) + list(qkv[3:6]) + list(qkv[6:9]) + [bias]
    return pl.pallas_call(
        _p_attn_body,
        grid=(batch, seq // tile),
        in_specs=in_specs,
        out_specs=pl.BlockSpec((1, tile, ATT_W), lambda b, n: (b, n, 0)),
        out_shape=jax.ShapeDtypeStruct((batch, seq, ATT_W), BF16),
        scratch_shapes=scratch * 2 + [pltpu.VMEM((len(GROUPS) * (ATT_W // LANES), tile, LANES), F32)] * 2,
        compiler_params=_params("parallel", "arbitrary"),
        name="p_attn",
    )(*args)


def _merge(x1, y_att, y_rec, mem, norm_g, w_c, b_gate, w_att_o, w_rec_o, w_mem_o, w_out, fused_mem):
    batch, seq, _ = x1.shape
    tile = min(MERGE_TILE, seq)

    def tok(width):
        return pl.BlockSpec((1, tile, width), lambda b, t: (b, t, 0))

    in_specs = [tok(D_MODEL), tok(ATT_W), tok(LRU_W)]
    if fused_mem:
        in_specs += [pl.BlockSpec((1, N_MEM, MEM_W), lambda b, t: (b, 0, 0))] * 2
    else:
        in_specs += [tok(MEM_W)]
    weights = [norm_g, w_c, b_gate, w_att_o, w_rec_o, w_mem_o, w_out]
    in_specs += [_resident(w.shape) for w in weights]
    return pl.pallas_call(
        functools.partial(_merge_body, fused_mem=fused_mem),
        grid=(batch, seq // tile),
        in_specs=in_specs,
        out_specs=tok(D_MODEL),
        out_shape=jax.ShapeDtypeStruct((batch, seq, D_MODEL), F32),
        compiler_params=_params("parallel", "parallel"),
        name="merge_p" if fused_mem else "merge_s",
    )(x1, y_att, y_rec, *mem, *weights)


def _s_mix(z, caches, state_conv, state_lru, cmk, cmv, bias_c, bias_n, lru):
    batch, steps, _ = z.shape

    def per_batch(shape):
        return pl.BlockSpec((1,) + tuple(shape[1:]), lambda b: (b,) + (0,) * (len(shape) - 1))

    ins = [z] + list(caches) + [state_conv, state_lru, cmk, cmv]
    in_specs = [per_batch(a.shape) for a in ins]
    consts = list(bias_c) + [bias_n] + list(lru)
    in_specs += [_resident(a.shape) for a in consts]
    out_shapes = [jax.ShapeDtypeStruct((batch, steps, ATT_W), BF16),
                  jax.ShapeDtypeStruct((batch, steps, LRU_W), BF16),
                  jax.ShapeDtypeStruct((batch, steps, MEM_W), BF16)]
    out_shapes += [jax.ShapeDtypeStruct(c.shape, F32) for c in caches]
    out_shapes += [jax.ShapeDtypeStruct(state_conv.shape, F32), jax.ShapeDtypeStruct(state_lru.shape, F32)]
    return pl.pallas_call(
        functools.partial(_s_mix_body, steps=steps),
        grid=(batch,),
        in_specs=in_specs,
        out_specs=[per_batch(s.shape) for s in out_shapes],
        out_shape=out_shapes,
        scratch_shapes=[pltpu.VMEM((2 * SUBLANES, LRU_W), F32)],
        compiler_params=_params("parallel"),
        name="s_mix",
    )(*ins, *consts)


def _bucket(n):
    n = np.maximum(n, 0)
    max_exact = NUM_BUCKETS // 2
    nf = np.maximum(n, 1).astype(np.float32)
    large = max_exact + (np.log(nf / max_exact) / np.float32(math.log(MAX_DISTANCE / max_exact))
                         * (NUM_BUCKETS - max_exact)).astype(np.int32)
    return np.where(n < max_exact, n, np.minimum(large, NUM_BUCKETS - 1))


def _bias_table(rel_bias, g, dist, valid):
    tab = rel_bias[:, g * HEADS:(g + 1) * HEADS][_bucket(dist)]
    tab = jnp.where(valid[:, :, None], tab, -jnp.inf)
    return jnp.transpose(tab, (2, 0, 1)).reshape(HEADS * dist.shape[0], dist.shape[1]).astype(F32)


def _prompt_bias(rel_bias):
    qi = np.arange(N_REL)[:, None]
    ki = np.arange(2 * N_REL)[None, :]
    delta = qi - ki + N_REL
    valid = (delta >= 0) & (delta <= N_REL)
    return jnp.stack([_bias_table(rel_bias, g, dil * delta, valid) for g, (_, dil) in enumerate(GROUPS)])


def _sample_bias(rel_bias, steps):
    t = np.arange(steps)[:, None]
    cache, new = [], []
    for g, (window, dil) in enumerate(GROUPS):
        dist = window + t - np.arange(window)[None, :]
        cache.append(_bias_table(rel_bias, g, dist, (dist % dil == 0) & (dist <= window)))
        j = np.arange(N_REL)[None, :]
        dist = t - j
        new.append(_bias_table(rel_bias, g, dist, (j < steps) & (dist >= 0) & (dist % dil == 0)))
    return cache, jnp.stack(new)


def _block_diag(w):
    per = MXU_TILE // LRU_BLOCK
    w4 = w.reshape(-1, per, LRU_BLOCK, LRU_BLOCK)
    bd = jnp.einsum('jnef,nm->jnemf', w4, jnp.eye(per, dtype=w.dtype))
    return bd.reshape(-1, MXU_TILE, MXU_TILE).astype(BF16)


def kernel(x_prompt, x_sample, mem_prompt, cache_win_k0, cache_win_v0, cache_win_k1, cache_win_v1, cache_win_k2, cache_win_v2, state_conv, state_lru, cache_mem_k, cache_mem_v, rel_bias, ffn1_norm, ffn1_w_gu, ffn1_w_down, mix_norm, w_in, b_gate, conv_w, conv_b, lru_w_a, lru_b_a, lru_w_i, lru_b_i, lru_lambda, mem_norm, w_mem_kv, w_att_o, w_rec_o, w_mem_o, w_out, ffn2_norm, ffn2_w_gu, ffn2_w_down, final_norm):
    depth = ffn1_norm.shape[0]
    assert depth == 1
    l = 0
    batch, seq, _ = x_prompt.shape
    sbatch, steps, _ = x_sample.shape
    row = lambda v: v.reshape(1, -1)

    w_in_b = w_in[l].astype(BF16)
    lru = (conv_w[l], row(conv_b[l]), _block_diag(lru_w_a[l]), _block_diag(lru_w_i[l]),
           row(lru_b_a[l]), row(lru_b_i[l]), row(lru_lambda[l]))
    ffn1 = (row(ffn1_norm[l]), ffn1_w_gu[l].astype(BF16), ffn1_w_down[l].astype(BF16))
    ffn2 = (row(ffn2_norm[l]), ffn2_w_gu[l].astype(BF16), ffn2_w_down[l].astype(BF16))
    outw = (row(b_gate[l]), w_att_o[l].astype(BF16), w_rec_o[l].astype(BF16), w_mem_o[l].astype(BF16),
            w_out[l].astype(BF16))
    mixg = row(mix_norm[l])
    fin = row(final_norm)

    xp1 = _ffn(x_prompt.reshape(batch * seq, D_MODEL), *ffn1).reshape(batch, seq, D_MODEL)
    mkv = _norm_mm(mem_prompt.reshape(batch * N_MEM, D_MODEL), row(mem_norm[l]), w_mem_kv[l].astype(BF16),
                   row_tile=512, col_tile=2 * MEM_W).reshape(batch, N_MEM, 2 * MEM_W)
    p_mk, p_mv = mkv[:, :, :MEM_W], mkv[:, :, MEM_W:]
    proj = _p_proj(xp1, mixg, w_in_b[:, :COL_QM], lru)
    qkv, wins = proj[0:9], proj[9:15]
    y_rec, p_conv, p_lru = proj[15], proj[16], proj[17]
    y_att = _p_attn(qkv, _prompt_bias(rel_bias))
    xp2 = _merge(xp1, y_att, y_rec, (p_mk.astype(BF16), p_mv.astype(BF16)), mixg, w_in_b[:, COL_QM:],
                 *outw, fused_mem=True)
    y_prompt = _ffn(xp2.reshape(batch * seq, D_MODEL), *ffn2, final_g=fin).reshape(batch, seq, D_MODEL)

    n_s = sbatch * steps
    xs1 = _ffn(x_sample.reshape(n_s, D_MODEL), *ffn1)
    z = _norm_mm(xs1, mixg, w_in_b, row_tile=n_s, col_tile=IN_COLS // 2).reshape(sbatch, steps, IN_COLS)
    caches = [c[l].reshape(sbatch, c.shape[2], ATT_W) for c in
              (cache_win_k0, cache_win_v0, cache_win_k1, cache_win_v1, cache_win_k2, cache_win_v2)]
    bias_c, bias_n = _sample_bias(rel_bias, steps)
    mix = _s_mix(z, caches, state_conv[l], state_lru[l].reshape(sbatch, 1, LRU_W),
                 cache_mem_k[l].reshape(sbatch, N_MEM, MEM_W), cache_mem_v[l].reshape(sbatch, N_MEM, MEM_W),
                 bias_c, bias_n, lru)
    s_att, s_rec, s_mem = (a.reshape(1, n_s, a.shape[-1]) for a in mix[0:3])
    xs2 = _merge(xs1.reshape(1, n_s, D_MODEL), s_att, s_rec, (s_mem,), mixg, w_in_b[:, COL_GATE:],
                 *outw, fused_mem=False)
    y_sample = _ffn(xs2.reshape(n_s, D_MODEL), *ffn2, final_g=fin).reshape(sbatch, steps, D_MODEL)

    def heads(a, n_heads, dim):
        return a.reshape(1, a.shape[0], a.shape[1], n_heads, dim)

    p_states = [heads(w, HEADS, HEAD_DIM) for w in wins]
    s_states = [heads(w, HEADS, HEAD_DIM) for w in mix[3:9]]
    return (y_prompt, y_sample, *p_states,
            p_conv[None], p_lru.reshape(1, batch, LRU_W),
            heads(p_mk, MEM_HEADS, MEM_HEAD_DIM), heads(p_mv, MEM_HEADS, MEM_HEAD_DIM),
            *s_states, mix[9][None], mix[10].reshape(1, sbatch, LRU_W))
```

```python
import functools
import math

import numpy as np
import jax
import jax.numpy as jnp
from jax import lax
from jax.experimental import pallas as pl
from jax.experimental.pallas import tpu as pltpu

F32 = jnp.float32
BF16 = jnp.bfloat16

D_MODEL = 1024
D_FF = 2816
HEAD_DIM = 64
HEADS = 4
GROUPS = ((128, 1), (512, 4), (2048, 16))
N_REL = 128
ATT_W = HEADS * HEAD_DIM
ATT_SCALE = HEAD_DIM ** -0.5
NUM_BUCKETS = 32
MAX_DISTANCE = 2048
LRU_W = 768
LRU_BLOCK = 64
LRU_C = 8.0
CONV_W = 4
N_MEM = 256
MEM_HEADS = 4
MEM_HEAD_DIM = 128
MEM_W = MEM_HEADS * MEM_HEAD_DIM
MEM_SCALE = MEM_HEAD_DIM ** -0.5
RMS_EPS = 1e-6
QKV_W = 3 * len(GROUPS) * ATT_W
COL_XR = QKV_W
COL_YR = COL_XR + LRU_W
COL_QM = COL_YR + LRU_W
COL_GATE = COL_QM + MEM_W
IN_COLS = COL_GATE + 3 * D_MODEL

V7X_VMEM_LIMIT = 56 * 1024 * 1024
MXU_TILE = 256
SUBLANES = 8
LANES = 128
FFN_TILE = 512
FFN_CHUNK = 256
PROJ_TILE = 512
SCAN_SEGS = SUBLANES
SCAN_STEPS = PROJ_TILE // SCAN_SEGS
SCAN_PITCH = SCAN_STEPS + SUBLANES
ATTN_TILE = 2048
MERGE_TILE = 512
COMBINE_ROWS = 256
LRU_CHUNKS = LRU_W // LANES
ATT_CHUNKS = ATT_W // LANES


def _dot(a, b):
    return jnp.dot(a, b, preferred_element_type=F32)


def _dot_nt(a, b):
    return lax.dot_general(a, b, (((1,), (1,)), ((), ())), preferred_element_type=F32)


def _rms(x, g):
    return x * lax.rsqrt(jnp.mean(x * x, axis=-1, keepdims=True) + RMS_EPS) * g


def _stack_heads(q):
    lane = lax.broadcasted_iota(jnp.int32, q.shape, 1)
    return jnp.concatenate(
        [jnp.where((lane >= h * HEAD_DIM) & (lane < (h + 1) * HEAD_DIM), q, 0.0) for h in range(HEADS)], axis=0)


def _unstack_heads(x, rows):
    low = lax.broadcasted_iota(jnp.int32, (rows, LANES), 1) < HEAD_DIM
    per = LANES // HEAD_DIM
    return [jnp.where(low,
                      x[(per * c) * rows:(per * c + 1) * rows, c * LANES:(c + 1) * LANES],
                      x[(per * c + 1) * rows:(per * c + 2) * rows, c * LANES:(c + 1) * LANES])
            for c in range(ATT_CHUNKS)]


def _linear_scan(a, u, rows):
    row = lax.broadcasted_iota(jnp.int32, (rows, 1), 0)
    d = 1
    while d < rows:
        a_s = pltpu.roll(a, d, 0)
        u_s = pltpu.roll(u, d, 0)
        keep = row >= d
        u = jnp.where(keep, a * u_s + u, u)
        a = jnp.where(keep, a * a_s, a)
        d *= 2
    return a, u


def _lru_gates(xc, wa_ref, wi_ref, ba_ref, bi_ref, lam_ref):
    xcb = xc.astype(BF16)
    ra, ia = [], []
    for j in range(LRU_W // MXU_TILE):
        blk = xcb[:, j * MXU_TILE:(j + 1) * MXU_TILE]
        ra.append(_dot(blk, wa_ref[j]))
        ia.append(_dot(blk, wi_ref[j]))
    r = jax.nn.sigmoid(jnp.concatenate(ra, axis=-1) + ba_ref[...])
    i = jax.nn.sigmoid(jnp.concatenate(ia, axis=-1) + bi_ref[...])
    nl = -lam_ref[...]
    softplus = jnp.maximum(nl, 0.0) + jnp.log1p(jnp.exp(-jnp.abs(nl)))
    log_a = (-LRU_C * r) * softplus
    a = jnp.exp(log_a)
    u = jnp.sqrt(-jnp.tanh(log_a) * (a * a + 1.0)) * (i * xc)
    return a, u


def _conv_taps(cw_ref, cb_ref, taps):
    xc = cb_ref[...] + cw_ref[CONV_W - 1:CONV_W, :] * taps[0]
    for j in range(1, CONV_W):
        xc = xc + cw_ref[CONV_W - 1 - j:CONV_W - j, :] * taps[j]
    return xc


def _mem_attention(qm, mk_heads, mv_heads):
    heads = []
    for h in range(MEM_HEADS):
        q = qm[:, h * MEM_HEAD_DIM:(h + 1) * MEM_HEAD_DIM].astype(BF16)
        s = _dot_nt(q, mk_heads[h]) * MEM_SCALE
        m = jnp.max(s, axis=-1, keepdims=True)
        p = jnp.exp(s - m)
        l = jnp.sum(p, axis=-1, keepdims=True)
        heads.append(_dot(p.astype(BF16), mv_heads[h]) * (1.0 / l))
    return jnp.concatenate(heads, axis=-1)


def _ffn_body(x_ref, g_ref, wgu_ref, wd_ref, *rest, final_norm):
    if final_norm:
        fg_ref, o_ref, act_ref = rest
    else:
        o_ref, act_ref = rest
    x = x_ref[...]
    hn = _rms(x, g_ref[...]).astype(BF16)
    for c in range(D_FF // FFN_CHUNK):
        lo = c * FFN_CHUNK
        gate = _dot(hn, wgu_ref[:, lo:lo + FFN_CHUNK])
        up = _dot(hn, wgu_ref[:, D_FF + lo:D_FF + lo + FFN_CHUNK])
        act_ref[:, lo:lo + FFN_CHUNK] = (gate * jax.nn.sigmoid(gate) * up).astype(BF16)
    y = x + 0.5 * _dot(act_ref[...], wd_ref[...])
    if final_norm:
        y = _rms(y, fg_ref[...])
    o_ref[...] = y


def _norm_mm_body(x_ref, g_ref, w_ref, o_ref):
    hn = _rms(x_ref[...], g_ref[...]).astype(BF16)
    o_ref[...] = _dot(hn, w_ref[...])


def _mem_kv_body(x_ref, g_ref, w_ref, mk_ref, mv_ref, mkv_ref):
    rows = x_ref.shape[0]
    z = _dot(_rms(x_ref[...], g_ref[...]).astype(BF16), w_ref[...])
    mkv_ref[...] = z.astype(BF16)
    for h in range(MEM_HEADS):
        dst = pl.ds(h, rows, stride=MEM_HEADS)
        mk_ref[dst, :] = z[:, h * MEM_HEAD_DIM:(h + 1) * MEM_HEAD_DIM]
        mv_ref[dst, :] = z[:, MEM_W + h * MEM_HEAD_DIM:MEM_W + (h + 1) * MEM_HEAD_DIM]


def _p_proj_body(x_ref, g_ref, w_ref, cw_ref, cb_ref, wa_ref, wi_ref, ba_ref, bi_ref, lam_ref,
                 q0, q1, q2, k0, k1, k2, v0, v1, v2,
                 pk0, pv0, pk1, pv1, pk2, pv2, yrec_ref, pconv_ref, plru_ref,
                 zs_ref, xpad_ref, hpad_ref, hc_ref):
    rows = PROJ_TILE
    t = pl.program_id(1)
    tail = slice(SCAN_STEPS - SUBLANES, SCAN_STEPS)

    @pl.when(t == 0)
    def _():
        xpad_ref[:, tail, :] = jnp.zeros((LRU_CHUNKS, SUBLANES, LANES), F32)
        hc_ref[...] = jnp.zeros((1, LRU_W), F32)

    hn = _rms(x_ref[0], g_ref[...]).astype(BF16)
    xr = _dot(hn, w_ref[:, COL_XR:COL_XR + LRU_W])
    for c in range(LRU_CHUNKS):
        for s in range(SCAN_SEGS):
            lo = SCAN_PITCH * (s + 1)
            xpad_ref[c, lo:lo + SCAN_STEPS, :] = xr[s * SCAN_STEPS:(s + 1) * SCAN_STEPS, c * LANES:(c + 1) * LANES]

    outs = ((q0, q1, q2), (k0, k1, k2), (v0, v1, v2))
    wins = (None, (pk0, pk1, pk2), (pv0, pv1, pv2))

    def project(kind):
        for g, (window, dil) in enumerate(GROUPS):
            col = (kind * len(GROUPS) + g) * ATT_W
            z = _dot(hn, w_ref[:, col:col + ATT_W])
            if kind == 0:
                z = z * ATT_SCALE
            else:
                keep = min(window, rows)
                wins[kind][g][0] = jnp.transpose(z[rows - keep:, :])
            if dil == 1:
                outs[kind][g][0, 0] = z.astype(BF16)
            else:
                stage = zs_ref.at[kind * (len(GROUPS) - 1) + g - 1]
                for c in range(ATT_CHUNKS):
                    stage[c] = z[:, c * LANES:(c + 1) * LANES]
                for r in range(dil):
                    sel = pl.ds(r, rows // dil, stride=dil)
                    outs[kind][g][0, r] = jnp.concatenate(
                        [stage[c, sel, :] for c in range(ATT_CHUNKS)], axis=-1).astype(BF16)

    project(0)

    def step_rows(i):
        start = SCAN_PITCH + i if i >= 0 else SCAN_STEPS + i
        sel = pl.ds(start, SCAN_SEGS, stride=SCAN_PITCH)
        return jnp.concatenate([xpad_ref[c, sel, :] for c in range(LRU_CHUNKS)], axis=-1)

    hist = CONV_W - 1
    xp = jnp.concatenate([step_rows(i) for i in range(-hist, SCAN_STEPS)], axis=0)
    n = SCAN_STEPS * SCAN_SEGS
    taps = [xp[(hist - j) * SCAN_SEGS:(hist - j) * SCAN_SEGS + n, :] for j in range(CONV_W)]
    a, u = _lru_gates(_conv_taps(cw_ref, cb_ref, taps), wa_ref, wi_ref, ba_ref, bi_ref, lam_ref)

    project(1)
    project(2)
    yr = _dot(hn, w_ref[:, COL_YR:COL_YR + LRU_W])

    def rows_of(v, i):
        return v[i * SCAN_SEGS:(i + 1) * SCAN_SEGS, :]

    h = jnp.zeros((SCAN_SEGS, LRU_W), F32)
    prod = jnp.ones((SCAN_SEGS, LRU_W), F32)
    for i in range(SCAN_STEPS):
        a_i = rows_of(a, i)
        h = a_i * h + rows_of(u, i)
        prod = a_i * prod
    prod_c, h_c = _linear_scan(prod, h, SCAN_SEGS)
    h_end = prod_c * hc_ref[...] + h_c
    seg = lax.broadcasted_iota(jnp.int32, (SCAN_SEGS, 1), 0)
    h = jnp.where(seg == 0, hc_ref[...], pltpu.roll(h_end, 1, 0))
    for i in range(SCAN_STEPS):
        h = rows_of(a, i) * h + rows_of(u, i)
        dst = pl.ds(i, SCAN_SEGS, stride=SCAN_PITCH)
        for c in range(LRU_CHUNKS):
            hpad_ref[c, dst, :] = h[:, c * LANES:(c + 1) * LANES]
    hs = jnp.concatenate(
        [jnp.concatenate([hpad_ref[c, SCAN_PITCH * s:SCAN_PITCH * s + SCAN_STEPS, :] for c in range(LRU_CHUNKS)],
                         axis=-1) for s in range(SCAN_SEGS)], axis=0)
    yrec_ref[0] = (jax.nn.gelu(yr) * hs).astype(BF16)
    hc_ref[...] = h_end[SCAN_SEGS - 1:SCAN_SEGS, :]
    plru_ref[0] = h_end[SCAN_SEGS - 1:SCAN_SEGS, :]
    pconv_ref[0] = xr[rows - hist:rows, :]
    for c in range(LRU_CHUNKS):
        xpad_ref[c, tail, :] = xr[rows - SUBLANES:rows, c * LANES:(c + 1) * LANES]


def _p_attn_body(q0, q1, q2, k0, k1, k2, v0, v1, v2, kp0, kp1, kp2, vp0, vp1, vp2, bias_ref,
                 y_ref, kc0, kc1, kc2, vc0, vc1, vc2, o_scr, l_scr):
    n = pl.program_id(1)
    qs_, ks_, vs_ = (q0, q1, q2), (k0, k1, k2), (v0, v1, v2)
    kps, vps = (kp0, kp1, kp2), (vp0, vp1, vp2)
    kcs, vcs = (kc0, kc1, kc2), (vc0, vc1, vc2)
    col = lax.broadcasted_iota(jnp.int32, (HEADS * N_REL, 2 * N_REL), 1)
    for g, (window, dil) in enumerate(GROUPS):
        nblk = ATTN_TILE // dil // N_REL
        q_ref, kc, vc = qs_[g], kcs[g], vcs[g]
        kc[:, 0:N_REL, :] = kps[g][0]
        kc[:, N_REL:, :] = ks_[g][0]
        vc[:, 0:N_REL, :] = vps[g][0]
        vc[:, N_REL:, :] = vs_[g][0]

        def block(i, carry, g=g, dil=dil, nblk=nblk, q_ref=q_ref, kc=kc, vc=vc):
            r = i // nblk
            j = i % nblk
            row0 = pl.multiple_of(j * N_REL, N_REL)
            q = q_ref[0, r, pl.ds(row0, N_REL), :]
            kk = kc[r, pl.ds(row0, 2 * N_REL), :]
            vv = vc[r, pl.ds(row0, 2 * N_REL), :]
            s = _dot_nt(_stack_heads(q), kk) + bias_ref[g]
            masked_cols = jnp.where(jnp.logical_and(n == 0, j == 0), N_REL, 0)
            s = jnp.where(col < masked_cols, -jnp.inf, s)
            m = jnp.max(s, axis=-1, keepdims=True)
            p = jnp.exp(s - m)
            l = jnp.sum(p, axis=-1, keepdims=True)
            o = _dot(p.astype(BF16), vv) * (1.0 / l)
            lse = jnp.broadcast_to(m + jnp.log(l), o.shape)
            start = j * (N_REL * dil) + r
            if dil == 1:
                dst = pl.ds(pl.multiple_of(start, N_REL), N_REL)
            else:
                dst = pl.ds(start, N_REL, stride=dil)
            o_c = _unstack_heads(o, N_REL)
            l_c = _unstack_heads(lse, N_REL)
            for c in range(ATT_CHUNKS):
                o_scr[g * ATT_CHUNKS + c, dst, :] = o_c[c]
                l_scr[g * ATT_CHUNKS + c, dst, :] = l_c[c]
            return carry

        lax.fori_loop(0, ATTN_TILE // N_REL, block, 0, unroll=2)

    def combine(c, carry):
        sl = pl.ds(pl.multiple_of(c * COMBINE_ROWS, COMBINE_ROWS), COMBINE_ROWS)
        for k in range(ATT_CHUNKS):
            l0, l1, l2 = (l_scr[g * ATT_CHUNKS + k, sl, :] for g in range(len(GROUPS)))
            o0, o1, o2 = (o_scr[g * ATT_CHUNKS + k, sl, :] for g in range(len(GROUPS)))
            m = jnp.maximum(jnp.maximum(l0, l1), l2)
            w0, w1, w2 = jnp.exp(l0 - m), jnp.exp(l1 - m), jnp.exp(l2 - m)
            mix = (w0 * o0 + w1 * o1 + w2 * o2) / (w0 + w1 + w2)
            y_ref[0, sl, k * LANES:(k + 1) * LANES] = mix.astype(BF16)
        return carry

    lax.fori_loop(0, ATTN_TILE // COMBINE_ROWS, combine, 0)


def _merge_body(x_ref, ya_ref, yr_ref, *rest, fused_mem):
    if fused_mem:
        mkv_ref, g_ref, wc_ref, bg_ref, wao_ref, wro_ref, wmo_ref, wout_ref, o_ref = rest
    else:
        ym_ref, g_ref, wc_ref, bg_ref, wao_ref, wro_ref, wmo_ref, wout_ref, o_ref = rest
    x = x_ref[0]
    hn = _rms(x, g_ref[...]).astype(BF16)
    if fused_mem:
        qm = _dot(hn, wc_ref[:, 0:MEM_W])
        mk = [mkv_ref[0, :, h * MEM_HEAD_DIM:(h + 1) * MEM_HEAD_DIM] for h in range(MEM_HEADS)]
        mv = [mkv_ref[0, :, MEM_W + h * MEM_HEAD_DIM:MEM_W + (h + 1) * MEM_HEAD_DIM] for h in range(MEM_HEADS)]
        ym = _mem_attention(qm, mk, mv).astype(BF16)
        off = MEM_W
    else:
        ym = ym_ref[0]
        off = 0
    merged = None
    for idx, (y, w_ref) in enumerate(((ya_ref[0], wao_ref), (yr_ref[0], wro_ref), (ym, wmo_ref))):
        lo = idx * D_MODEL
        gate = jax.nn.sigmoid(_dot(hn, wc_ref[:, off + lo:off + lo + D_MODEL]) + bg_ref[:, lo:lo + D_MODEL])
        term = gate * _dot(y, w_ref[...])
        merged = term if merged is None else merged + term
    o_ref[0] = x + _dot(merged.astype(BF16), wout_ref[...])


def _s_mix_body(z_ref, ck0, cv0, ck1, cv1, ck2, cv2, sconv_ref, slru_ref, cmk_ref, cmv_ref,
                bc0, bc1, bc2, bn_ref, cw_ref, cb_ref, wa_ref, wi_ref, ba_ref, bi_ref, lam_ref,
                ya_ref, yrec_ref, ym_ref, ok0, ov0, ok1, ov1, ok2, ov2, oconv_ref, olru_ref,
                ext_ref, *, steps):
    cks, cvs = (ck0, ck1, ck2), (cv0, cv1, cv2)
    oks, ovs = (ok0, ok1, ok2), (ov0, ov1, ov2)
    bcs = (bc0, bc1, bc2)
    zeros = jnp.zeros((N_REL - steps, ATT_W), F32)
    new_lanes = lax.broadcasted_iota(jnp.int32, (ATT_W, LANES), 1) >= LANES - steps
    outs, lses = [], []
    for g, (window, dil) in enumerate(GROUPS):
        q = z_ref[0, :, g * ATT_W:(g + 1) * ATT_W] * ATT_SCALE
        kn = z_ref[0, :, (3 + g) * ATT_W:(4 + g) * ATT_W]
        vn = z_ref[0, :, (6 + g) * ATT_W:(7 + g) * ATT_W]
        qs = _stack_heads(q).astype(BF16)
        kt = cks[g][0]
        vt = cvs[g][0]
        s_c = _dot(qs, kt.astype(BF16)) + bcs[g][...]
        s_n = _dot_nt(qs, jnp.concatenate([kn, zeros], axis=0).astype(BF16)) + bn_ref[g]
        m = jnp.maximum(jnp.max(s_c, axis=-1, keepdims=True), jnp.max(s_n, axis=-1, keepdims=True))
        p_c = jnp.exp(s_c - m)
        p_n = jnp.exp(s_n - m)
        l = jnp.sum(p_c, axis=-1, keepdims=True) + jnp.sum(p_n, axis=-1, keepdims=True)
        o = (_dot_nt(p_c.astype(BF16), vt.astype(BF16))
             + _dot(p_n.astype(BF16), jnp.concatenate([vn, zeros], axis=0).astype(BF16))) * (1.0 / l)
        lse = jnp.broadcast_to(m + jnp.log(l), o.shape)
        outs.append(jnp.concatenate(_unstack_heads(o, steps), axis=-1))
        lses.append(jnp.concatenate(_unstack_heads(lse, steps), axis=-1))
        for old, new, dst in ((kt, kn, oks[g]), (vt, vn, ovs[g])):
            rolled = pltpu.roll(old, window - steps, 1)
            new_t = jnp.transpose(jnp.concatenate([zeros, new], axis=0))
            if window > LANES:
                dst[0, :, 0:window - LANES] = rolled[:, 0:window - LANES]
            dst[0, :, window - LANES:window] = jnp.where(new_lanes, new_t, rolled[:, window - LANES:window])
    m = jnp.maximum(jnp.maximum(lses[0], lses[1]), lses[2])
    ws = [jnp.exp(l - m) for l in lses]
    ya_ref[0] = ((ws[0] * outs[0] + ws[1] * outs[1] + ws[2] * outs[2]) / (ws[0] + ws[1] + ws[2])).astype(BF16)

    hist = CONV_W - 1
    ext_ref[0:SUBLANES, :] = jnp.zeros((SUBLANES, LRU_W), F32)
    ext_ref[SUBLANES - hist:SUBLANES, :] = sconv_ref[0]
    ext_ref[SUBLANES:SUBLANES + steps, :] = z_ref[0, :, COL_XR:COL_XR + LRU_W]
    taps = [ext_ref[SUBLANES - j:SUBLANES - j + steps, :] for j in range(CONV_W)]
    a, u = _lru_gates(_conv_taps(cw_ref, cb_ref, taps), wa_ref, wi_ref, ba_ref, bi_ref, lam_ref)
    a_c, u_c = _linear_scan(a, u, steps)
    h = a_c * slru_ref[0] + u_c
    yrec_ref[0] = (jax.nn.gelu(z_ref[0, :, COL_YR:COL_YR + LRU_W]) * h).astype(BF16)
    olru_ref[0] = h[steps - 1:steps, :]
    oconv_ref[0] = ext_ref[SUBLANES + steps - hist:SUBLANES + steps, :]

    qm = z_ref[0, :, COL_QM:COL_QM + MEM_W]
    mk = [cmk_ref[0, pl.ds(h, N_MEM, stride=MEM_HEADS), :].astype(BF16) for h in range(MEM_HEADS)]
    mv = [cmv_ref[0, pl.ds(h, N_MEM, stride=MEM_HEADS), :].astype(BF16) for h in range(MEM_HEADS)]
    ym_ref[0] = _mem_attention(qm, mk, mv).astype(BF16)


def _resident(shape):
    zeros = (0,) * len(shape)
    return pl.BlockSpec(shape, lambda *_: zeros, pipeline_mode=pl.Buffered(1))


def _params(*semantics):
    return pltpu.CompilerParams(dimension_semantics=semantics, vmem_limit_bytes=V7X_VMEM_LIMIT)


def _ffn(x, norm_g, w_gu, w_down, final_g=None):
    n = x.shape[0]
    tile = min(FFN_TILE, n)
    row = pl.BlockSpec((tile, D_MODEL), lambda i: (i, 0))
    in_specs = [row, _resident((1, D_MODEL)), _resident(w_gu.shape), _resident(w_down.shape)]
    args = [x, norm_g, w_gu, w_down]
    if final_g is not None:
        in_specs.append(_resident((1, D_MODEL)))
        args.append(final_g)
    return pl.pallas_call(
        functools.partial(_ffn_body, final_norm=final_g is not None),
        grid=(n // tile,),
        in_specs=in_specs,
        out_specs=row,
        out_shape=jax.ShapeDtypeStruct((n, D_MODEL), F32),
        scratch_shapes=[pltpu.VMEM((tile, D_FF), BF16)],
        compiler_params=_params("parallel"),
        name="ffn_final" if final_g is not None else "ffn",
    )(*args)


def _norm_mm(x, norm_g, w, row_tile, col_tile):
    n, cols = x.shape[0], w.shape[1]
    return pl.pallas_call(
        _norm_mm_body,
        grid=(n // row_tile, cols // col_tile),
        in_specs=[pl.BlockSpec((row_tile, D_MODEL), lambda i, j: (i, 0)),
                  _resident((1, D_MODEL)),
                  pl.BlockSpec((D_MODEL, col_tile), lambda i, j: (0, j))],
        out_specs=pl.BlockSpec((row_tile, col_tile), lambda i, j: (i, j)),
        out_shape=jax.ShapeDtypeStruct((n, cols), F32),
        compiler_params=_params("parallel", "parallel"),
        name="norm_mm",
    )(x, norm_g, w)


def _mem_kv(mem, norm_g, w_kv, row_tile):
    n = mem.shape[0]
    state = jax.ShapeDtypeStruct((n * MEM_HEADS, MEM_HEAD_DIM), F32)
    state_spec = pl.BlockSpec((row_tile * MEM_HEADS, MEM_HEAD_DIM), lambda i: (i, 0))
    return pl.pallas_call(
        _mem_kv_body,
        grid=(n // row_tile,),
        in_specs=[pl.BlockSpec((row_tile, D_MODEL), lambda i: (i, 0)),
                  _resident((1, D_MODEL)), _resident(w_kv.shape)],
        out_specs=[state_spec, state_spec, pl.BlockSpec((row_tile, 2 * MEM_W), lambda i: (i, 0))],
        out_shape=[state, state, jax.ShapeDtypeStruct((n, 2 * MEM_W), BF16)],
        compiler_params=_params("parallel"),
        name="mem_kv",
    )(mem, norm_g, w_kv)


def _p_proj(x1, norm_g, w_a, lru):
    batch, seq, _ = x1.shape
    tile = PROJ_TILE
    nt = seq // tile
    qkv_shapes, qkv_specs = [], []
    for _kind in range(3):
        for window, dil in GROUPS:
            qkv_shapes.append(jax.ShapeDtypeStruct((batch, dil, seq // dil, ATT_W), BF16))
            qkv_specs.append(pl.BlockSpec((1, dil, tile // dil, ATT_W), lambda b, t: (b, 0, t, 0)))
    win_shapes, win_specs = [], []
    for window, dil in GROUPS:
        keep = min(window, seq)
        blk = min(keep, tile)
        first = nt - max(keep // tile, 1)
        for _kv in range(2):
            win_shapes.append(jax.ShapeDtypeStruct((batch, ATT_W, keep), F32))
            win_specs.append(pl.BlockSpec((1, ATT_W, blk),
                                          lambda b, t, first=first: (b, 0, jnp.maximum(t - first, 0))))
    out_shapes = qkv_shapes + win_shapes + [
        jax.ShapeDtypeStruct((batch, seq, LRU_W), BF16),
        jax.ShapeDtypeStruct((batch, CONV_W - 1, LRU_W), F32),
        jax.ShapeDtypeStruct((batch, 1, LRU_W), F32)]
    out_specs = qkv_specs + win_specs + [
        pl.BlockSpec((1, tile, LRU_W), lambda b, t: (b, t, 0)),
        pl.BlockSpec((1, CONV_W - 1, LRU_W), lambda b, t: (b, 0, 0)),
        pl.BlockSpec((1, 1, LRU_W), lambda b, t: (b, 0, 0))]
    in_specs = [pl.BlockSpec((1, tile, D_MODEL), lambda b, t: (b, t, 0)),
                _resident((1, D_MODEL)), _resident(w_a.shape)] + [_resident(a.shape) for a in lru]
    return pl.pallas_call(
        _p_proj_body,
        grid=(batch, nt),
        in_specs=in_specs,
        out_specs=out_specs,
        out_shape=out_shapes,
        scratch_shapes=[pltpu.VMEM((3 * (len(GROUPS) - 1), ATT_CHUNKS, tile, LANES), F32),
                        pltpu.VMEM((LRU_CHUNKS, SCAN_PITCH * (SCAN_SEGS + 1), LANES), F32),
                        pltpu.VMEM((LRU_CHUNKS, SCAN_PITCH * SCAN_SEGS, LANES), F32),
                        pltpu.VMEM((1, LRU_W), F32)],
        compiler_params=_params("parallel", "arbitrary"),
        name="p_proj",
    )(x1, norm_g, w_a, *lru)


def _p_attn(qkv, bias):
    batch = qkv[0].shape[0]
    seq = qkv[0].shape[1] * qkv[0].shape[2]
    tile = ATTN_TILE
    cur_specs, prev_specs, scratch = [], [], []
    for window, dil in GROUPS:
        rows = tile // dil
        cur_specs.append(pl.BlockSpec((1, dil, rows, ATT_W), lambda b, n: (b, 0, n, 0)))
        per = rows // N_REL
        prev_specs.append(pl.BlockSpec((1, dil, N_REL, ATT_W),
                                       lambda b, n, per=per: (b, 0, jnp.maximum(n * per - 1, 0), 0)))
        scratch.append(pltpu.VMEM((dil, N_REL + rows, ATT_W), BF16))
    in_specs = cur_specs * 3 + prev_specs * 2 + [_resident(bias.shape)]
    args = list(qkv) + list(qkv[3:6]) + list(qkv[6:9]) + [bias]
    return pl.pallas_call(
        _p_attn_body,
        grid=(batch, seq // tile),
        in_specs=in_specs,
        out_specs=pl.BlockSpec((1, tile, ATT_W), lambda b, n: (b, n, 0)),
        out_shape=jax.ShapeDtypeStruct((batch, seq, ATT_W), BF16),
        scratch_shapes=scratch * 2 + [pltpu.VMEM((len(GROUPS) * ATT_CHUNKS, tile, LANES), F32)] * 2,
        compiler_params=_params("parallel", "arbitrary"),
        name="p_attn",
    )(*args)


def _merge(x1, y_att, y_rec, mem, norm_g, w_c, b_gate, w_att_o, w_rec_o, w_mem_o, w_out, fused_mem):
    batch, seq, _ = x1.shape
    tile = min(MERGE_TILE, seq)

    def tok(width):
        return pl.BlockSpec((1, tile, width), lambda b, t: (b, t, 0))

    in_specs = [tok(D_MODEL), tok(ATT_W), tok(LRU_W)]
    if fused_mem:
        in_specs += [pl.BlockSpec((1, N_MEM, 2 * MEM_W), lambda b, t: (b, 0, 0))]
    else:
        in_specs += [tok(MEM_W)]
    weights = [norm_g, w_c, b_gate, w_att_o, w_rec_o, w_mem_o, w_out]
    in_specs += [_resident(w.shape) for w in weights]
    return pl.pallas_call(
        functools.partial(_merge_body, fused_mem=fused_mem),
        grid=(batch, seq // tile),
        in_specs=in_specs,
        out_specs=tok(D_MODEL),
        out_shape=jax.ShapeDtypeStruct((batch, seq, D_MODEL), F32),
        compiler_params=_params("parallel", "parallel"),
        name="merge_p" if fused_mem else "merge_s",
    )(x1, y_att, y_rec, mem, *weights)


def _s_mix(z, caches, state_conv, state_lru, cmk, cmv, bias_c, bias_n, lru):
    batch, steps, _ = z.shape

    def per_batch(shape):
        return pl.BlockSpec((1,) + tuple(shape[1:]), lambda b: (b,) + (0,) * (len(shape) - 1))

    ins = [z] + list(caches) + [state_conv, state_lru, cmk, cmv]
    in_specs = [per_batch(a.shape) for a in ins]
    consts = list(bias_c) + [bias_n] + list(lru)
    in_specs += [_resident(a.shape) for a in consts]
    out_shapes = [jax.ShapeDtypeStruct((batch, steps, ATT_W), BF16),
                  jax.ShapeDtypeStruct((batch, steps, LRU_W), BF16),
                  jax.ShapeDtypeStruct((batch, steps, MEM_W), BF16)]
    out_shapes += [jax.ShapeDtypeStruct(c.shape, F32) for c in caches]
    out_shapes += [jax.ShapeDtypeStruct(state_conv.shape, F32), jax.ShapeDtypeStruct(state_lru.shape, F32)]
    return pl.pallas_call(
        functools.partial(_s_mix_body, steps=steps),
        grid=(batch,),
        in_specs=in_specs,
        out_specs=[per_batch(s.shape) for s in out_shapes],
        out_shape=out_shapes,
        scratch_shapes=[pltpu.VMEM((2 * SUBLANES, LRU_W), F32)],
        compiler_params=_params("parallel"),
        name="s_mix",
    )(*ins, *consts)


def _bucket(n):
    n = np.maximum(n, 0)
    max_exact = NUM_BUCKETS // 2
    nf = np.maximum(n, 1).astype(np.float32)
    large = max_exact + (np.log(nf / max_exact) / np.float32(math.log(MAX_DISTANCE / max_exact))
                         * (NUM_BUCKETS - max_exact)).astype(np.int32)
    return np.where(n < max_exact, n, np.minimum(large, NUM_BUCKETS - 1))


def _dist_bias(rel_bias, g, dist, valid):
    tab = rel_bias[:, g * HEADS:(g + 1) * HEADS].T[:, _bucket(dist)]
    return jnp.where(valid[None], tab, -jnp.inf).astype(F32)


def _toeplitz(u, rows, cols):
    lu = rows + cols - 1
    ue = jnp.concatenate([u, u[:, :1]], axis=1)
    m = jnp.tile(ue, (1, rows))[:, :rows * lu].reshape(-1, rows, lu)
    return m[:, :, rows - 1:rows - 1 + cols].reshape(-1, cols)


def _prompt_bias(rel_bias):
    delta = np.arange(-(N_REL - 1), 2 * N_REL)
    valid = (delta >= 0) & (delta <= N_REL)
    tabs = []
    for g, (_, dil) in enumerate(GROUPS):
        v = _dist_bias(rel_bias, g, dil * delta, valid)
        tabs.append(_toeplitz(v[:, ::-1], N_REL, 2 * N_REL))
    return jnp.stack(tabs)


def _sample_bias(rel_bias, steps):
    t = np.arange(steps)[:, None]
    cache, new = [], []
    for g, (window, dil) in enumerate(GROUPS):
        dist = np.arange(window + steps)
        tab = _dist_bias(rel_bias, g, dist, (dist % dil == 0) & (dist <= window))
        cache.append(_toeplitz(tab[:, 1:][:, ::-1], steps, window))
        j = np.arange(N_REL)[None, :]
        dist = t - j
        valid = (j < steps) & (dist >= 0) & (dist % dil == 0)
        new.append(_dist_bias(rel_bias, g, dist, valid).reshape(HEADS * steps, N_REL))
    return cache, jnp.stack(new)


def _block_diag(w):
    per = MXU_TILE // LRU_BLOCK
    w4 = w.reshape(-1, per, LRU_BLOCK, LRU_BLOCK)
    bd = jnp.einsum('jnef,nm->jnemf', w4, jnp.eye(per, dtype=w.dtype))
    return bd.reshape(-1, MXU_TILE, MXU_TILE).astype(BF16)


def _to_pos_minor(c):
    return jnp.transpose(c[0], (0, 2, 3, 1)).reshape(c.shape[1], ATT_W, c.shape[2])


def _from_pos_minor(c):
    return jnp.transpose(c.reshape(c.shape[0], HEADS, HEAD_DIM, c.shape[2]), (0, 3, 1, 2))[None]


def kernel(x_prompt, x_sample, mem_prompt, cache_win_k0, cache_win_v0, cache_win_k1, cache_win_v1, cache_win_k2, cache_win_v2, state_conv, state_lru, cache_mem_k, cache_mem_v, rel_bias, ffn1_norm, ffn1_w_gu, ffn1_w_down, mix_norm, w_in, b_gate, conv_w, conv_b, lru_w_a, lru_b_a, lru_w_i, lru_b_i, lru_lambda, mem_norm, w_mem_kv, w_att_o, w_rec_o, w_mem_o, w_out, ffn2_norm, ffn2_w_gu, ffn2_w_down, final_norm):
    depth = ffn1_norm.shape[0]
    assert depth == 1
    l = 0
    batch, seq, _ = x_prompt.shape
    sbatch, steps, _ = x_sample.shape
    row = lambda v: v.reshape(1, -1)

    lru = (conv_w[l], row(conv_b[l]), _block_diag(lru_w_a[l]), _block_diag(lru_w_i[l]),
           row(lru_b_a[l]), row(lru_b_i[l]), row(lru_lambda[l]))
    ffn1 = (row(ffn1_norm[l]), ffn1_w_gu[l].astype(BF16), ffn1_w_down[l].astype(BF16))
    ffn2 = (row(ffn2_norm[l]), ffn2_w_gu[l].astype(BF16), ffn2_w_down[l].astype(BF16))
    outw = (row(b_gate[l]), w_att_o[l].astype(BF16), w_rec_o[l].astype(BF16), w_mem_o[l].astype(BF16),
            w_out[l].astype(BF16))
    mixg = row(mix_norm[l])
    fin = row(final_norm)

    xp1 = _ffn(x_prompt.reshape(batch * seq, D_MODEL), *ffn1).reshape(batch, seq, D_MODEL)
    p_mk, p_mv, mkv = _mem_kv(mem_prompt.reshape(batch * N_MEM, D_MODEL), row(mem_norm[l]),
                              w_mem_kv[l].astype(BF16), row_tile=2 * N_MEM)
    proj = _p_proj(xp1, mixg, w_in[l][:, :COL_QM].astype(BF16), lru)
    qkv, wins = proj[0:9], proj[9:15]
    y_rec, p_conv, p_lru = proj[15], proj[16], proj[17]
    y_att = _p_attn(qkv, _prompt_bias(rel_bias))
    xp2 = _merge(xp1, y_att, y_rec, mkv.reshape(batch, N_MEM, 2 * MEM_W), mixg,
                 w_in[l][:, COL_QM:].astype(BF16), *outw, fused_mem=True)
    y_prompt = _ffn(xp2.reshape(batch * seq, D_MODEL), *ffn2, final_g=fin).reshape(batch, seq, D_MODEL)

    n_s = sbatch * steps
    xs1 = _ffn(x_sample.reshape(n_s, D_MODEL), *ffn1)
    z = _norm_mm(xs1, mixg, w_in[l].astype(BF16), row_tile=n_s, col_tile=IN_COLS // 2)
    caches = [_to_pos_minor(c) for c in
              (cache_win_k0, cache_win_v0, cache_win_k1, cache_win_v1, cache_win_k2, cache_win_v2)]
    bias_c, bias_n = _sample_bias(rel_bias, steps)
    mem_rows = lambda c: c[l].reshape(sbatch, N_MEM * MEM_HEADS, MEM_HEAD_DIM)
    mix = _s_mix(z.reshape(sbatch, steps, IN_COLS), caches, state_conv[l], state_lru[l].reshape(sbatch, 1, LRU_W),
                 mem_rows(cache_mem_k), mem_rows(cache_mem_v), bias_c, bias_n, lru)
    s_att, s_rec, s_mem = (a.reshape(1, n_s, a.shape[-1]) for a in mix[0:3])
    xs2 = _merge(xs1.reshape(1, n_s, D_MODEL), s_att, s_rec, s_mem, mixg, w_in[l][:, COL_GATE:].astype(BF16),
                 *outw, fused_mem=False)
    y_sample = _ffn(xs2.reshape(n_s, D_MODEL), *ffn2, final_g=fin).reshape(sbatch, steps, D_MODEL)

    mem_state = lambda a: a.reshape(1, batch, N_MEM, MEM_HEADS, MEM_HEAD_DIM)
    return (y_prompt, y_sample, *[_from_pos_minor(w) for w in wins],
            p_conv[None], p_lru.reshape(1, batch, LRU_W), mem_state(p_mk), mem_state(p_mv),
            *[_from_pos_minor(w) for w in mix[3:9]], mix[9][None], mix[10].reshape(1, sbatch, LRU_W))
```

```python
import functools
import math

import numpy as np
import jax
import jax.numpy as jnp
from jax import lax
from jax.experimental import pallas as pl
from jax.experimental.pallas import tpu as pltpu

F32 = jnp.float32
BF16 = jnp.bfloat16

D_MODEL = 1024
D_FF = 2816
HEAD_DIM = 64
HEADS = 4
GROUPS = ((128, 1), (512, 4), (2048, 16))
N_REL = 128
ATT_W = HEADS * HEAD_DIM
ATT_SCALE = HEAD_DIM ** -0.5
NUM_BUCKETS = 32
MAX_DISTANCE = 2048
LRU_W = 768
LRU_BLOCK = 64
LRU_C = 8.0
CONV_W = 4
N_MEM = 256
MEM_HEADS = 4
MEM_HEAD_DIM = 128
MEM_W = MEM_HEADS * MEM_HEAD_DIM
MEM_SCALE = MEM_HEAD_DIM ** -0.5
RMS_EPS = 1e-6
QKV_W = 3 * len(GROUPS) * ATT_W
COL_XR = QKV_W
COL_YR = COL_XR + LRU_W
COL_QM = COL_YR + LRU_W
COL_GATE = COL_QM + MEM_W
IN_COLS = COL_GATE + 3 * D_MODEL

V7X_VMEM_LIMIT = 56 * 1024 * 1024
MXU_TILE = 256
SUBLANES = 8
LANES = 128
FFN_TILE = 512
FFN_CHUNK = 256
PROJ_TILE = 512
SCAN_SEGS = SUBLANES
SCAN_STEPS = PROJ_TILE // SCAN_SEGS
SCAN_PITCH = SCAN_STEPS + SUBLANES
ATTN_TILE = 2048
MERGE_TILE = 512
COMBINE_ROWS = 256
LRU_CHUNKS = LRU_W // LANES
ATT_CHUNKS = ATT_W // LANES


def _dot(a, b):
    return jnp.dot(a, b, preferred_element_type=F32)


def _dot_nt(a, b):
    return lax.dot_general(a, b, (((1,), (1,)), ((), ())), preferred_element_type=F32)


def _rms(x, g):
    return x * lax.rsqrt(jnp.mean(x * x, axis=-1, keepdims=True) + RMS_EPS) * g


def _stack_heads(q):
    lane = lax.broadcasted_iota(jnp.int32, q.shape, 1)
    return jnp.concatenate(
        [jnp.where((lane >= h * HEAD_DIM) & (lane < (h + 1) * HEAD_DIM), q, 0.0) for h in range(HEADS)], axis=0)


def _unstack_heads(x, rows):
    low = lax.broadcasted_iota(jnp.int32, (rows, LANES), 1) < HEAD_DIM
    per = LANES // HEAD_DIM
    return [jnp.where(low,
                      x[(per * c) * rows:(per * c + 1) * rows, c * LANES:(c + 1) * LANES],
                      x[(per * c + 1) * rows:(per * c + 2) * rows, c * LANES:(c + 1) * LANES])
            for c in range(ATT_CHUNKS)]


def _linear_scan(a, u, rows):
    row = lax.broadcasted_iota(jnp.int32, (rows, 1), 0)
    d = 1
    while d < rows:
        a_s = pltpu.roll(a, d, 0)
        u_s = pltpu.roll(u, d, 0)
        keep = row >= d
        u = jnp.where(keep, a * u_s + u, u)
        a = jnp.where(keep, a * a_s, a)
        d *= 2
    return a, u


def _lru_gates(xc, wa_ref, wi_ref, ba_ref, bi_ref, lam_ref):
    xcb = xc.astype(BF16)
    ra, ia = [], []
    for j in range(LRU_W // MXU_TILE):
        blk = xcb[:, j * MXU_TILE:(j + 1) * MXU_TILE]
        ra.append(_dot(blk, wa_ref[j]))
        ia.append(_dot(blk, wi_ref[j]))
    r = jax.nn.sigmoid(jnp.concatenate(ra, axis=-1) + ba_ref[...])
    i = jax.nn.sigmoid(jnp.concatenate(ia, axis=-1) + bi_ref[...])
    nl = -lam_ref[...]
    softplus = jnp.maximum(nl, 0.0) + jnp.log1p(jnp.exp(-jnp.abs(nl)))
    log_a = (-LRU_C * r) * softplus
    a = jnp.exp(log_a)
    u = jnp.sqrt(-jnp.tanh(log_a) * (a * a + 1.0)) * (i * xc)
    return a, u


def _conv_taps(cw_ref, cb_ref, taps):
    xc = cb_ref[...] + cw_ref[CONV_W - 1:CONV_W, :] * taps[0]
    for j in range(1, CONV_W):
        xc = xc + cw_ref[CONV_W - 1 - j:CONV_W - j, :] * taps[j]
    return xc


def _mem_attention(qm, mk_heads, mv_heads):
    heads = []
    for h in range(MEM_HEADS):
        q = qm[:, h * MEM_HEAD_DIM:(h + 1) * MEM_HEAD_DIM].astype(BF16)
        s = _dot_nt(q, mk_heads[h]) * MEM_SCALE
        m = jnp.max(s, axis=-1, keepdims=True)
        p = jnp.exp(s - m)
        l = jnp.sum(p, axis=-1, keepdims=True)
        heads.append(_dot(p.astype(BF16), mv_heads[h]) * (1.0 / l))
    return jnp.concatenate(heads, axis=-1)


def _ffn_body(x_ref, g_ref, wgu_ref, wd_ref, *rest, final_norm):
    if final_norm:
        fg_ref, o_ref, act_ref = rest
    else:
        o_ref, act_ref = rest
    x = x_ref[...]
    hn = _rms(x, g_ref[...]).astype(BF16)
    for c in range(D_FF // FFN_CHUNK):
        lo = c * FFN_CHUNK
        gate = _dot(hn, wgu_ref[:, lo:lo + FFN_CHUNK])
        up = _dot(hn, wgu_ref[:, D_FF + lo:D_FF + lo + FFN_CHUNK])
        act_ref[:, lo:lo + FFN_CHUNK] = (gate * jax.nn.sigmoid(gate) * up).astype(BF16)
    y = x + 0.5 * _dot(act_ref[...], wd_ref[...])
    if final_norm:
        y = _rms(y, fg_ref[...])
    o_ref[...] = y


def _norm_mm_body(x_ref, g_ref, w_ref, o_ref):
    hn = _rms(x_ref[...], g_ref[...]).astype(BF16)
    o_ref[...] = _dot(hn, w_ref[...])


def _mem_kv_body(x_ref, g_ref, w_ref, mk_ref, mv_ref, mkv_ref):
    rows = x_ref.shape[0]
    z = _dot(_rms(x_ref[...], g_ref[...]).astype(BF16), w_ref[...])
    mkv_ref[...] = z.astype(BF16)
    for h in range(MEM_HEADS):
        dst = pl.ds(h, rows, stride=MEM_HEADS)
        mk_ref[dst, :] = z[:, h * MEM_HEAD_DIM:(h + 1) * MEM_HEAD_DIM]
        mv_ref[dst, :] = z[:, MEM_W + h * MEM_HEAD_DIM:MEM_W + (h + 1) * MEM_HEAD_DIM]


def _p_proj_body(x_ref, g_ref, w_ref, cw_ref, cb_ref, wa_ref, wi_ref, ba_ref, bi_ref, lam_ref,
                 q0, q1, q2, k0, k1, k2, v0, v1, v2,
                 pk0, pv0, pk1, pv1, pk2, pv2, yrec_ref, pconv_ref, plru_ref,
                 zs_ref, xpad_ref, hpad_ref, hc_ref):
    rows = PROJ_TILE
    t = pl.program_id(1)
    tail = slice(SCAN_STEPS - SUBLANES, SCAN_STEPS)

    @pl.when(t == 0)
    def _():
        xpad_ref[:, tail, :] = jnp.zeros((LRU_CHUNKS, SUBLANES, LANES), F32)
        hc_ref[...] = jnp.zeros((1, LRU_W), F32)

    hn = _rms(x_ref[0], g_ref[...]).astype(BF16)
    xr = _dot(hn, w_ref[:, COL_XR:COL_XR + LRU_W])
    for c in range(LRU_CHUNKS):
        for s in range(SCAN_SEGS):
            lo = SCAN_PITCH * (s + 1)
            xpad_ref[c, lo:lo + SCAN_STEPS, :] = xr[s * SCAN_STEPS:(s + 1) * SCAN_STEPS, c * LANES:(c + 1) * LANES]

    outs = ((q0, q1, q2), (k0, k1, k2), (v0, v1, v2))
    wins = (None, (pk0, pk1, pk2), (pv0, pv1, pv2))

    def project(kind):
        for g, (window, dil) in enumerate(GROUPS):
            col = (kind * len(GROUPS) + g) * ATT_W
            z = _dot(hn, w_ref[:, col:col + ATT_W])
            if kind == 0:
                z = z * ATT_SCALE
            else:
                keep = min(window, rows)
                wins[kind][g][0] = jnp.transpose(z[rows - keep:, :])
            if dil == 1:
                outs[kind][g][0, 0] = z.astype(BF16)
            else:
                stage = zs_ref.at[kind * (len(GROUPS) - 1) + g - 1]
                for c in range(ATT_CHUNKS):
                    stage[c] = z[:, c * LANES:(c + 1) * LANES]
                for r in range(dil):
                    sel = pl.ds(r, rows // dil, stride=dil)
                    outs[kind][g][0, r] = jnp.concatenate(
                        [stage[c, sel, :] for c in range(ATT_CHUNKS)], axis=-1).astype(BF16)

    project(0)

    def step_rows(i):
        start = SCAN_PITCH + i if i >= 0 else SCAN_STEPS + i
        sel = pl.ds(start, SCAN_SEGS, stride=SCAN_PITCH)
        return jnp.concatenate([xpad_ref[c, sel, :] for c in range(LRU_CHUNKS)], axis=-1)

    hist = CONV_W - 1
    xp = jnp.concatenate([step_rows(i) for i in range(-hist, SCAN_STEPS)], axis=0)
    n = SCAN_STEPS * SCAN_SEGS
    taps = [xp[(hist - j) * SCAN_SEGS:(hist - j) * SCAN_SEGS + n, :] for j in range(CONV_W)]
    a, u = _lru_gates(_conv_taps(cw_ref, cb_ref, taps), wa_ref, wi_ref, ba_ref, bi_ref, lam_ref)

    project(1)
    project(2)
    yr = _dot(hn, w_ref[:, COL_YR:COL_YR + LRU_W])

    def rows_of(v, i):
        return v[i * SCAN_SEGS:(i + 1) * SCAN_SEGS, :]

    h = jnp.zeros((SCAN_SEGS, LRU_W), F32)
    prod = jnp.ones((SCAN_SEGS, LRU_W), F32)
    for i in range(SCAN_STEPS):
        a_i = rows_of(a, i)
        h = a_i * h + rows_of(u, i)
        prod = a_i * prod
    prod_c, h_c = _linear_scan(prod, h, SCAN_SEGS)
    h_end = prod_c * hc_ref[...] + h_c
    seg = lax.broadcasted_iota(jnp.int32, (SCAN_SEGS, 1), 0)
    h = jnp.where(seg == 0, hc_ref[...], pltpu.roll(h_end, 1, 0))
    for i in range(SCAN_STEPS):
        h = rows_of(a, i) * h + rows_of(u, i)
        dst = pl.ds(i, SCAN_SEGS, stride=SCAN_PITCH)
        for c in range(LRU_CHUNKS):
            hpad_ref[c, dst, :] = h[:, c * LANES:(c + 1) * LANES]
    hs = jnp.concatenate(
        [jnp.concatenate([hpad_ref[c, SCAN_PITCH * s:SCAN_PITCH * s + SCAN_STEPS, :] for c in range(LRU_CHUNKS)],
                         axis=-1) for s in range(SCAN_SEGS)], axis=0)
    yrec_ref[0] = (jax.nn.gelu(yr) * hs).astype(BF16)
    hc_ref[...] = h_end[SCAN_SEGS - 1:SCAN_SEGS, :]
    plru_ref[0] = h_end[SCAN_SEGS - 1:SCAN_SEGS, :]
    pconv_ref[0] = xr[rows - hist:rows, :]
    for c in range(LRU_CHUNKS):
        xpad_ref[c, tail, :] = xr[rows - SUBLANES:rows, c * LANES:(c + 1) * LANES]


def _p_attn_body(q0, q1, q2, k0, k1, k2, v0, v1, v2, kp0, kp1, kp2, vp0, vp1, vp2, bias_ref,
                 y_ref, kc0, kc1, kc2, vc0, vc1, vc2, o_scr, l_scr):
    n = pl.program_id(1)
    qs_, ks_, vs_ = (q0, q1, q2), (k0, k1, k2), (v0, v1, v2)
    kps, vps = (kp0, kp1, kp2), (vp0, vp1, vp2)
    kcs, vcs = (kc0, kc1, kc2), (vc0, vc1, vc2)
    for g, (window, dil) in enumerate(GROUPS):
        nblk = ATTN_TILE // dil // N_REL
        q_ref, kc, vc = qs_[g], kcs[g], vcs[g]
        kc[:, 0:N_REL, :] = kps[g][0]
        kc[:, N_REL:, :] = ks_[g][0]
        vc[:, 0:N_REL, :] = vps[g][0]
        vc[:, N_REL:, :] = vs_[g][0]

        def block(i, carry, g=g, dil=dil, nblk=nblk, q_ref=q_ref, kc=kc, vc=vc):
            r = i // nblk
            j = i % nblk
            row0 = pl.multiple_of(j * N_REL, N_REL)
            q = q_ref[0, r, pl.ds(row0, N_REL), :]
            kk = kc[r, pl.ds(row0, 2 * N_REL), :]
            vv = vc[r, pl.ds(row0, 2 * N_REL), :]
            no_prev = jnp.logical_and(n == 0, j == 0).astype(jnp.int32)
            s = _dot_nt(_stack_heads(q), kk) + bias_ref[g + len(GROUPS) * no_prev]
            m = jnp.max(s, axis=-1, keepdims=True)
            p = jnp.exp(s - m)
            l = jnp.sum(p, axis=-1, keepdims=True)
            o = _dot(p.astype(BF16), vv) * (1.0 / l)
            lse = jnp.broadcast_to(m + jnp.log(l), o.shape)
            start = j * (N_REL * dil) + r
            if dil == 1:
                dst = pl.ds(pl.multiple_of(start, N_REL), N_REL)
            else:
                dst = pl.ds(start, N_REL, stride=dil)
            o_c = _unstack_heads(o, N_REL)
            l_c = _unstack_heads(lse, N_REL)
            for c in range(ATT_CHUNKS):
                o_scr[g * ATT_CHUNKS + c, dst, :] = o_c[c]
                l_scr[g * ATT_CHUNKS + c, dst, :] = l_c[c]
            return carry

        lax.fori_loop(0, ATTN_TILE // N_REL, block, 0, unroll=8)

    def combine(c, carry):
        sl = pl.ds(pl.multiple_of(c * COMBINE_ROWS, COMBINE_ROWS), COMBINE_ROWS)
        for k in range(ATT_CHUNKS):
            l0, l1, l2 = (l_scr[g * ATT_CHUNKS + k, sl, :] for g in range(len(GROUPS)))
            o0, o1, o2 = (o_scr[g * ATT_CHUNKS + k, sl, :] for g in range(len(GROUPS)))
            m = jnp.maximum(jnp.maximum(l0, l1), l2)
            w0, w1, w2 = jnp.exp(l0 - m), jnp.exp(l1 - m), jnp.exp(l2 - m)
            mix = (w0 * o0 + w1 * o1 + w2 * o2) / (w0 + w1 + w2)
            y_ref[0, sl, k * LANES:(k + 1) * LANES] = mix.astype(BF16)
        return carry

    lax.fori_loop(0, ATTN_TILE // COMBINE_ROWS, combine, 0)


def _merge_body(x_ref, ya_ref, yr_ref, *rest, fused_mem):
    if fused_mem:
        mkv_ref, g_ref, wc_ref, bg_ref, wao_ref, wro_ref, wmo_ref, wout_ref, o_ref = rest
    else:
        ym_ref, g_ref, wc_ref, bg_ref, wao_ref, wro_ref, wmo_ref, wout_ref, o_ref = rest
    x = x_ref[0]
    hn = _rms(x, g_ref[...]).astype(BF16)
    if fused_mem:
        qm = _dot(hn, wc_ref[:, 0:MEM_W])
        mk = [mkv_ref[0, :, h * MEM_HEAD_DIM:(h + 1) * MEM_HEAD_DIM] for h in range(MEM_HEADS)]
        mv = [mkv_ref[0, :, MEM_W + h * MEM_HEAD_DIM:MEM_W + (h + 1) * MEM_HEAD_DIM] for h in range(MEM_HEADS)]
        ym = _mem_attention(qm, mk, mv).astype(BF16)
        off = MEM_W
    else:
        ym = ym_ref[0]
        off = 0
    merged = None
    for idx, (y, w_ref) in enumerate(((ya_ref[0], wao_ref), (yr_ref[0], wro_ref), (ym, wmo_ref))):
        lo = idx * D_MODEL
        gate = jax.nn.sigmoid(_dot(hn, wc_ref[:, off + lo:off + lo + D_MODEL]) + bg_ref[:, lo:lo + D_MODEL])
        term = gate * _dot(y, w_ref[...])
        merged = term if merged is None else merged + term
    o_ref[0] = x + _dot(merged.astype(BF16), wout_ref[...])


def _s_mix_body(z_ref, ck0, cv0, ck1, cv1, ck2, cv2, sconv_ref, slru_ref, cmk_ref, cmv_ref,
                bc0, bc1, bc2, bn_ref, cw_ref, cb_ref, wa_ref, wi_ref, ba_ref, bi_ref, lam_ref,
                ya_ref, yrec_ref, ym_ref, ok0, ov0, ok1, ov1, ok2, ov2, oconv_ref, olru_ref,
                ext_ref, *, steps):
    cks, cvs = (ck0, ck1, ck2), (cv0, cv1, cv2)
    oks, ovs = (ok0, ok1, ok2), (ov0, ov1, ov2)
    bcs = (bc0, bc1, bc2)
    zeros = jnp.zeros((N_REL - steps, ATT_W), F32)
    new_lanes = lax.broadcasted_iota(jnp.int32, (ATT_W, LANES), 1) >= LANES - steps
    outs, lses = [], []
    for g, (window, dil) in enumerate(GROUPS):
        q = z_ref[0, :, g * ATT_W:(g + 1) * ATT_W] * ATT_SCALE
        kn = z_ref[0, :, (3 + g) * ATT_W:(4 + g) * ATT_W]
        vn = z_ref[0, :, (6 + g) * ATT_W:(7 + g) * ATT_W]
        qs = _stack_heads(q).astype(BF16)
        kt = cks[g][0]
        vt = cvs[g][0]
        s_c = _dot(qs, kt.astype(BF16)) + bcs[g][...]
        s_n = _dot_nt(qs, jnp.concatenate([kn, zeros], axis=0).astype(BF16)) + bn_ref[g]
        m = jnp.maximum(jnp.max(s_c, axis=-1, keepdims=True), jnp.max(s_n, axis=-1, keepdims=True))
        p_c = jnp.exp(s_c - m)
        p_n = jnp.exp(s_n - m)
        l = jnp.sum(p_c, axis=-1, keepdims=True) + jnp.sum(p_n, axis=-1, keepdims=True)
        o = (_dot_nt(p_c.astype(BF16), vt.astype(BF16))
             + _dot(p_n.astype(BF16), jnp.concatenate([vn, zeros], axis=0).astype(BF16))) * (1.0 / l)
        lse = jnp.broadcast_to(m + jnp.log(l), o.shape)
        outs.append(jnp.concatenate(_unstack_heads(o, steps), axis=-1))
        lses.append(jnp.concatenate(_unstack_heads(lse, steps), axis=-1))
        for old, new, dst in ((kt, kn, oks[g]), (vt, vn, ovs[g])):
            rolled = pltpu.roll(old, window - steps, 1)
            new_t = jnp.transpose(jnp.concatenate([zeros, new], axis=0))
            if window > LANES:
                dst[0, :, 0:window - LANES] = rolled[:, 0:window - LANES]
            dst[0, :, window - LANES:window] = jnp.where(new_lanes, new_t, rolled[:, window - LANES:window])
    m = jnp.maximum(jnp.maximum(lses[0], lses[1]), lses[2])
    ws = [jnp.exp(l - m) for l in lses]
    ya_ref[0] = ((ws[0] * outs[0] + ws[1] * outs[1] + ws[2] * outs[2]) / (ws[0] + ws[1] + ws[2])).astype(BF16)

    hist = CONV_W - 1
    ext_ref[0:SUBLANES, :] = jnp.zeros((SUBLANES, LRU_W), F32)
    ext_ref[SUBLANES - hist:SUBLANES, :] = sconv_ref[0]
    ext_ref[SUBLANES:SUBLANES + steps, :] = z_ref[0, :, COL_XR:COL_XR + LRU_W]
    taps = [ext_ref[SUBLANES - j:SUBLANES - j + steps, :] for j in range(CONV_W)]
    a, u = _lru_gates(_conv_taps(cw_ref, cb_ref, taps), wa_ref, wi_ref, ba_ref, bi_ref, lam_ref)
    a_c, u_c = _linear_scan(a, u, steps)
    h = a_c * slru_ref[0] + u_c
    yrec_ref[0] = (jax.nn.gelu(z_ref[0, :, COL_YR:COL_YR + LRU_W]) * h).astype(BF16)
    olru_ref[0] = h[steps - 1:steps, :]
    oconv_ref[0] = ext_ref[SUBLANES + steps - hist:SUBLANES + steps, :]

    qm = z_ref[0, :, COL_QM:COL_QM + MEM_W]
    mk = [cmk_ref[0, pl.ds(h, N_MEM, stride=MEM_HEADS), :].astype(BF16) for h in range(MEM_HEADS)]
    mv = [cmv_ref[0, pl.ds(h, N_MEM, stride=MEM_HEADS), :].astype(BF16) for h in range(MEM_HEADS)]
    ym_ref[0] = _mem_attention(qm, mk, mv).astype(BF16)


def _resident(shape):
    zeros = (0,) * len(shape)
    return pl.BlockSpec(shape, lambda *_: zeros, pipeline_mode=pl.Buffered(1))


def _params(*semantics):
    return pltpu.CompilerParams(dimension_semantics=semantics, vmem_limit_bytes=V7X_VMEM_LIMIT)


def _ffn(x, norm_g, w_gu, w_down, final_g=None):
    n = x.shape[0]
    tile = min(FFN_TILE, n)
    row = pl.BlockSpec((tile, D_MODEL), lambda i: (i, 0))
    in_specs = [row, _resident((1, D_MODEL)), _resident(w_gu.shape), _resident(w_down.shape)]
    args = [x, norm_g, w_gu, w_down]
    if final_g is not None:
        in_specs.append(_resident((1, D_MODEL)))
        args.append(final_g)
    return pl.pallas_call(
        functools.partial(_ffn_body, final_norm=final_g is not None),
        grid=(n // tile,),
        in_specs=in_specs,
        out_specs=row,
        out_shape=jax.ShapeDtypeStruct((n, D_MODEL), F32),
        scratch_shapes=[pltpu.VMEM((tile, D_FF), BF16)],
        compiler_params=_params("parallel"),
        name="ffn_final" if final_g is not None else "ffn",
    )(*args)


def _norm_mm(x, norm_g, w, row_tile, col_tile):
    n, cols = x.shape[0], w.shape[1]
    return pl.pallas_call(
        _norm_mm_body,
        grid=(n // row_tile, cols // col_tile),
        in_specs=[pl.BlockSpec((row_tile, D_MODEL), lambda i, j: (i, 0)),
                  _resident((1, D_MODEL)),
                  pl.BlockSpec((D_MODEL, col_tile), lambda i, j: (0, j))],
        out_specs=pl.BlockSpec((row_tile, col_tile), lambda i, j: (i, j)),
        out_shape=jax.ShapeDtypeStruct((n, cols), F32),
        compiler_params=_params("parallel", "parallel"),
        name="norm_mm",
    )(x, norm_g, w)


def _mem_kv(mem, norm_g, w_kv, row_tile):
    n = mem.shape[0]
    state = jax.ShapeDtypeStruct((n * MEM_HEADS, MEM_HEAD_DIM), F32)
    state_spec = pl.BlockSpec((row_tile * MEM_HEADS, MEM_HEAD_DIM), lambda i: (i, 0))
    return pl.pallas_call(
        _mem_kv_body,
        grid=(n // row_tile,),
        in_specs=[pl.BlockSpec((row_tile, D_MODEL), lambda i: (i, 0)),
                  _resident((1, D_MODEL)), _resident(w_kv.shape)],
        out_specs=[state_spec, state_spec, pl.BlockSpec((row_tile, 2 * MEM_W), lambda i: (i, 0))],
        out_shape=[state, state, jax.ShapeDtypeStruct((n, 2 * MEM_W), BF16)],
        compiler_params=_params("parallel"),
        name="mem_kv",
    )(mem, norm_g, w_kv)


def _p_proj(x1, norm_g, w_a, lru):
    batch, seq, _ = x1.shape
    tile = PROJ_TILE
    nt = seq // tile
    qkv_shapes, qkv_specs = [], []
    for _kind in range(3):
        for window, dil in GROUPS:
            qkv_shapes.append(jax.ShapeDtypeStruct((batch, dil, seq // dil, ATT_W), BF16))
            qkv_specs.append(pl.BlockSpec((1, dil, tile // dil, ATT_W), lambda b, t: (b, 0, t, 0)))
    win_shapes, win_specs = [], []
    for window, dil in GROUPS:
        keep = min(window, seq)
        blk = min(keep, tile)
        first = nt - max(keep // tile, 1)
        for _kv in range(2):
            win_shapes.append(jax.ShapeDtypeStruct((batch, ATT_W, keep), F32))
            win_specs.append(pl.BlockSpec((1, ATT_W, blk),
                                          lambda b, t, first=first: (b, 0, jnp.maximum(t - first, 0))))
    out_shapes = qkv_shapes + win_shapes + [
        jax.ShapeDtypeStruct((batch, seq, LRU_W), BF16),
        jax.ShapeDtypeStruct((batch, CONV_W - 1, LRU_W), F32),
        jax.ShapeDtypeStruct((batch, 1, LRU_W), F32)]
    out_specs = qkv_specs + win_specs + [
        pl.BlockSpec((1, tile, LRU_W), lambda b, t: (b, t, 0)),
        pl.BlockSpec((1, CONV_W - 1, LRU_W), lambda b, t: (b, 0, 0)),
        pl.BlockSpec((1, 1, LRU_W), lambda b, t: (b, 0, 0))]
    in_specs = [pl.BlockSpec((1, tile, D_MODEL), lambda b, t: (b, t, 0)),
                _resident((1, D_MODEL)), _resident(w_a.shape)] + [_resident(a.shape) for a in lru]
    return pl.pallas_call(
        _p_proj_body,
        grid=(batch, nt),
        in_specs=in_specs,
        out_specs=out_specs,
        out_shape=out_shapes,
        scratch_shapes=[pltpu.VMEM((3 * (len(GROUPS) - 1), ATT_CHUNKS, tile, LANES), F32),
                        pltpu.VMEM((LRU_CHUNKS, SCAN_PITCH * (SCAN_SEGS + 1), LANES), F32),
                        pltpu.VMEM((LRU_CHUNKS, SCAN_PITCH * SCAN_SEGS, LANES), F32),
                        pltpu.VMEM((1, LRU_W), F32)],
        compiler_params=_params("parallel", "arbitrary"),
        name="p_proj",
    )(x1, norm_g, w_a, *lru)


def _p_attn(qkv, bias):
    batch = qkv[0].shape[0]
    seq = qkv[0].shape[1] * qkv[0].shape[2]
    tile = ATTN_TILE
    cur_specs, prev_specs, scratch = [], [], []
    for window, dil in GROUPS:
        rows = tile // dil
        cur_specs.append(pl.BlockSpec((1, dil, rows, ATT_W), lambda b, n: (b, 0, n, 0)))
        per = rows // N_REL
        prev_specs.append(pl.BlockSpec((1, dil, N_REL, ATT_W),
                                       lambda b, n, per=per: (b, 0, jnp.maximum(n * per - 1, 0), 0)))
        scratch.append(pltpu.VMEM((dil, N_REL + rows, ATT_W), BF16))
    in_specs = cur_specs * 3 + prev_specs * 2 + [_resident(bias.shape)]
    args = list(qkv) + list(qkv[3:6]) + list(qkv[6:9]) + [bias]
    return pl.pallas_call(
        _p_attn_body,
        grid=(batch, seq // tile),
        in_specs=in_specs,
        out_specs=pl.BlockSpec((1, tile, ATT_W), lambda b, n: (b, n, 0)),
        out_shape=jax.ShapeDtypeStruct((batch, seq, ATT_W), BF16),
        scratch_shapes=scratch * 2 + [pltpu.VMEM((len(GROUPS) * ATT_CHUNKS, tile, LANES), F32)] * 2,
        compiler_params=_params("parallel", "arbitrary"),
        name="p_attn",
    )(*args)


def _merge(x1, y_att, y_rec, mem, norm_g, w_c, b_gate, w_att_o, w_rec_o, w_mem_o, w_out, fused_mem):
    batch, seq, _ = x1.shape
    tile = min(MERGE_TILE, seq)

    def tok(width):
        return pl.BlockSpec((1, tile, width), lambda b, t: (b, t, 0))

    in_specs = [tok(D_MODEL), tok(ATT_W), tok(LRU_W)]
    if fused_mem:
        in_specs += [pl.BlockSpec((1, N_MEM, 2 * MEM_W), lambda b, t: (b, 0, 0))]
    else:
        in_specs += [tok(MEM_W)]
    weights = [norm_g, w_c, b_gate, w_att_o, w_rec_o, w_mem_o, w_out]
    in_specs += [_resident(w.shape) for w in weights]
    return pl.pallas_call(
        functools.partial(_merge_body, fused_mem=fused_mem),
        grid=(batch, seq // tile),
        in_specs=in_specs,
        out_specs=tok(D_MODEL),
        out_shape=jax.ShapeDtypeStruct((batch, seq, D_MODEL), F32),
        compiler_params=_params("parallel", "parallel"),
        name="merge_p" if fused_mem else "merge_s",
    )(x1, y_att, y_rec, mem, *weights)


def _s_mix(z, caches, state_conv, state_lru, cmk, cmv, bias_c, bias_n, lru):
    batch, steps, _ = z.shape

    def per_batch(shape):
        return pl.BlockSpec((1,) + tuple(shape[1:]), lambda b: (b,) + (0,) * (len(shape) - 1))

    ins = [z] + list(caches) + [state_conv, state_lru, cmk, cmv]
    in_specs = [per_batch(a.shape) for a in ins]
    consts = list(bias_c) + [bias_n] + list(lru)
    in_specs += [_resident(a.shape) for a in consts]
    out_shapes = [jax.ShapeDtypeStruct((batch, steps, ATT_W), BF16),
                  jax.ShapeDtypeStruct((batch, steps, LRU_W), BF16),
                  jax.ShapeDtypeStruct((batch, steps, MEM_W), BF16)]
    out_shapes += [jax.ShapeDtypeStruct(c.shape, F32) for c in caches]
    out_shapes += [jax.ShapeDtypeStruct(state_conv.shape, F32), jax.ShapeDtypeStruct(state_lru.shape, F32)]
    return pl.pallas_call(
        functools.partial(_s_mix_body, steps=steps),
        grid=(batch,),
        in_specs=in_specs,
        out_specs=[per_batch(s.shape) for s in out_shapes],
        out_shape=out_shapes,
        scratch_shapes=[pltpu.VMEM((2 * SUBLANES, LRU_W), F32)],
        compiler_params=_params("parallel"),
        name="s_mix",
    )(*ins, *consts)


def _bucket(n):
    n = np.maximum(n, 0)
    max_exact = NUM_BUCKETS // 2
    nf = np.maximum(n, 1).astype(np.float32)
    large = max_exact + (np.log(nf / max_exact) / np.float32(math.log(MAX_DISTANCE / max_exact))
                         * (NUM_BUCKETS - max_exact)).astype(np.int32)
    return np.where(n < max_exact, n, np.minimum(large, NUM_BUCKETS - 1))


def _dist_bias(rel_bias, g, dist, valid):
    tab = rel_bias[:, g * HEADS:(g + 1) * HEADS].T[:, _bucket(dist)]
    return jnp.where(valid[None], tab, -jnp.inf).astype(F32)


def _toeplitz(u, rows, cols):
    lu = rows + cols - 1
    ue = jnp.concatenate([u, u[:, :1]], axis=1)
    m = jnp.tile(ue, (1, rows))[:, :rows * lu].reshape(-1, rows, lu)
    return m[:, :, rows - 1:rows - 1 + cols].reshape(-1, cols)


def _prompt_bias(rel_bias):
    delta = np.arange(-(N_REL - 1), 2 * N_REL)
    valid = (delta >= 0) & (delta <= N_REL)
    tabs = []
    for g, (_, dil) in enumerate(GROUPS):
        v = _dist_bias(rel_bias, g, dil * delta, valid)
        tabs.append(_toeplitz(v[:, ::-1], N_REL, 2 * N_REL))
    tabs = jnp.stack(tabs)
    has_prev = np.arange(2 * N_REL) >= N_REL
    return jnp.concatenate([tabs, jnp.where(has_prev, tabs, -jnp.inf)])


def _sample_bias(rel_bias, steps):
    t = np.arange(steps)[:, None]
    cache, new = [], []
    for g, (window, dil) in enumerate(GROUPS):
        dist = np.arange(window + steps)
        tab = _dist_bias(rel_bias, g, dist, (dist % dil == 0) & (dist <= window))
        cache.append(_toeplitz(tab[:, 1:][:, ::-1], steps, window))
        j = np.arange(N_REL)[None, :]
        dist = t - j
        valid = (j < steps) & (dist >= 0) & (dist % dil == 0)
        new.append(_dist_bias(rel_bias, g, dist, valid).reshape(HEADS * steps, N_REL))
    return cache, jnp.stack(new)


def _block_diag(w):
    per = MXU_TILE // LRU_BLOCK
    w4 = w.reshape(-1, per, LRU_BLOCK, LRU_BLOCK)
    bd = jnp.einsum('jnef,nm->jnemf', w4, jnp.eye(per, dtype=w.dtype))
    return bd.reshape(-1, MXU_TILE, MXU_TILE).astype(BF16)


def _to_pos_minor(c):
    return jnp.transpose(c[0], (0, 2, 3, 1)).reshape(c.shape[1], ATT_W, c.shape[2])


def _from_pos_minor(c):
    return jnp.transpose(c.reshape(c.shape[0], HEADS, HEAD_DIM, c.shape[2]), (0, 3, 1, 2))[None]


def kernel(x_prompt, x_sample, mem_prompt, cache_win_k0, cache_win_v0, cache_win_k1, cache_win_v1, cache_win_k2, cache_win_v2, state_conv, state_lru, cache_mem_k, cache_mem_v, rel_bias, ffn1_norm, ffn1_w_gu, ffn1_w_down, mix_norm, w_in, b_gate, conv_w, conv_b, lru_w_a, lru_b_a, lru_w_i, lru_b_i, lru_lambda, mem_norm, w_mem_kv, w_att_o, w_rec_o, w_mem_o, w_out, ffn2_norm, ffn2_w_gu, ffn2_w_down, final_norm):
    depth = ffn1_norm.shape[0]
    assert depth == 1
    l = 0
    batch, seq, _ = x_prompt.shape
    sbatch, steps, _ = x_sample.shape
    row = lambda v: v.reshape(1, -1)

    lru = (conv_w[l], row(conv_b[l]), _block_diag(lru_w_a[l]), _block_diag(lru_w_i[l]),
           row(lru_b_a[l]), row(lru_b_i[l]), row(lru_lambda[l]))
    ffn1 = (row(ffn1_norm[l]), ffn1_w_gu[l].astype(BF16), ffn1_w_down[l].astype(BF16))
    ffn2 = (row(ffn2_norm[l]), ffn2_w_gu[l].astype(BF16), ffn2_w_down[l].astype(BF16))
    outw = (row(b_gate[l]), w_att_o[l].astype(BF16), w_rec_o[l].astype(BF16), w_mem_o[l].astype(BF16),
            w_out[l].astype(BF16))
    mixg = row(mix_norm[l])
    fin = row(final_norm)

    xp1 = _ffn(x_prompt.reshape(batch * seq, D_MODEL), *ffn1).reshape(batch, seq, D_MODEL)
    p_mk, p_mv, mkv = _mem_kv(mem_prompt.reshape(batch * N_MEM, D_MODEL), row(mem_norm[l]),
                              w_mem_kv[l].astype(BF16), row_tile=2 * N_MEM)
    proj = _p_proj(xp1, mixg, w_in[l][:, :COL_QM].astype(BF16), lru)
    qkv, wins = proj[0:9], proj[9:15]
    y_rec, p_conv, p_lru = proj[15], proj[16], proj[17]
    y_att = _p_attn(qkv, _prompt_bias(rel_bias))
    xp2 = _merge(xp1, y_att, y_rec, mkv.reshape(batch, N_MEM, 2 * MEM_W), mixg,
                 w_in[l][:, COL_QM:].astype(BF16), *outw, fused_mem=True)
    y_prompt = _ffn(xp2.reshape(batch * seq, D_MODEL), *ffn2, final_g=fin).reshape(batch, seq, D_MODEL)

    n_s = sbatch * steps
    xs1 = _ffn(x_sample.reshape(n_s, D_MODEL), *ffn1)
    z = _norm_mm(xs1, mixg, w_in[l].astype(BF16), row_tile=n_s, col_tile=IN_COLS // 2)
    caches = [_to_pos_minor(c) for c in
              (cache_win_k0, cache_win_v0, cache_win_k1, cache_win_v1, cache_win_k2, cache_win_v2)]
    bias_c, bias_n = _sample_bias(rel_bias, steps)
    mem_rows = lambda c: c[l].reshape(sbatch, N_MEM * MEM_HEADS, MEM_HEAD_DIM)
    mix = _s_mix(z.reshape(sbatch, steps, IN_COLS), caches, state_conv[l], state_lru[l].reshape(sbatch, 1, LRU_W),
                 mem_rows(cache_mem_k), mem_rows(cache_mem_v), bias_c, bias_n, lru)
    s_att, s_rec, s_mem = (a.reshape(1, n_s, a.shape[-1]) for a in mix[0:3])
    xs2 = _merge(xs1.reshape(1, n_s, D_MODEL), s_att, s_rec, s_mem, mixg, w_in[l][:, COL_GATE:].astype(BF16),
                 *outw, fused_mem=False)
    y_sample = _ffn(xs2.reshape(n_s, D_MODEL), *ffn2, final_g=fin).reshape(sbatch, steps, D_MODEL)

    mem_state = lambda a: a.reshape(1, batch, N_MEM, MEM_HEADS, MEM_HEAD_DIM)
    return (y_prompt, y_sample, *[_from_pos_minor(w) for w in wins],
            p_conv[None], p_lru.reshape(1, batch, LRU_W), mem_state(p_mk), mem_state(p_mv),
            *[_from_pos_minor(w) for w in mix[3:9]], mix[9][None], mix[10].reshape(1, sbatch, LRU_W))
```

```python
import functools
import math

import numpy as np
import jax
import jax.numpy as jnp
from jax import lax
from jax.experimental import pallas as pl
from jax.experimental.pallas import tpu as pltpu

F32 = jnp.float32
BF16 = jnp.bfloat16

D_MODEL = 1024
D_FF = 2816
HEAD_DIM = 64
HEADS = 4
GROUPS = ((128, 1), (512, 4), (2048, 16))
N_REL = 128
ATT_W = HEADS * HEAD_DIM
ATT_SCALE = HEAD_DIM ** -0.5
NUM_BUCKETS = 32
MAX_DISTANCE = 2048
LRU_W = 768
LRU_BLOCK = 64
LRU_C = 8.0
CONV_W = 4
N_MEM = 256
MEM_HEADS = 4
MEM_HEAD_DIM = 128
MEM_W = MEM_HEADS * MEM_HEAD_DIM
MEM_SCALE = MEM_HEAD_DIM ** -0.5
RMS_EPS = 1e-6
QKV_W = 3 * len(GROUPS) * ATT_W
COL_XR = QKV_W
COL_YR = COL_XR + LRU_W
COL_QM = COL_YR + LRU_W
COL_GATE = COL_QM + MEM_W
IN_COLS = COL_GATE + 3 * D_MODEL

V7X_VMEM_LIMIT = 62 * 1024 * 1024
MXU_TILE = 256
SUBLANES = 8
LANES = 128
FFN_TILE = 512
FFN_CHUNK = 256
PROJ_TILE = 512
SCAN_SEGS = SUBLANES
SCAN_STEPS = PROJ_TILE // SCAN_SEGS
SCAN_PITCH = SCAN_STEPS + SUBLANES
MIX_TILE = 512
ATTN_DEPTH = 6
LRU_CHUNKS = LRU_W // LANES
ATT_CHUNKS = ATT_W // LANES


def _dot(a, b):
    return jnp.dot(a, b, preferred_element_type=F32)


def _dot_nt(a, b):
    return lax.dot_general(a, b, (((1,), (1,)), ((), ())), preferred_element_type=F32)


def _rms(x, g):
    return x * lax.rsqrt(jnp.mean(x * x, axis=-1, keepdims=True) + RMS_EPS) * g


def _stack_heads(q):
    lane = lax.broadcasted_iota(jnp.int32, q.shape, 1)
    return jnp.concatenate(
        [jnp.where((lane >= h * HEAD_DIM) & (lane < (h + 1) * HEAD_DIM), q, 0.0) for h in range(HEADS)], axis=0)


def _unstack_heads(x, rows):
    low = lax.broadcasted_iota(jnp.int32, (rows, LANES), 1) < HEAD_DIM
    per = LANES // HEAD_DIM
    return [jnp.where(low,
                      x[(per * c) * rows:(per * c + 1) * rows, c * LANES:(c + 1) * LANES],
                      x[(per * c + 1) * rows:(per * c + 2) * rows, c * LANES:(c + 1) * LANES])
            for c in range(ATT_CHUNKS)]


def _linear_scan(a, u, rows):
    row = lax.broadcasted_iota(jnp.int32, (rows, 1), 0)
    d = 1
    while d < rows:
        a_s = pltpu.roll(a, d, 0)
        u_s = pltpu.roll(u, d, 0)
        keep = row >= d
        u = jnp.where(keep, a * u_s + u, u)
        a = jnp.where(keep, a * a_s, a)
        d *= 2
    return a, u


def _lru_gates(xc, wa_ref, wi_ref, ba_ref, bi_ref, lam_ref):
    xcb = xc.astype(BF16)
    ra, ia = [], []
    for j in range(LRU_W // MXU_TILE):
        blk = xcb[:, j * MXU_TILE:(j + 1) * MXU_TILE]
        ra.append(_dot(blk, wa_ref[j]))
        ia.append(_dot(blk, wi_ref[j]))
    r = jax.nn.sigmoid(jnp.concatenate(ra, axis=-1) + ba_ref[...])
    i = jax.nn.sigmoid(jnp.concatenate(ia, axis=-1) + bi_ref[...])
    nl = -lam_ref[...]
    softplus = jnp.maximum(nl, 0.0) + jnp.log1p(jnp.exp(-jnp.abs(nl)))
    log_a = (-LRU_C * r) * softplus
    a = jnp.exp(log_a)
    u = jnp.sqrt(-jnp.tanh(log_a) * (a * a + 1.0)) * (i * xc)
    return a, u


def _conv_taps(cw_ref, cb_ref, taps):
    xc = cb_ref[...] + cw_ref[CONV_W - 1:CONV_W, :] * taps[0]
    for j in range(1, CONV_W):
        xc = xc + cw_ref[CONV_W - 1 - j:CONV_W - j, :] * taps[j]
    return xc


def _mem_attention(qm, mk_heads, mv_heads):
    heads = []
    for h in range(MEM_HEADS):
        q = qm[:, h * MEM_HEAD_DIM:(h + 1) * MEM_HEAD_DIM].astype(BF16)
        s = _dot_nt(q, mk_heads[h]) * MEM_SCALE
        m = jnp.max(s, axis=-1, keepdims=True)
        p = jnp.exp(s - m)
        l = jnp.sum(p, axis=-1, keepdims=True)
        heads.append(_dot(p.astype(BF16), mv_heads[h]) * (1.0 / l))
    return jnp.concatenate(heads, axis=-1)


def _ffn_body(x_ref, g_ref, wgu_ref, wd_ref, *rest, final_norm):
    if final_norm:
        fg_ref, o_ref, act_ref = rest
    else:
        o_ref, act_ref = rest
    x = x_ref[...]
    hn = _rms(x, g_ref[...]).astype(BF16)
    for c in range(D_FF // FFN_CHUNK):
        lo = c * FFN_CHUNK
        gate = _dot(hn, wgu_ref[:, lo:lo + FFN_CHUNK])
        up = _dot(hn, wgu_ref[:, D_FF + lo:D_FF + lo + FFN_CHUNK])
        act_ref[:, lo:lo + FFN_CHUNK] = (gate * jax.nn.sigmoid(gate) * up).astype(BF16)
    y = x + 0.5 * _dot(act_ref[...], wd_ref[...])
    if final_norm:
        y = _rms(y, fg_ref[...])
    o_ref[...] = y


def _norm_mm_body(x_ref, g_ref, w_ref, o_ref):
    hn = _rms(x_ref[...], g_ref[...]).astype(BF16)
    o_ref[...] = _dot(hn, w_ref[...])


def _mem_kv_body(x_ref, g_ref, w_ref, mk_ref, mv_ref, mkv_ref):
    rows = x_ref.shape[0]
    z = _dot(_rms(x_ref[...], g_ref[...]).astype(BF16), w_ref[...])
    mkv_ref[...] = z.astype(BF16)
    for h in range(MEM_HEADS):
        dst = pl.ds(h, rows, stride=MEM_HEADS)
        mk_ref[dst, :] = z[:, h * MEM_HEAD_DIM:(h + 1) * MEM_HEAD_DIM]
        mv_ref[dst, :] = z[:, MEM_W + h * MEM_HEAD_DIM:MEM_W + (h + 1) * MEM_HEAD_DIM]


def _p_proj_body(x_ref, g_ref, w_ref, cw_ref, cb_ref, wa_ref, wi_ref, ba_ref, bi_ref, lam_ref,
                 q0, q1, q2, k0, k1, k2, v0, v1, v2,
                 pk0, pv0, pk1, pv1, pk2, pv2, yrec_ref, pconv_ref, plru_ref,
                 zs_ref, xpad_ref, hpad_ref, hc_ref):
    rows = PROJ_TILE
    t = pl.program_id(1)
    tail = slice(SCAN_STEPS - SUBLANES, SCAN_STEPS)

    @pl.when(t == 0)
    def _():
        xpad_ref[:, tail, :] = jnp.zeros((LRU_CHUNKS, SUBLANES, LANES), F32)
        hc_ref[...] = jnp.zeros((1, LRU_W), F32)

    hn = _rms(x_ref[0], g_ref[...]).astype(BF16)
    xr = _dot(hn, w_ref[:, COL_XR:COL_XR + LRU_W])
    for c in range(LRU_CHUNKS):
        for s in range(SCAN_SEGS):
            lo = SCAN_PITCH * (s + 1)
            xpad_ref[c, lo:lo + SCAN_STEPS, :] = xr[s * SCAN_STEPS:(s + 1) * SCAN_STEPS, c * LANES:(c + 1) * LANES]

    outs = ((q0, q1, q2), (k0, k1, k2), (v0, v1, v2))
    wins = (None, (pk0, pk1, pk2), (pv0, pv1, pv2))

    def project(kind):
        for g, (window, dil) in enumerate(GROUPS):
            col = (kind * len(GROUPS) + g) * ATT_W
            z = _dot(hn, w_ref[:, col:col + ATT_W])
            if kind == 0:
                z = z * ATT_SCALE
            else:
                keep = min(window, rows)
                wins[kind][g][0] = jnp.transpose(z[rows - keep:, :])
            if dil == 1:
                outs[kind][g][0, 0] = z.astype(BF16)
            else:
                stage = zs_ref.at[kind * (len(GROUPS) - 1) + g - 1]
                for c in range(ATT_CHUNKS):
                    stage[c] = z[:, c * LANES:(c + 1) * LANES]
                for r in range(dil):
                    sel = pl.ds(r, rows // dil, stride=dil)
                    outs[kind][g][0, r] = jnp.concatenate(
                        [stage[c, sel, :] for c in range(ATT_CHUNKS)], axis=-1).astype(BF16)

    project(0)

    def step_rows(i):
        start = SCAN_PITCH + i if i >= 0 else SCAN_STEPS + i
        sel = pl.ds(start, SCAN_SEGS, stride=SCAN_PITCH)
        return jnp.concatenate([xpad_ref[c, sel, :] for c in range(LRU_CHUNKS)], axis=-1)

    hist = CONV_W - 1
    xp = jnp.concatenate([step_rows(i) for i in range(-hist, SCAN_STEPS)], axis=0)
    n = SCAN_STEPS * SCAN_SEGS
    taps = [xp[(hist - j) * SCAN_SEGS:(hist - j) * SCAN_SEGS + n, :] for j in range(CONV_W)]
    a, u = _lru_gates(_conv_taps(cw_ref, cb_ref, taps), wa_ref, wi_ref, ba_ref, bi_ref, lam_ref)

    project(1)
    project(2)
    yr = _dot(hn, w_ref[:, COL_YR:COL_YR + LRU_W])

    def rows_of(v, i):
        return v[i * SCAN_SEGS:(i + 1) * SCAN_SEGS, :]

    h = jnp.zeros((SCAN_SEGS, LRU_W), F32)
    prod = jnp.ones((SCAN_SEGS, LRU_W), F32)
    for i in range(SCAN_STEPS):
        a_i = rows_of(a, i)
        h = a_i * h + rows_of(u, i)
        prod = a_i * prod
    prod_c, h_c = _linear_scan(prod, h, SCAN_SEGS)
    h_end = prod_c * hc_ref[...] + h_c
    seg = lax.broadcasted_iota(jnp.int32, (SCAN_SEGS, 1), 0)
    h = jnp.where(seg == 0, hc_ref[...], pltpu.roll(h_end, 1, 0))
    for i in range(SCAN_STEPS):
        h = rows_of(a, i) * h + rows_of(u, i)
        dst = pl.ds(i, SCAN_SEGS, stride=SCAN_PITCH)
        for c in range(LRU_CHUNKS):
            hpad_ref[c, dst, :] = h[:, c * LANES:(c + 1) * LANES]
    hs = jnp.concatenate(
        [jnp.concatenate([hpad_ref[c, SCAN_PITCH * s:SCAN_PITCH * s + SCAN_STEPS, :] for c in range(LRU_CHUNKS)],
                         axis=-1) for s in range(SCAN_SEGS)], axis=0)
    yrec_ref[0] = (jax.nn.gelu(yr) * hs).astype(BF16)
    hc_ref[...] = h_end[SCAN_SEGS - 1:SCAN_SEGS, :]
    plru_ref[0] = h_end[SCAN_SEGS - 1:SCAN_SEGS, :]
    pconv_ref[0] = xr[rows - hist:rows, :]
    for c in range(LRU_CHUNKS):
        xpad_ref[c, tail, :] = xr[rows - SUBLANES:rows, c * LANES:(c + 1) * LANES]


def _softmax_parts(s):
    m = jnp.max(s, axis=-1, keepdims=True)
    p = jnp.exp(s - m)
    return p.astype(BF16), jnp.sum(p, axis=-1, keepdims=True), m


def _gate(hn, wc_ref, bg_ref, gate_col, idx):
    lo = idx * D_MODEL
    return jax.nn.sigmoid(_dot(hn, wc_ref[:, gate_col + lo:gate_col + lo + D_MODEL]) + bg_ref[:, lo:lo + D_MODEL])


def _p_mix_body(x_ref, q0, q1, q2, k0, k1, k2, v0, v1, v2, yr_ref, mkv_ref, b0, b1, b2,
                g_ref, wc_ref, bg_ref, wao_ref, wro_ref, wmo_ref, wout_ref,
                o_ref, kh0, kh1, kh2, vh0, vh1, vh2, o_scr, l_scr):
    tile = MIX_TILE
    t = pl.program_id(1)
    slot = lax.rem(t, 2)
    qs_, ks_, vs_ = (q0, q1, q2), (k0, k1, k2), (v0, v1, v2)
    khs, vhs, biases = (kh0, kh1, kh2), (vh0, vh1, vh2), (b0, b1, b2)

    @pl.when(t == 0)
    def _():
        for h_ref in khs + vhs:
            h_ref[...] = jnp.zeros(h_ref.shape, BF16)

    x = x_ref[0]
    hn = _rms(x, g_ref[...]).astype(BF16)
    qm = _dot(hn, wc_ref[:, 0:MEM_W])
    res = {}

    blocks, carried = [], []

    def window_block(g, dil, r, lo, qrows, keys, tab):
        def scores():
            q = qs_[g][0, r, lo:lo + qrows, :]
            return _softmax_parts(_dot_nt(_stack_heads(q), khs[g][slot, keys, :]) + tab)

        def values(parts):
            p, l, m = parts
            o = _dot(p, vhs[g][slot, keys, :]) * (1.0 / l)
            lse = jnp.broadcast_to(m + jnp.log(l), o.shape)
            o_c, l_c = _unstack_heads(o, qrows), _unstack_heads(lse, qrows)
            dst = slice(lo, lo + qrows) if dil == 1 else pl.ds(lo * dil + r, qrows, stride=dil)
            for c in range(ATT_CHUNKS):
                o_scr[g * ATT_CHUNKS + c, dst, :] = o_c[c]
                l_scr[g * ATT_CHUNKS + c, dst, :] = l_c[c]
        return scores, values

    for g, (window, dil) in enumerate(GROUPS):
        cur = tile // dil
        qrows = min(cur, N_REL)
        hist = 2 * N_REL - qrows
        for r in range(dil):
            base = r * (hist + cur)
            khs[g][slot, base + hist:base + hist + cur, :] = ks_[g][0, r]
            vhs[g][slot, base + hist:base + hist + cur, :] = vs_[g][0, r]
            carried.append((g, base, hist, cur))
            for j in range(cur // qrows):
                if cur < N_REL:
                    tab = biases[g][jnp.minimum(t, biases[g].shape[0] - 1)]
                elif j == 0:
                    tab = biases[g][jnp.where(t == 0, 1, 0)]
                else:
                    tab = biases[g][0]
                lo = j * qrows
                blocks.append(window_block(g, dil, r, lo, qrows, slice(base + lo, base + lo + 2 * N_REL), tab))

    def memory_block(h):
        sl = slice(h * MEM_HEAD_DIM, (h + 1) * MEM_HEAD_DIM)

        def scores():
            return _softmax_parts(_dot_nt(qm[:, sl].astype(BF16), mkv_ref[0, :, sl]) * MEM_SCALE)

        def values(parts):
            p, l, _ = parts
            res["mem", h] = _dot(p, mkv_ref[0, :, MEM_W + h * MEM_HEAD_DIM:MEM_W + (h + 1) * MEM_HEAD_DIM]) * (1.0 / l)
        return scores, values

    blocks += [memory_block(h) for h in range(MEM_HEADS)]

    def rec_term():
        res["rec"] = _gate(hn, wc_ref, bg_ref, MEM_W, 1) * _dot(yr_ref[0], wro_ref[...])

    def att_gate():
        res["att_gate"] = _gate(hn, wc_ref, bg_ref, MEM_W, 0)

    def mem_gate():
        res["mem_gate"] = _gate(hn, wc_ref, bg_ref, MEM_W, 2)

    fillers = [rec_term, att_gate, mem_gate]
    every = len(blocks) // (len(fillers) + 1)

    pending = []
    for i, (scores, values) in enumerate(blocks):
        pending.append((values, scores()))
        if len(pending) > ATTN_DEPTH:
            done, parts = pending.pop(0)
            done(parts)
        if fillers and (i + 1) % every == 0:
            fillers.pop(0)()
    for done, parts in pending:
        done(parts)
    for f in fillers:
        f()

    mixed = []
    for c in range(ATT_CHUNKS):
        l0, l1, l2 = (l_scr[g * ATT_CHUNKS + c] for g in range(len(GROUPS)))
        o0, o1, o2 = (o_scr[g * ATT_CHUNKS + c] for g in range(len(GROUPS)))
        m = jnp.maximum(jnp.maximum(l0, l1), l2)
        w0, w1, w2 = jnp.exp(l0 - m), jnp.exp(l1 - m), jnp.exp(l2 - m)
        mixed.append((w0 * o0 + w1 * o1 + w2 * o2) / (w0 + w1 + w2))
    ya = jnp.concatenate(mixed, axis=-1).astype(BF16)
    ym = jnp.concatenate([res["mem", h] for h in range(MEM_HEADS)], axis=-1).astype(BF16)
    merged = (res["att_gate"] * _dot(ya, wao_ref[...]) + res["rec"]
              + res["mem_gate"] * _dot(ym, wmo_ref[...]))
    o_ref[0] = x + _dot(merged.astype(BF16), wout_ref[...])
    for g, base, hist, cur in carried:
        for h_ref in (khs[g], vhs[g]):
            h_ref[1 - slot, base:base + hist, :] = h_ref[slot, base + cur:base + cur + hist, :]


def _merge_out(x, hn, branches, wc_ref, gate_col, bg_ref, wout_ref):
    merged = None
    for idx, (y, w_ref) in enumerate(branches):
        term = _gate(hn, wc_ref, bg_ref, gate_col, idx) * _dot(y, w_ref[...])
        merged = term if merged is None else merged + term
    return x + _dot(merged.astype(BF16), wout_ref[...])


def _merge_body(x_ref, ya_ref, yr_ref, ym_ref, g_ref, wc_ref, bg_ref, wao_ref, wro_ref, wmo_ref, wout_ref, o_ref):
    x = x_ref[0]
    hn = _rms(x, g_ref[...]).astype(BF16)
    o_ref[0] = _merge_out(x, hn, ((ya_ref[0], wao_ref), (yr_ref[0], wro_ref), (ym_ref[0], wmo_ref)),
                          wc_ref, 0, bg_ref, wout_ref)


def _s_mix_body(z_ref, ck0, cv0, ck1, cv1, ck2, cv2, sconv_ref, slru_ref, cmk_ref, cmv_ref,
                bc0, bc1, bc2, bn_ref, cw_ref, cb_ref, wa_ref, wi_ref, ba_ref, bi_ref, lam_ref,
                ya_ref, yrec_ref, ym_ref, ok0, ov0, ok1, ov1, ok2, ov2, oconv_ref, olru_ref,
                ext_ref, *, steps):
    cks, cvs = (ck0, ck1, ck2), (cv0, cv1, cv2)
    oks, ovs = (ok0, ok1, ok2), (ov0, ov1, ov2)
    bcs = (bc0, bc1, bc2)
    zeros = jnp.zeros((N_REL - steps, ATT_W), F32)
    new_lanes = lax.broadcasted_iota(jnp.int32, (ATT_W, LANES), 1) >= LANES - steps
    outs, lses = [], []
    for g, (window, dil) in enumerate(GROUPS):
        q = z_ref[0, :, g * ATT_W:(g + 1) * ATT_W] * ATT_SCALE
        kn = z_ref[0, :, (3 + g) * ATT_W:(4 + g) * ATT_W]
        vn = z_ref[0, :, (6 + g) * ATT_W:(7 + g) * ATT_W]
        qs = _stack_heads(q).astype(BF16)
        kt = cks[g][0]
        vt = cvs[g][0]
        s_c = _dot(qs, kt.astype(BF16)) + bcs[g][...]
        s_n = _dot_nt(qs, jnp.concatenate([kn, zeros], axis=0).astype(BF16)) + bn_ref[g]
        m = jnp.maximum(jnp.max(s_c, axis=-1, keepdims=True), jnp.max(s_n, axis=-1, keepdims=True))
        p_c = jnp.exp(s_c - m)
        p_n = jnp.exp(s_n - m)
        l = jnp.sum(p_c, axis=-1, keepdims=True) + jnp.sum(p_n, axis=-1, keepdims=True)
        o = (_dot_nt(p_c.astype(BF16), vt.astype(BF16))
             + _dot(p_n.astype(BF16), jnp.concatenate([vn, zeros], axis=0).astype(BF16))) * (1.0 / l)
        lse = jnp.broadcast_to(m + jnp.log(l), o.shape)
        outs.append(jnp.concatenate(_unstack_heads(o, steps), axis=-1))
        lses.append(jnp.concatenate(_unstack_heads(lse, steps), axis=-1))
        for old, new, dst in ((kt, kn, oks[g]), (vt, vn, ovs[g])):
            rolled = pltpu.roll(old, window - steps, 1)
            new_t = jnp.transpose(jnp.concatenate([zeros, new], axis=0))
            if window > LANES:
                dst[0, :, 0:window - LANES] = rolled[:, 0:window - LANES]
            dst[0, :, window - LANES:window] = jnp.where(new_lanes, new_t, rolled[:, window - LANES:window])
    m = jnp.maximum(jnp.maximum(lses[0], lses[1]), lses[2])
    ws = [jnp.exp(l - m) for l in lses]
    ya_ref[0] = ((ws[0] * outs[0] + ws[1] * outs[1] + ws[2] * outs[2]) / (ws[0] + ws[1] + ws[2])).astype(BF16)

    hist = CONV_W - 1
    ext_ref[0:SUBLANES, :] = jnp.zeros((SUBLANES, LRU_W), F32)
    ext_ref[SUBLANES - hist:SUBLANES, :] = sconv_ref[0]
    ext_ref[SUBLANES:SUBLANES + steps, :] = z_ref[0, :, COL_XR:COL_XR + LRU_W]
    taps = [ext_ref[SUBLANES - j:SUBLANES - j + steps, :] for j in range(CONV_W)]
    a, u = _lru_gates(_conv_taps(cw_ref, cb_ref, taps), wa_ref, wi_ref, ba_ref, bi_ref, lam_ref)
    a_c, u_c = _linear_scan(a, u, steps)
    h = a_c * slru_ref[0] + u_c
    yrec_ref[0] = (jax.nn.gelu(z_ref[0, :, COL_YR:COL_YR + LRU_W]) * h).astype(BF16)
    olru_ref[0] = h[steps - 1:steps, :]
    oconv_ref[0] = ext_ref[SUBLANES + steps - hist:SUBLANES + steps, :]

    qm = z_ref[0, :, COL_QM:COL_QM + MEM_W]
    mk = [cmk_ref[0, pl.ds(h, N_MEM, stride=MEM_HEADS), :].astype(BF16) for h in range(MEM_HEADS)]
    mv = [cmv_ref[0, pl.ds(h, N_MEM, stride=MEM_HEADS), :].astype(BF16) for h in range(MEM_HEADS)]
    ym_ref[0] = _mem_attention(qm, mk, mv).astype(BF16)


def _resident(shape):
    zeros = (0,) * len(shape)
    return pl.BlockSpec(shape, lambda *_: zeros, pipeline_mode=pl.Buffered(1))


def _params(*semantics):
    return pltpu.CompilerParams(dimension_semantics=semantics, vmem_limit_bytes=V7X_VMEM_LIMIT)


def _ffn(x, norm_g, w_gu, w_down, final_g=None):
    n = x.shape[0]
    tile = min(FFN_TILE, n)
    row = pl.BlockSpec((tile, D_MODEL), lambda i: (i, 0))
    in_specs = [row, _resident((1, D_MODEL)), _resident(w_gu.shape), _resident(w_down.shape)]
    args = [x, norm_g, w_gu, w_down]
    if final_g is not None:
        in_specs.append(_resident((1, D_MODEL)))
        args.append(final_g)
    return pl.pallas_call(
        functools.partial(_ffn_body, final_norm=final_g is not None),
        grid=(n // tile,),
        in_specs=in_specs,
        out_specs=row,
        out_shape=jax.ShapeDtypeStruct((n, D_MODEL), F32),
        scratch_shapes=[pltpu.VMEM((tile, D_FF), BF16)],
        compiler_params=_params("parallel"),
        name="ffn_final" if final_g is not None else "ffn",
    )(*args)


def _norm_mm(x, norm_g, w, row_tile, col_tile):
    n, cols = x.shape[0], w.shape[1]
    return pl.pallas_call(
        _norm_mm_body,
        grid=(n // row_tile, cols // col_tile),
        in_specs=[pl.BlockSpec((row_tile, D_MODEL), lambda i, j: (i, 0)),
                  _resident((1, D_MODEL)),
                  pl.BlockSpec((D_MODEL, col_tile), lambda i, j: (0, j))],
        out_specs=pl.BlockSpec((row_tile, col_tile), lambda i, j: (i, j)),
        out_shape=jax.ShapeDtypeStruct((n, cols), F32),
        compiler_params=_params("parallel", "parallel"),
        name="norm_mm",
    )(x, norm_g, w)


def _mem_kv(mem, norm_g, w_kv, row_tile):
    n = mem.shape[0]
    state = jax.ShapeDtypeStruct((n * MEM_HEADS, MEM_HEAD_DIM), F32)
    state_spec = pl.BlockSpec((row_tile * MEM_HEADS, MEM_HEAD_DIM), lambda i: (i, 0))
    return pl.pallas_call(
        _mem_kv_body,
        grid=(n // row_tile,),
        in_specs=[pl.BlockSpec((row_tile, D_MODEL), lambda i: (i, 0)),
                  _resident((1, D_MODEL)), _resident(w_kv.shape)],
        out_specs=[state_spec, state_spec, pl.BlockSpec((row_tile, 2 * MEM_W), lambda i: (i, 0))],
        out_shape=[state, state, jax.ShapeDtypeStruct((n, 2 * MEM_W), BF16)],
        compiler_params=_params("parallel"),
        name="mem_kv",
    )(mem, norm_g, w_kv)


def _p_proj(x1, norm_g, w_a, lru):
    batch, seq, _ = x1.shape
    tile = PROJ_TILE
    nt = seq // tile
    qkv_shapes, qkv_specs = [], []
    for _kind in range(3):
        for window, dil in GROUPS:
            qkv_shapes.append(jax.ShapeDtypeStruct((batch, dil, seq // dil, ATT_W), BF16))
            qkv_specs.append(pl.BlockSpec((1, dil, tile // dil, ATT_W), lambda b, t: (b, 0, t, 0)))
    win_shapes, win_specs = [], []
    for window, dil in GROUPS:
        keep = min(window, seq)
        blk = min(keep, tile)
        first = nt - max(keep // tile, 1)
        for _kv in range(2):
            win_shapes.append(jax.ShapeDtypeStruct((batch, ATT_W, keep), F32))
            win_specs.append(pl.BlockSpec((1, ATT_W, blk),
                                          lambda b, t, first=first: (b, 0, jnp.maximum(t - first, 0))))
    out_shapes = qkv_shapes + win_shapes + [
        jax.ShapeDtypeStruct((batch, seq, LRU_W), BF16),
        jax.ShapeDtypeStruct((batch, CONV_W - 1, LRU_W), F32),
        jax.ShapeDtypeStruct((batch, 1, LRU_W), F32)]
    out_specs = qkv_specs + win_specs + [
        pl.BlockSpec((1, tile, LRU_W), lambda b, t: (b, t, 0)),
        pl.BlockSpec((1, CONV_W - 1, LRU_W), lambda b, t: (b, 0, 0)),
        pl.BlockSpec((1, 1, LRU_W), lambda b, t: (b, 0, 0))]
    in_specs = [pl.BlockSpec((1, tile, D_MODEL), lambda b, t: (b, t, 0)),
                _resident((1, D_MODEL)), _resident(w_a.shape)] + [_resident(a.shape) for a in lru]
    return pl.pallas_call(
        _p_proj_body,
        grid=(batch, nt),
        in_specs=in_specs,
        out_specs=out_specs,
        out_shape=out_shapes,
        scratch_shapes=[pltpu.VMEM((3 * (len(GROUPS) - 1), ATT_CHUNKS, tile, LANES), F32),
                        pltpu.VMEM((LRU_CHUNKS, SCAN_PITCH * (SCAN_SEGS + 1), LANES), F32),
                        pltpu.VMEM((LRU_CHUNKS, SCAN_PITCH * SCAN_SEGS, LANES), F32),
                        pltpu.VMEM((1, LRU_W), F32)],
        compiler_params=_params("parallel", "arbitrary"),
        name="p_proj",
    )(x1, norm_g, w_a, *lru)


def _p_mix(x1, qkv, y_rec, mkv, biases, norm_g, w_c, b_gate, w_att_o, w_rec_o, w_mem_o, w_out):
    batch, seq, _ = x1.shape
    tile = MIX_TILE

    def tok(width):
        return pl.BlockSpec((1, tile, width), lambda b, t: (b, t, 0))

    qkv_specs, hist = [], []
    for window, dil in GROUPS:
        cur = tile // dil
        qkv_specs.append(pl.BlockSpec((1, dil, cur, ATT_W), lambda b, t: (b, 0, t, 0)))
        hist.append(pltpu.VMEM((2, dil * (2 * N_REL - min(cur, N_REL) + cur), ATT_W), BF16))
    weights = [norm_g, w_c, b_gate, w_att_o, w_rec_o, w_mem_o, w_out]
    in_specs = ([tok(D_MODEL)] + qkv_specs * 3
                + [tok(LRU_W), pl.BlockSpec((1, N_MEM, 2 * MEM_W), lambda b, t: (b, 0, 0))]
                + [_resident(a.shape) for a in list(biases) + weights])
    return pl.pallas_call(
        _p_mix_body,
        grid=(batch, seq // tile),
        in_specs=in_specs,
        out_specs=tok(D_MODEL),
        out_shape=jax.ShapeDtypeStruct((batch, seq, D_MODEL), F32),
        scratch_shapes=hist * 2 + [pltpu.VMEM((len(GROUPS) * ATT_CHUNKS, tile, LANES), F32)] * 2,
        compiler_params=_params("parallel", "arbitrary"),
        name="p_mix",
    )(x1, *qkv, y_rec, mkv, *biases, *weights)


def _merge(x1, y_att, y_rec, y_mem, norm_g, w_c, b_gate, w_att_o, w_rec_o, w_mem_o, w_out):
    batch, seq, _ = x1.shape
    tile = seq

    def tok(width):
        return pl.BlockSpec((1, tile, width), lambda b, t: (b, t, 0))

    weights = [norm_g, w_c, b_gate, w_att_o, w_rec_o, w_mem_o, w_out]
    return pl.pallas_call(
        _merge_body,
        grid=(batch, seq // tile),
        in_specs=[tok(D_MODEL), tok(ATT_W), tok(LRU_W), tok(MEM_W)] + [_resident(w.shape) for w in weights],
        out_specs=tok(D_MODEL),
        out_shape=jax.ShapeDtypeStruct((batch, seq, D_MODEL), F32),
        compiler_params=_params("parallel", "parallel"),
        name="merge_s",
    )(x1, y_att, y_rec, y_mem, *weights)


def _s_mix(z, caches, state_conv, state_lru, cmk, cmv, bias_c, bias_n, lru):
    batch, steps, _ = z.shape

    def per_batch(shape):
        return pl.BlockSpec((1,) + tuple(shape[1:]), lambda b: (b,) + (0,) * (len(shape) - 1))

    ins = [z] + list(caches) + [state_conv, state_lru, cmk, cmv]
    in_specs = [per_batch(a.shape) for a in ins]
    consts = list(bias_c) + [bias_n] + list(lru)
    in_specs += [_resident(a.shape) for a in consts]
    out_shapes = [jax.ShapeDtypeStruct((batch, steps, ATT_W), BF16),
                  jax.ShapeDtypeStruct((batch, steps, LRU_W), BF16),
                  jax.ShapeDtypeStruct((batch, steps, MEM_W), BF16)]
    out_shapes += [jax.ShapeDtypeStruct(c.shape, F32) for c in caches]
    out_shapes += [jax.ShapeDtypeStruct(state_conv.shape, F32), jax.ShapeDtypeStruct(state_lru.shape, F32)]
    return pl.pallas_call(
        functools.partial(_s_mix_body, steps=steps),
        grid=(batch,),
        in_specs=in_specs,
        out_specs=[per_batch(s.shape) for s in out_shapes],
        out_shape=out_shapes,
        scratch_shapes=[pltpu.VMEM((2 * SUBLANES, LRU_W), F32)],
        compiler_params=_params("parallel"),
        name="s_mix",
    )(*ins, *consts)


def _bucket(n):
    n = np.maximum(n, 0)
    max_exact = NUM_BUCKETS // 2
    nf = np.maximum(n, 1).astype(np.float32)
    large = max_exact + (np.log(nf / max_exact) / np.float32(math.log(MAX_DISTANCE / max_exact))
                         * (NUM_BUCKETS - max_exact)).astype(np.int32)
    return np.where(n < max_exact, n, np.minimum(large, NUM_BUCKETS - 1))


def _dist_bias(rel_bias, g, dist, valid):
    tab = rel_bias[:, g * HEADS:(g + 1) * HEADS].T[:, _bucket(dist)]
    return jnp.where(valid[None], tab, -jnp.inf).astype(F32)


def _toeplitz(u, rows, cols):
    lu = rows + cols - 1
    ue = jnp.concatenate([u, u[:, :1]], axis=1)
    m = jnp.tile(ue, (1, rows))[:, :rows * lu].reshape(-1, rows, lu)
    return m[:, :, rows - 1:rows - 1 + cols].reshape(-1, cols)


def _prompt_bias(rel_bias, tile):
    col = np.arange(2 * N_REL)
    stacks = []
    for g, (_, dil) in enumerate(GROUPS):
        cur = tile // dil
        qrows = min(cur, N_REL)
        hist = 2 * N_REL - qrows
        delta = np.arange(-(qrows - 1), 2 * N_REL)
        v = _dist_bias(rel_bias, g, dil * delta, (delta >= 0) & (delta <= N_REL))
        base = _toeplitz(v[:, ::-1], qrows, 2 * N_REL)
        present = [hist, 0] if cur >= N_REL else list(range(0, hist + 1, cur))
        stacks.append(jnp.stack([jnp.where(col >= hist - n, base, -jnp.inf) for n in present]))
    return stacks


def _sample_bias(rel_bias, steps):
    t = np.arange(steps)[:, None]
    cache, new = [], []
    for g, (window, dil) in enumerate(GROUPS):
        dist = np.arange(window + steps)
        tab = _dist_bias(rel_bias, g, dist, (dist % dil == 0) & (dist <= window))
        cache.append(_toeplitz(tab[:, 1:][:, ::-1], steps, window))
        j = np.arange(N_REL)[None, :]
        dist = t - j
        valid = (j < steps) & (dist >= 0) & (dist % dil == 0)
        new.append(_dist_bias(rel_bias, g, dist, valid).reshape(HEADS * steps, N_REL))
    return cache, jnp.stack(new)


def _block_diag(w):
    per = MXU_TILE // LRU_BLOCK
    w4 = w.reshape(-1, per, LRU_BLOCK, LRU_BLOCK)
    bd = jnp.einsum('jnef,nm->jnemf', w4, jnp.eye(per, dtype=w.dtype))
    return bd.reshape(-1, MXU_TILE, MXU_TILE).astype(BF16)


def _to_pos_minor(c):
    return jnp.transpose(c[0], (0, 2, 3, 1)).reshape(c.shape[1], ATT_W, c.shape[2])


def _from_pos_minor(c):
    return jnp.transpose(c.reshape(c.shape[0], HEADS, HEAD_DIM, c.shape[2]), (0, 3, 1, 2))[None]


def kernel(x_prompt, x_sample, mem_prompt, cache_win_k0, cache_win_v0, cache_win_k1, cache_win_v1, cache_win_k2, cache_win_v2, state_conv, state_lru, cache_mem_k, cache_mem_v, rel_bias, ffn1_norm, ffn1_w_gu, ffn1_w_down, mix_norm, w_in, b_gate, conv_w, conv_b, lru_w_a, lru_b_a, lru_w_i, lru_b_i, lru_lambda, mem_norm, w_mem_kv, w_att_o, w_rec_o, w_mem_o, w_out, ffn2_norm, ffn2_w_gu, ffn2_w_down, final_norm):
    depth = ffn1_norm.shape[0]
    assert depth == 1
    l = 0
    batch, seq, _ = x_prompt.shape
    sbatch, steps, _ = x_sample.shape
    row = lambda v: v.reshape(1, -1)

    lru = (conv_w[l], row(conv_b[l]), _block_diag(lru_w_a[l]), _block_diag(lru_w_i[l]),
           row(lru_b_a[l]), row(lru_b_i[l]), row(lru_lambda[l]))
    ffn1 = (row(ffn1_norm[l]), ffn1_w_gu[l].astype(BF16), ffn1_w_down[l].astype(BF16))
    ffn2 = (row(ffn2_norm[l]), ffn2_w_gu[l].astype(BF16), ffn2_w_down[l].astype(BF16))
    outw = (row(b_gate[l]), w_att_o[l].astype(BF16), w_rec_o[l].astype(BF16), w_mem_o[l].astype(BF16),
            w_out[l].astype(BF16))
    mixg = row(mix_norm[l])
    fin = row(final_norm)

    xp1 = _ffn(x_prompt.reshape(batch * seq, D_MODEL), *ffn1).reshape(batch, seq, D_MODEL)
    p_mk, p_mv, mkv = _mem_kv(mem_prompt.reshape(batch * N_MEM, D_MODEL), row(mem_norm[l]),
                              w_mem_kv[l].astype(BF16), row_tile=2 * N_MEM)
    proj = _p_proj(xp1, mixg, w_in[l][:, :COL_QM].astype(BF16), lru)
    qkv, wins = proj[0:9], proj[9:15]
    y_rec, p_conv, p_lru = proj[15], proj[16], proj[17]
    xp2 = _p_mix(xp1, qkv, y_rec, mkv.reshape(batch, N_MEM, 2 * MEM_W), _prompt_bias(rel_bias, MIX_TILE), mixg,
                 w_in[l][:, COL_QM:].astype(BF16), *outw)
    y_prompt = _ffn(xp2.reshape(batch * seq, D_MODEL), *ffn2, final_g=fin).reshape(batch, seq, D_MODEL)

    n_s = sbatch * steps
    xs1 = _ffn(x_sample.reshape(n_s, D_MODEL), *ffn1)
    z = _norm_mm(xs1, mixg, w_in[l].astype(BF16), row_tile=n_s, col_tile=IN_COLS // 2)
    caches = [_to_pos_minor(c) for c in
              (cache_win_k0, cache_win_v0, cache_win_k1, cache_win_v1, cache_win_k2, cache_win_v2)]
    bias_c, bias_n = _sample_bias(rel_bias, steps)
    mem_rows = lambda c: c[l].reshape(sbatch, N_MEM * MEM_HEADS, MEM_HEAD_DIM)
    mix = _s_mix(z.reshape(sbatch, steps, IN_COLS), caches, state_conv[l], state_lru[l].reshape(sbatch, 1, LRU_W),
                 mem_rows(cache_mem_k), mem_rows(cache_mem_v), bias_c, bias_n, lru)
    s_att, s_rec, s_mem = (a.reshape(1, n_s, a.shape[-1]) for a in mix[0:3])
    xs2 = _merge(xs1.reshape(1, n_s, D_MODEL), s_att, s_rec, s_mem, mixg, w_in[l][:, COL_GATE:].astype(BF16),
                 *outw)
    y_sample = _ffn(xs2.reshape(n_s, D_MODEL), *ffn2, final_g=fin).reshape(sbatch, steps, D_MODEL)

    mem_state = lambda a: a.reshape(1, batch, N_MEM, MEM_HEADS, MEM_HEAD_DIM)
    return (y_prompt, y_sample, *[_from_pos_minor(w) for w in wins],
            p_conv[None], p_lru.reshape(1, batch, LRU_W), mem_state(p_mk), mem_state(p_mv),
            *[_from_pos_minor(w) for w in mix[3:9]], mix[9][None], mix[10].reshape(1, sbatch, LRU_W))
```

```python
import functools
import math

import numpy as np
import jax
import jax.numpy as jnp
from jax import lax
from jax.experimental import pallas as pl
from jax.experimental.pallas import tpu as pltpu

F32 = jnp.float32
BF16 = jnp.bfloat16

D_MODEL = 1024
D_FF = 2816
HEAD_DIM = 64
HEADS = 4
GROUPS = ((128, 1), (512, 4), (2048, 16))
N_REL = 128
ATT_W = HEADS * HEAD_DIM
ATT_SCALE = HEAD_DIM ** -0.5
NUM_BUCKETS = 32
MAX_DISTANCE = 2048
LRU_W = 768
LRU_BLOCK = 64
LRU_C = 8.0
CONV_W = 4
N_MEM = 256
MEM_HEADS = 4
MEM_HEAD_DIM = 128
MEM_W = MEM_HEADS * MEM_HEAD_DIM
MEM_SCALE = MEM_HEAD_DIM ** -0.5
RMS_EPS = 1e-6
QKV_W = 3 * len(GROUPS) * ATT_W
COL_XR = QKV_W
COL_YR = COL_XR + LRU_W
COL_QM = COL_YR + LRU_W
COL_GATE = COL_QM + MEM_W
IN_COLS = COL_GATE + 3 * D_MODEL

V7X_VMEM_LIMIT = 62 * 1024 * 1024
MXU_TILE = 256
SUBLANES = 8
LANES = 128
FFN_TILE = 512
FFN_CHUNK = 256
PROJ_TILE = 512
SCAN_SEGS = SUBLANES
SCAN_STEPS = FFN_TILE // SCAN_SEGS
SCAN_PITCH = SCAN_STEPS + SUBLANES
MIX_TILE = 512
ATTN_DEPTH = 6
LRU_CHUNKS = LRU_W // LANES
ATT_CHUNKS = ATT_W // LANES


def _dot(a, b):
    return jnp.dot(a, b, preferred_element_type=F32)


def _dot_nt(a, b):
    return lax.dot_general(a, b, (((1,), (1,)), ((), ())), preferred_element_type=F32)


def _rms(x, g):
    return x * lax.rsqrt(jnp.mean(x * x, axis=-1, keepdims=True) + RMS_EPS) * g


def _stack_heads(q):
    lane = lax.broadcasted_iota(jnp.int32, q.shape, 1)
    return jnp.concatenate(
        [jnp.where((lane >= h * HEAD_DIM) & (lane < (h + 1) * HEAD_DIM), q, 0.0) for h in range(HEADS)], axis=0)


def _unstack_heads(x, rows):
    low = lax.broadcasted_iota(jnp.int32, (rows, LANES), 1) < HEAD_DIM
    per = LANES // HEAD_DIM
    return [jnp.where(low,
                      x[(per * c) * rows:(per * c + 1) * rows, c * LANES:(c + 1) * LANES],
                      x[(per * c + 1) * rows:(per * c + 2) * rows, c * LANES:(c + 1) * LANES])
            for c in range(ATT_CHUNKS)]


def _linear_scan(a, u, rows):
    row = lax.broadcasted_iota(jnp.int32, (rows, 1), 0)
    d = 1
    while d < rows:
        a_s = pltpu.roll(a, d, 0)
        u_s = pltpu.roll(u, d, 0)
        keep = row >= d
        u = jnp.where(keep, a * u_s + u, u)
        a = jnp.where(keep, a * a_s, a)
        d *= 2
    return a, u


def _zero_after(x):
    bits = pltpu.bitcast(x, jnp.uint32)
    half = jnp.uint32(17)
    return pltpu.bitcast(lax.shift_right_logical(lax.shift_right_logical(bits, half), half), F32)


def _lru_preact(xc, wa_ref, wi_ref):
    xcb = xc.astype(BF16)
    ra, ia = [], []
    for j in range(LRU_W // MXU_TILE):
        blk = xcb[:, j * MXU_TILE:(j + 1) * MXU_TILE]
        ra.append(_dot(blk, wa_ref[j]))
        ia.append(_dot(blk, wi_ref[j]))
    return jnp.concatenate(ra, axis=-1), jnp.concatenate(ia, axis=-1)


def _lru_decay_input(xc, ra, ia, ba_ref, bi_ref, lam_ref):
    r = jax.nn.sigmoid(ra + ba_ref[...])
    i = jax.nn.sigmoid(ia + bi_ref[...])
    nl = -lam_ref[...]
    softplus = jnp.maximum(nl, 0.0) + jnp.log1p(jnp.exp(-jnp.abs(nl)))
    log_a = (-LRU_C * r) * softplus
    a = jnp.exp(log_a)
    u = jnp.sqrt(1.0 - a * a) * (i * xc)
    return a, u


def _lru_gates(xc, wa_ref, wi_ref, ba_ref, bi_ref, lam_ref):
    return _lru_decay_input(xc, *_lru_preact(xc, wa_ref, wi_ref), ba_ref, bi_ref, lam_ref)


def _conv_taps(cw_ref, cb_ref, taps):
    xc = cb_ref[...] + cw_ref[CONV_W - 1:CONV_W, :] * taps[0]
    for j in range(1, CONV_W):
        xc = xc + cw_ref[CONV_W - 1 - j:CONV_W - j, :] * taps[j]
    return xc


def _mem_attention(qm, mk_heads, mv_heads):
    heads = []
    for h in range(MEM_HEADS):
        q = qm[:, h * MEM_HEAD_DIM:(h + 1) * MEM_HEAD_DIM].astype(BF16)
        s = _dot_nt(q, mk_heads[h]) * MEM_SCALE
        m = jnp.max(s, axis=-1, keepdims=True)
        p = jnp.exp(s - m)
        l = jnp.sum(p, axis=-1, keepdims=True)
        heads.append(_dot(p.astype(BF16), mv_heads[h]) * (1.0 / l))
    return jnp.concatenate(heads, axis=-1)


def _ffn_body(x_ref, g_ref, wgu_ref, wd_ref, *rest, final_norm):
    if final_norm:
        fg_ref, o_ref, act_ref = rest
    else:
        o_ref, act_ref = rest
    x = x_ref[...]
    hn = _rms(x, g_ref[...]).astype(BF16)
    for c in range(D_FF // FFN_CHUNK):
        lo = c * FFN_CHUNK
        gate = _dot(hn, wgu_ref[:, lo:lo + FFN_CHUNK])
        up = _dot(hn, wgu_ref[:, D_FF + lo:D_FF + lo + FFN_CHUNK])
        act_ref[:, lo:lo + FFN_CHUNK] = (gate * jax.nn.sigmoid(gate) * up).astype(BF16)
    y = x + 0.5 * _dot(act_ref[...], wd_ref[...])
    if final_norm:
        y = _rms(y, fg_ref[...])
    o_ref[...] = y


def _norm_mm_body(x_ref, g_ref, w_ref, o_ref):
    hn = _rms(x_ref[...], g_ref[...]).astype(BF16)
    o_ref[...] = _dot(hn, w_ref[...])


def _mem_kv_body(x_ref, g_ref, w_ref, mk_ref, mv_ref, mkv_ref):
    rows = x_ref.shape[0]
    z = _dot(_rms(x_ref[...], g_ref[...]).astype(BF16), w_ref[...])
    mkv_ref[...] = z.astype(BF16)
    for h in range(MEM_HEADS):
        dst = pl.ds(h, rows, stride=MEM_HEADS)
        mk_ref[dst, :] = z[:, h * MEM_HEAD_DIM:(h + 1) * MEM_HEAD_DIM]
        mv_ref[dst, :] = z[:, MEM_W + h * MEM_HEAD_DIM:MEM_W + (h + 1) * MEM_HEAD_DIM]


def _ffn_rec_body(x_ref, g_ref, wgu_ref, wd_ref, mg_ref, wxy_ref, cw_ref, cb_ref, wa_ref, wi_ref, ba_ref, bi_ref,
                  lam_ref, o_ref, yrec_ref, pconv_ref, plru_ref, act_ref, xbuf_ref, xpad_ref, hpad_ref, hc_ref,
                  *, tiles_per_seq):
    rows = FFN_TILE
    i = pl.program_id(0)
    t = lax.rem(jnp.maximum(i - 1, 0), tiles_per_seq)
    slot = lax.rem(i, 2)
    tail = slice(SCAN_STEPS - SUBLANES, SCAN_STEPS)

    @pl.when(i == 0)
    def _():
        xbuf_ref[...] = jnp.zeros(xbuf_ref.shape, F32)

    @pl.when(t == 0)
    def _():
        xpad_ref[:, tail, :] = jnp.zeros((LRU_CHUNKS, SUBLANES, LANES), F32)
        hc_ref[...] = jnp.zeros((1, LRU_W), F32)

    hn_p = _rms(xbuf_ref[1 - slot], mg_ref[...]).astype(BF16)
    xr = _dot(hn_p, wxy_ref[:, 0:LRU_W])
    for c in range(LRU_CHUNKS):
        for s in range(SCAN_SEGS):
            lo = SCAN_PITCH * (s + 1)
            xpad_ref[c, lo:lo + SCAN_STEPS, :] = xr[s * SCAN_STEPS:(s + 1) * SCAN_STEPS, c * LANES:(c + 1) * LANES]

    x = x_ref[...]
    hn = _rms(x, g_ref[...]).astype(BF16)

    def ffn_chunk(c):
        lo = c * FFN_CHUNK
        gate = _dot(hn, wgu_ref[:, lo:lo + FFN_CHUNK])
        up = _dot(hn, wgu_ref[:, D_FF + lo:D_FF + lo + FFN_CHUNK])
        act_ref[:, lo:lo + FFN_CHUNK] = (gate * jax.nn.sigmoid(gate) * up).astype(BF16)
        return gate

    n_chunks = D_FF // FFN_CHUNK
    lead = 2
    for c in range(lead):
        ffn_chunk(c)

    def step_rows(j):
        start = SCAN_PITCH + j if j >= 0 else SCAN_STEPS + j
        sel = pl.ds(start, SCAN_SEGS, stride=SCAN_PITCH)
        return jnp.concatenate([xpad_ref[c, sel, :] for c in range(LRU_CHUNKS)], axis=-1)

    hist = CONV_W - 1
    xp = jnp.concatenate([step_rows(j) for j in range(-hist, SCAN_STEPS)], axis=0)
    n = SCAN_STEPS * SCAN_SEGS
    taps = [xp[(hist - j) * SCAN_SEGS:(hist - j) * SCAN_SEGS + n, :] for j in range(CONV_W)]
    xc = _conv_taps(cw_ref, cb_ref, taps)
    ra, ia = _lru_preact(xc, wa_ref, wi_ref)

    yr = _dot(hn_p, wxy_ref[:, LRU_W:2 * LRU_W])
    a_parts, u_parts = [], []
    per = n // SCAN_SEGS
    for k in range(SCAN_SEGS):
        gate = ffn_chunk(lead + k)
        tie = jnp.concatenate([_zero_after(gate[0:per, 0:LANES])] * LRU_CHUNKS, axis=-1)
        sl = slice(k * per, (k + 1) * per)
        a_k, u_k = _lru_decay_input(xc[sl] + tie, ra[sl], ia[sl], ba_ref, bi_ref, lam_ref)
        a_parts.append(a_k)
        u_parts.append(u_k)
    a = jnp.concatenate(a_parts, axis=0)
    u = jnp.concatenate(u_parts, axis=0)
    for c in range(lead + SCAN_SEGS, n_chunks):
        ffn_chunk(c)

    def rows_of(v, j):
        return v[j * SCAN_SEGS:(j + 1) * SCAN_SEGS, :]

    h = jnp.zeros((SCAN_SEGS, LRU_W), F32)
    prod = jnp.ones((SCAN_SEGS, LRU_W), F32)
    for j in range(SCAN_STEPS):
        a_j = rows_of(a, j)
        h = a_j * h + rows_of(u, j)
        prod = a_j * prod
    prod_c, h_c = _linear_scan(prod, h, SCAN_SEGS)
    h_end = prod_c * hc_ref[...] + h_c
    seg = lax.broadcasted_iota(jnp.int32, (SCAN_SEGS, 1), 0)
    h = jnp.where(seg == 0, hc_ref[...], pltpu.roll(h_end, 1, 0))
    for j in range(SCAN_STEPS):
        h = rows_of(a, j) * h + rows_of(u, j)
        dst = pl.ds(j, SCAN_SEGS, stride=SCAN_PITCH)
        for c in range(LRU_CHUNKS):
            hpad_ref[c, dst, :] = h[:, c * LANES:(c + 1) * LANES]
    hs = jnp.concatenate(
        [jnp.concatenate([hpad_ref[c, SCAN_PITCH * s:SCAN_PITCH * s + SCAN_STEPS, :] for c in range(LRU_CHUNKS)],
                         axis=-1) for s in range(SCAN_SEGS)], axis=0)
    yrec_ref[0] = (jax.nn.gelu(yr) * hs).astype(BF16)
    hc_ref[...] = h_end[SCAN_SEGS - 1:SCAN_SEGS, :]
    plru_ref[0] = h_end[SCAN_SEGS - 1:SCAN_SEGS, :]
    pconv_ref[0] = xr[rows - hist:rows, :]
    for c in range(LRU_CHUNKS):
        xpad_ref[c, tail, :] = xr[rows - SUBLANES:rows, c * LANES:(c + 1) * LANES]

    y = x + 0.5 * _dot(act_ref[...], wd_ref[...])
    o_ref[...] = y
    xbuf_ref[slot] = y


def _p_proj_body(x_ref, g_ref, w_ref, q0, q1, q2, k0, k1, k2, v0, v1, v2, pk0, pv0, pk1, pv1, pk2, pv2, zs_ref):
    rows = PROJ_TILE
    hn = _rms(x_ref[0], g_ref[...]).astype(BF16)
    outs = ((q0, q1, q2), (k0, k1, k2), (v0, v1, v2))
    wins = (None, (pk0, pk1, pk2), (pv0, pv1, pv2))
    for kind in range(3):
        for g, (window, dil) in enumerate(GROUPS):
            col = (kind * len(GROUPS) + g) * ATT_W
            z = _dot(hn, w_ref[:, col:col + ATT_W])
            if kind == 0:
                z = z * ATT_SCALE
            else:
                keep = min(window, rows)
                wins[kind][g][0] = jnp.transpose(z[rows - keep:, :])
            if dil == 1:
                outs[kind][g][0, 0] = z.astype(BF16)
            else:
                stage = zs_ref.at[kind * (len(GROUPS) - 1) + g - 1]
                for c in range(ATT_CHUNKS):
                    stage[c] = z[:, c * LANES:(c + 1) * LANES]
                for r in range(dil):
                    sel = pl.ds(r, rows // dil, stride=dil)
                    outs[kind][g][0, r] = jnp.concatenate(
                        [stage[c, sel, :] for c in range(ATT_CHUNKS)], axis=-1).astype(BF16)


def _softmax_parts(s):
    m = jnp.max(s, axis=-1, keepdims=True)
    p = jnp.exp(s - m)
    return p.astype(BF16), jnp.sum(p, axis=-1, keepdims=True), m


def _gate(hn, wc_ref, bg_ref, gate_col, idx):
    lo = idx * D_MODEL
    return jax.nn.sigmoid(_dot(hn, wc_ref[:, gate_col + lo:gate_col + lo + D_MODEL]) + bg_ref[:, lo:lo + D_MODEL])


def _p_mix_body(x_ref, q0, q1, q2, k0, k1, k2, v0, v1, v2, yr_ref, mkv_ref, b0, b1, b2,
                g_ref, wc_ref, bg_ref, wao_ref, wro_ref, wmo_ref, wout_ref,
                o_ref, kh0, kh1, kh2, vh0, vh1, vh2, o_scr, l_scr):
    tile = MIX_TILE
    t = pl.program_id(1)
    slot = lax.rem(t, 2)
    qs_, ks_, vs_ = (q0, q1, q2), (k0, k1, k2), (v0, v1, v2)
    khs, vhs, biases = (kh0, kh1, kh2), (vh0, vh1, vh2), (b0, b1, b2)

    @pl.when(t == 0)
    def _():
        for g, (window, dil) in enumerate(GROUPS):
            cur = tile // dil
            hist = 2 * N_REL - min(cur, N_REL)
            for r in range(dil):
                for h_ref in (khs[g], vhs[g]):
                    h_ref[0, r * (hist + cur):r * (hist + cur) + hist, :] = jnp.zeros((hist, ATT_W), BF16)

    res = {}

    windows, carried = [], []

    def window_block(g, dil, r, lo, qrows, keys, tab, new_rows):
        def scores():
            if new_rows is not None:
                khs[g][slot, new_rows, :] = ks_[g][0, r]
                vhs[g][slot, new_rows, :] = vs_[g][0, r]
            q = qs_[g][0, r, lo:lo + qrows, :]
            return _softmax_parts(_dot_nt(_stack_heads(q), khs[g][slot, keys, :]) + tab)

        def values(parts):
            p, l, m = parts
            o = _dot(p, vhs[g][slot, keys, :]) * (1.0 / l)
            lse = jnp.broadcast_to(m + jnp.log(l), o.shape)
            o_c, l_c = _unstack_heads(o, qrows), _unstack_heads(lse, qrows)
            dst = slice(lo, lo + qrows) if dil == 1 else pl.ds(lo * dil + r, qrows, stride=dil)
            for c in range(ATT_CHUNKS):
                o_scr[g * ATT_CHUNKS + c, dst, :] = o_c[c]
                l_scr[g * ATT_CHUNKS + c, dst, :] = l_c[c]
        return scores, values

    for g, (window, dil) in enumerate(GROUPS):
        cur = tile // dil
        qrows = min(cur, N_REL)
        hist = 2 * N_REL - qrows
        for r in range(dil):
            base = r * (hist + cur)
            carried.append((g, base, hist, cur))
            for j in range(cur // qrows):
                if cur < N_REL:
                    tab = biases[g][jnp.minimum(t, biases[g].shape[0] - 1)]
                elif j == 0:
                    tab = biases[g][jnp.where(t == 0, 1, 0)]
                else:
                    tab = biases[g][0]
                lo = j * qrows
                windows.append(window_block(g, dil, r, lo, qrows, slice(base + lo, base + lo + 2 * N_REL), tab,
                                            slice(base + hist, base + hist + cur) if j == 0 else None))

    def memory_block(h):
        sl = slice(h * MEM_HEAD_DIM, (h + 1) * MEM_HEAD_DIM)

        def scores():
            return _softmax_parts(_dot_nt(res["qm"][:, sl].astype(BF16), mkv_ref[0, :, sl]) * MEM_SCALE)

        def values(parts):
            p, l, _ = parts
            res["mem", h] = _dot(p, mkv_ref[0, :, MEM_W + h * MEM_HEAD_DIM:MEM_W + (h + 1) * MEM_HEAD_DIM]) * (1.0 / l)
        return scores, values

    def normed_input():
        res["x"] = x_ref[0]
        res["hn"] = _rms(res["x"], g_ref[...]).astype(BF16)
        res["qm"] = _dot(res["hn"], wc_ref[:, 0:MEM_W])

    def rec_term():
        res["rec"] = _gate(res["hn"], wc_ref, bg_ref, MEM_W, 1) * _dot(yr_ref[0], wro_ref[...])

    def att_gate():
        res["att_gate"] = _gate(res["hn"], wc_ref, bg_ref, MEM_W, 0)

    def mem_gate():
        res["mem_gate"] = _gate(res["hn"], wc_ref, bg_ref, MEM_W, 2)

    normed_input()
    blocks = windows + [memory_block(h) for h in range(MEM_HEADS)]
    after = {6: rec_term, 13: att_gate, 20: mem_gate}

    pending = []
    for i, (scores, values) in enumerate(blocks):
        pending.append((values, scores()))
        if len(pending) > ATTN_DEPTH:
            done, parts = pending.pop(0)
            done(parts)
        if i in after:
            after[i]()
    for done, parts in pending:
        done(parts)

    mixed = []
    for c in range(ATT_CHUNKS):
        l0, l1, l2 = (l_scr[g * ATT_CHUNKS + c] for g in range(len(GROUPS)))
        o0, o1, o2 = (o_scr[g * ATT_CHUNKS + c] for g in range(len(GROUPS)))
        m = jnp.maximum(jnp.maximum(l0, l1), l2)
        w0, w1, w2 = jnp.exp(l0 - m), jnp.exp(l1 - m), jnp.exp(l2 - m)
        mixed.append((w0 * o0 + w1 * o1 + w2 * o2) / (w0 + w1 + w2))
    ya = jnp.concatenate(mixed, axis=-1).astype(BF16)
    ym = jnp.concatenate([res["mem", h] for h in range(MEM_HEADS)], axis=-1).astype(BF16)
    merged = (res["att_gate"] * _dot(ya, wao_ref[...]) + res["rec"]
              + res["mem_gate"] * _dot(ym, wmo_ref[...]))
    o_ref[0] = res["x"] + _dot(merged.astype(BF16), wout_ref[...])
    for g, base, hist, cur in carried:
        for h_ref in (khs[g], vhs[g]):
            h_ref[1 - slot, base:base + hist, :] = h_ref[slot, base + cur:base + cur + hist, :]


def _merge_out(x, hn, branches, wc_ref, gate_col, bg_ref, wout_ref):
    merged = None
    for idx, (y, w_ref) in enumerate(branches):
        term = _gate(hn, wc_ref, bg_ref, gate_col, idx) * _dot(y, w_ref[...])
        merged = term if merged is None else merged + term
    return x + _dot(merged.astype(BF16), wout_ref[...])


def _merge_body(x_ref, ya_ref, yr_ref, ym_ref, g_ref, wc_ref, bg_ref, wao_ref, wro_ref, wmo_ref, wout_ref, o_ref):
    x = x_ref[0]
    hn = _rms(x, g_ref[...]).astype(BF16)
    o_ref[0] = _merge_out(x, hn, ((ya_ref[0], wao_ref), (yr_ref[0], wro_ref), (ym_ref[0], wmo_ref)),
                          wc_ref, 0, bg_ref, wout_ref)


def _s_mix_body(z_ref, ck0, cv0, ck1, cv1, ck2, cv2, sconv_ref, slru_ref, cmk_ref, cmv_ref,
                bc0, bc1, bc2, bn_ref, cw_ref, cb_ref, wa_ref, wi_ref, ba_ref, bi_ref, lam_ref,
                ya_ref, yrec_ref, ym_ref, ok0, ov0, ok1, ov1, ok2, ov2, oconv_ref, olru_ref,
                ext_ref, *, steps):
    cks, cvs = (ck0, ck1, ck2), (cv0, cv1, cv2)
    oks, ovs = (ok0, ok1, ok2), (ov0, ov1, ov2)
    bcs = (bc0, bc1, bc2)
    zeros = jnp.zeros((N_REL - steps, ATT_W), F32)
    new_lanes = lax.broadcasted_iota(jnp.int32, (ATT_W, LANES), 1) >= LANES - steps
    outs, lses = [], []
    for g, (window, dil) in enumerate(GROUPS):
        q = z_ref[0, :, g * ATT_W:(g + 1) * ATT_W] * ATT_SCALE
        kn = z_ref[0, :, (3 + g) * ATT_W:(4 + g) * ATT_W]
        vn = z_ref[0, :, (6 + g) * ATT_W:(7 + g) * ATT_W]
        qs = _stack_heads(q).astype(BF16)
        kt = cks[g][0]
        vt = cvs[g][0]
        s_c = _dot(qs, kt.astype(BF16)) + bcs[g][...]
        s_n = _dot_nt(qs, jnp.concatenate([kn, zeros], axis=0).astype(BF16)) + bn_ref[g]
        m = jnp.maximum(jnp.max(s_c, axis=-1, keepdims=True), jnp.max(s_n, axis=-1, keepdims=True))
        p_c = jnp.exp(s_c - m)
        p_n = jnp.exp(s_n - m)
        l = jnp.sum(p_c, axis=-1, keepdims=True) + jnp.sum(p_n, axis=-1, keepdims=True)
        o = (_dot_nt(p_c.astype(BF16), vt.astype(BF16))
             + _dot(p_n.astype(BF16), jnp.concatenate([vn, zeros], axis=0).astype(BF16))) * (1.0 / l)
        lse = jnp.broadcast_to(m + jnp.log(l), o.shape)
        outs.append(jnp.concatenate(_unstack_heads(o, steps), axis=-1))
        lses.append(jnp.concatenate(_unstack_heads(lse, steps), axis=-1))
        for old, new, dst in ((kt, kn, oks[g]), (vt, vn, ovs[g])):
            rolled = pltpu.roll(old, window - steps, 1)
            new_t = jnp.transpose(jnp.concatenate([zeros, new], axis=0))
            if window > LANES:
                dst[0, :, 0:window - LANES] = rolled[:, 0:window - LANES]
            dst[0, :, window - LANES:window] = jnp.where(new_lanes, new_t, rolled[:, window - LANES:window])
    m = jnp.maximum(jnp.maximum(lses[0], lses[1]), lses[2])
    ws = [jnp.exp(l - m) for l in lses]
    ya_ref[0] = ((ws[0] * outs[0] + ws[1] * outs[1] + ws[2] * outs[2]) / (ws[0] + ws[1] + ws[2])).astype(BF16)

    hist = CONV_W - 1
    ext_ref[0:SUBLANES, :] = jnp.zeros((SUBLANES, LRU_W), F32)
    ext_ref[SUBLANES - hist:SUBLANES, :] = sconv_ref[0]
    ext_ref[SUBLANES:SUBLANES + steps, :] = z_ref[0, :, COL_XR:COL_XR + LRU_W]
    taps = [ext_ref[SUBLANES - j:SUBLANES - j + steps, :] for j in range(CONV_W)]
    a, u = _lru_gates(_conv_taps(cw_ref, cb_ref, taps), wa_ref, wi_ref, ba_ref, bi_ref, lam_ref)
    a_c, u_c = _linear_scan(a, u, steps)
    h = a_c * slru_ref[0] + u_c
    yrec_ref[0] = (jax.nn.gelu(z_ref[0, :, COL_YR:COL_YR + LRU_W]) * h).astype(BF16)
    olru_ref[0] = h[steps - 1:steps, :]
    oconv_ref[0] = ext_ref[SUBLANES + steps - hist:SUBLANES + steps, :]

    qm = z_ref[0, :, COL_QM:COL_QM + MEM_W]
    mk = [cmk_ref[0, pl.ds(h, N_MEM, stride=MEM_HEADS), :].astype(BF16) for h in range(MEM_HEADS)]
    mv = [cmv_ref[0, pl.ds(h, N_MEM, stride=MEM_HEADS), :].astype(BF16) for h in range(MEM_HEADS)]
    ym_ref[0] = _mem_attention(qm, mk, mv).astype(BF16)


def _resident(shape):
    zeros = (0,) * len(shape)
    return pl.BlockSpec(shape, lambda *_: zeros, pipeline_mode=pl.Buffered(1))


def _params(*semantics):
    return pltpu.CompilerParams(dimension_semantics=semantics, vmem_limit_bytes=V7X_VMEM_LIMIT)


def _ffn(x, norm_g, w_gu, w_down, final_g=None):
    n = x.shape[0]
    tile = min(FFN_TILE, n)
    row = pl.BlockSpec((tile, D_MODEL), lambda i: (i, 0))
    in_specs = [row, _resident((1, D_MODEL)), _resident(w_gu.shape), _resident(w_down.shape)]
    args = [x, norm_g, w_gu, w_down]
    if final_g is not None:
        in_specs.append(_resident((1, D_MODEL)))
        args.append(final_g)
    return pl.pallas_call(
        functools.partial(_ffn_body, final_norm=final_g is not None),
        grid=(n // tile,),
        in_specs=in_specs,
        out_specs=row,
        out_shape=jax.ShapeDtypeStruct((n, D_MODEL), F32),
        scratch_shapes=[pltpu.VMEM((tile, D_FF), BF16)],
        compiler_params=_params("parallel"),
        name="ffn_final" if final_g is not None else "ffn",
    )(*args)


def _norm_mm(x, norm_g, w, row_tile, col_tile):
    n, cols = x.shape[0], w.shape[1]
    return pl.pallas_call(
        _norm_mm_body,
        grid=(n // row_tile, cols // col_tile),
        in_specs=[pl.BlockSpec((row_tile, D_MODEL), lambda i, j: (i, 0)),
                  _resident((1, D_MODEL)),
                  pl.BlockSpec((D_MODEL, col_tile), lambda i, j: (0, j))],
        out_specs=pl.BlockSpec((row_tile, col_tile), lambda i, j: (i, j)),
        out_shape=jax.ShapeDtypeStruct((n, cols), F32),
        compiler_params=_params("parallel", "parallel"),
        name="norm_mm",
    )(x, norm_g, w)


def _mem_kv(mem, norm_g, w_kv, row_tile):
    n = mem.shape[0]
    state = jax.ShapeDtypeStruct((n * MEM_HEADS, MEM_HEAD_DIM), F32)
    state_spec = pl.BlockSpec((row_tile * MEM_HEADS, MEM_HEAD_DIM), lambda i: (i, 0))
    return pl.pallas_call(
        _mem_kv_body,
        grid=(n // row_tile,),
        in_specs=[pl.BlockSpec((row_tile, D_MODEL), lambda i: (i, 0)),
                  _resident((1, D_MODEL)), _resident(w_kv.shape)],
        out_specs=[state_spec, state_spec, pl.BlockSpec((row_tile, 2 * MEM_W), lambda i: (i, 0))],
        out_shape=[state, state, jax.ShapeDtypeStruct((n, 2 * MEM_W), BF16)],
        compiler_params=_params("parallel"),
        name="mem_kv",
    )(mem, norm_g, w_kv)


def _ffn_rec(x, norm_g, w_gu, w_down, mix_g, w_xy, lru, batch, seq):
    tile = FFN_TILE
    nt = seq // tile
    n = batch * nt
    last = n - 1

    def prev(i):
        return jnp.maximum(i - 1, 0)

    weights = [norm_g, w_gu, w_down, mix_g, w_xy] + list(lru)
    return pl.pallas_call(
        functools.partial(_ffn_rec_body, tiles_per_seq=nt),
        grid=(n + 1,),
        in_specs=[pl.BlockSpec((tile, D_MODEL), lambda i: (jnp.minimum(i, last), 0))]
        + [_resident(w.shape) for w in weights],
        out_specs=[pl.BlockSpec((tile, D_MODEL), lambda i: (jnp.minimum(i, last), 0)),
                   pl.BlockSpec((1, tile, LRU_W), lambda i: (prev(i) // nt, prev(i) % nt, 0)),
                   pl.BlockSpec((1, CONV_W - 1, LRU_W), lambda i: (prev(i) // nt, 0, 0)),
                   pl.BlockSpec((1, 1, LRU_W), lambda i: (prev(i) // nt, 0, 0))],
        out_shape=[jax.ShapeDtypeStruct((n * tile, D_MODEL), F32),
                   jax.ShapeDtypeStruct((batch, seq, LRU_W), BF16),
                   jax.ShapeDtypeStruct((batch, CONV_W - 1, LRU_W), F32),
                   jax.ShapeDtypeStruct((batch, 1, LRU_W), F32)],
        scratch_shapes=[pltpu.VMEM((tile, D_FF), BF16),
                        pltpu.VMEM((2, tile, D_MODEL), F32),
                        pltpu.VMEM((LRU_CHUNKS, SCAN_PITCH * (SCAN_SEGS + 1), LANES), F32),
                        pltpu.VMEM((LRU_CHUNKS, SCAN_PITCH * SCAN_SEGS, LANES), F32),
                        pltpu.VMEM((1, LRU_W), F32)],
        compiler_params=_params("arbitrary"),
        name="ffn_rec",
    )(x, *weights)


def _p_proj(x1, norm_g, w_qkv):
    batch, seq, _ = x1.shape
    tile = PROJ_TILE
    nt = seq // tile
    qkv_shapes, qkv_specs = [], []
    for _kind in range(3):
        for window, dil in GROUPS:
            qkv_shapes.append(jax.ShapeDtypeStruct((batch, dil, seq // dil, ATT_W), BF16))
            qkv_specs.append(pl.BlockSpec((1, dil, tile // dil, ATT_W), lambda b, t: (b, 0, t, 0)))
    win_shapes, win_specs = [], []
    for window, dil in GROUPS:
        keep = min(window, seq)
        blk = min(keep, tile)
        first = nt - max(keep // tile, 1)
        for _kv in range(2):
            win_shapes.append(jax.ShapeDtypeStruct((batch, ATT_W, keep), F32))
            win_specs.append(pl.BlockSpec((1, ATT_W, blk),
                                          lambda b, t, first=first: (b, 0, jnp.maximum(t - first, 0))))
    return pl.pallas_call(
        _p_proj_body,
        grid=(batch, nt),
        in_specs=[pl.BlockSpec((1, tile, D_MODEL), lambda b, t: (b, t, 0)),
                  _resident((1, D_MODEL)), _resident(w_qkv.shape)],
        out_specs=qkv_specs + win_specs,
        out_shape=qkv_shapes + win_shapes,
        scratch_shapes=[pltpu.VMEM((3 * (len(GROUPS) - 1), ATT_CHUNKS, tile, LANES), F32)],
        compiler_params=_params("parallel", "arbitrary"),
        name="p_proj",
    )(x1, norm_g, w_qkv)


def _p_mix(x1, qkv, y_rec, mkv, biases, norm_g, w_c, b_gate, w_att_o, w_rec_o, w_mem_o, w_out):
    batch, seq, _ = x1.shape
    tile = MIX_TILE

    def tok(width):
        return pl.BlockSpec((1, tile, width), lambda b, t: (b, t, 0))

    qkv_specs, hist = [], []
    for window, dil in GROUPS:
        cur = tile // dil
        qkv_specs.append(pl.BlockSpec((1, dil, cur, ATT_W), lambda b, t: (b, 0, t, 0)))
        hist.append(pltpu.VMEM((2, dil * (2 * N_REL - min(cur, N_REL) + cur), ATT_W), BF16))
    weights = [norm_g, w_c, b_gate, w_att_o, w_rec_o, w_mem_o, w_out]
    in_specs = ([tok(D_MODEL)] + qkv_specs * 3
                + [tok(LRU_W), pl.BlockSpec((1, N_MEM, 2 * MEM_W), lambda b, t: (b, 0, 0))]
                + [_resident(a.shape) for a in list(biases) + weights])
    return pl.pallas_call(
        _p_mix_body,
        grid=(batch, seq // tile),
        in_specs=in_specs,
        out_specs=tok(D_MODEL),
        out_shape=jax.ShapeDtypeStruct((batch, seq, D_MODEL), F32),
        scratch_shapes=hist * 2 + [pltpu.VMEM((len(GROUPS) * ATT_CHUNKS, tile, LANES), F32)] * 2,
        compiler_params=_params("parallel", "arbitrary"),
        name="p_mix",
    )(x1, *qkv, y_rec, mkv, *biases, *weights)


def _merge(x1, y_att, y_rec, y_mem, norm_g, w_c, b_gate, w_att_o, w_rec_o, w_mem_o, w_out):
    batch, seq, _ = x1.shape
    tile = seq

    def tok(width):
        return pl.BlockSpec((1, tile, width), lambda b, t: (b, t, 0))

    weights = [norm_g, w_c, b_gate, w_att_o, w_rec_o, w_mem_o, w_out]
    return pl.pallas_call(
        _merge_body,
        grid=(batch, seq // tile),
        in_specs=[tok(D_MODEL), tok(ATT_W), tok(LRU_W), tok(MEM_W)] + [_resident(w.shape) for w in weights],
        out_specs=tok(D_MODEL),
        out_shape=jax.ShapeDtypeStruct((batch, seq, D_MODEL), F32),
        compiler_params=_params("parallel", "parallel"),
        name="merge_s",
    )(x1, y_att, y_rec, y_mem, *weights)


def _s_mix(z, caches, state_conv, state_lru, cmk, cmv, bias_c, bias_n, lru):
    batch, steps, _ = z.shape

    def per_batch(shape):
        return pl.BlockSpec((1,) + tuple(shape[1:]), lambda b: (b,) + (0,) * (len(shape) - 1))

    ins = [z] + list(caches) + [state_conv, state_lru, cmk, cmv]
    in_specs = [per_batch(a.shape) for a in ins]
    consts = list(bias_c) + [bias_n] + list(lru)
    in_specs += [_resident(a.shape) for a in consts]
    out_shapes = [jax.ShapeDtypeStruct((batch, steps, ATT_W), BF16),
                  jax.ShapeDtypeStruct((batch, steps, LRU_W), BF16),
                  jax.ShapeDtypeStruct((batch, steps, MEM_W), BF16)]
    out_shapes += [jax.ShapeDtypeStruct(c.shape, F32) for c in caches]
    out_shapes += [jax.ShapeDtypeStruct(state_conv.shape, F32), jax.ShapeDtypeStruct(state_lru.shape, F32)]
    return pl.pallas_call(
        functools.partial(_s_mix_body, steps=steps),
        grid=(batch,),
        in_specs=in_specs,
        out_specs=[per_batch(s.shape) for s in out_shapes],
        out_shape=out_shapes,
        scratch_shapes=[pltpu.VMEM((2 * SUBLANES, LRU_W), F32)],
        compiler_params=_params("parallel"),
        name="s_mix",
    )(*ins, *consts)


def _bucket(n):
    n = np.maximum(n, 0)
    max_exact = NUM_BUCKETS // 2
    nf = np.maximum(n, 1).astype(np.float32)
    large = max_exact + (np.log(nf / max_exact) / np.float32(math.log(MAX_DISTANCE / max_exact))
                         * (NUM_BUCKETS - max_exact)).astype(np.int32)
    return np.where(n < max_exact, n, np.minimum(large, NUM_BUCKETS - 1))


def _dist_bias(rel_bias, g, dist, valid):
    tab = rel_bias[:, g * HEADS:(g + 1) * HEADS].T[:, _bucket(dist)]
    return jnp.where(valid[None], tab, -jnp.inf).astype(F32)


def _toeplitz(u, rows, cols):
    lu = rows + cols - 1
    ue = jnp.concatenate([u, u[:, :1]], axis=1)
    m = jnp.tile(ue, (1, rows))[:, :rows * lu].reshape(-1, rows, lu)
    return m[:, :, rows - 1:rows - 1 + cols].reshape(-1, cols)


def _prompt_bias(rel_bias, tile):
    col = np.arange(2 * N_REL)
    stacks = []
    for g, (_, dil) in enumerate(GROUPS):
        cur = tile // dil
        qrows = min(cur, N_REL)
        hist = 2 * N_REL - qrows
        delta = np.arange(-(qrows - 1), 2 * N_REL)
        v = _dist_bias(rel_bias, g, dil * delta, (delta >= 0) & (delta <= N_REL))
        base = _toeplitz(v[:, ::-1], qrows, 2 * N_REL)
        present = [hist, 0] if cur >= N_REL else list(range(0, hist + 1, cur))
        stacks.append(jnp.stack([jnp.where(col >= hist - n, base, -jnp.inf) for n in present]))
    return stacks


def _sample_bias(rel_bias, steps):
    t = np.arange(steps)[:, None]
    cache, new = [], []
    for g, (window, dil) in enumerate(GROUPS):
        dist = np.arange(window + steps)
        tab = _dist_bias(rel_bias, g, dist, (dist % dil == 0) & (dist <= window))
        cache.append(_toeplitz(tab[:, 1:][:, ::-1], steps, window))
        j = np.arange(N_REL)[None, :]
        dist = t - j
        valid = (j < steps) & (dist >= 0) & (dist % dil == 0)
        new.append(_dist_bias(rel_bias, g, dist, valid).reshape(HEADS * steps, N_REL))
    return cache, jnp.stack(new)


def _block_diag(w):
    per = MXU_TILE // LRU_BLOCK
    w4 = w.reshape(-1, per, LRU_BLOCK, LRU_BLOCK)
    bd = jnp.einsum('jnef,nm->jnemf', w4, jnp.eye(per, dtype=w.dtype))
    return bd.reshape(-1, MXU_TILE, MXU_TILE).astype(BF16)


def _to_pos_minor(c):
    return jnp.transpose(c[0], (0, 2, 3, 1)).reshape(c.shape[1], ATT_W, c.shape[2])


def _from_pos_minor(c):
    return jnp.transpose(c.reshape(c.shape[0], HEADS, HEAD_DIM, c.shape[2]), (0, 3, 1, 2))[None]


def kernel(x_prompt, x_sample, mem_prompt, cache_win_k0, cache_win_v0, cache_win_k1, cache_win_v1, cache_win_k2, cache_win_v2, state_conv, state_lru, cache_mem_k, cache_mem_v, rel_bias, ffn1_norm, ffn1_w_gu, ffn1_w_down, mix_norm, w_in, b_gate, conv_w, conv_b, lru_w_a, lru_b_a, lru_w_i, lru_b_i, lru_lambda, mem_norm, w_mem_kv, w_att_o, w_rec_o, w_mem_o, w_out, ffn2_norm, ffn2_w_gu, ffn2_w_down, final_norm):
    depth = ffn1_norm.shape[0]
    assert depth == 1
    l = 0
    batch, seq, _ = x_prompt.shape
    sbatch, steps, _ = x_sample.shape
    row = lambda v: v.reshape(1, -1)

    lru = (conv_w[l], row(conv_b[l]), _block_diag(lru_w_a[l]), _block_diag(lru_w_i[l]),
           row(lru_b_a[l]), row(lru_b_i[l]), row(lru_lambda[l]))
    ffn1 = (row(ffn1_norm[l]), ffn1_w_gu[l].astype(BF16), ffn1_w_down[l].astype(BF16))
    ffn2 = (row(ffn2_norm[l]), ffn2_w_gu[l].astype(BF16), ffn2_w_down[l].astype(BF16))
    outw = (row(b_gate[l]), w_att_o[l].astype(BF16), w_rec_o[l].astype(BF16), w_mem_o[l].astype(BF16),
            w_out[l].astype(BF16))
    mixg = row(mix_norm[l])
    fin = row(final_norm)

    xp1, y_rec, p_conv, p_lru = _ffn_rec(x_prompt.reshape(batch * seq, D_MODEL), *ffn1, mixg,
                                         w_in[l][:, COL_XR:COL_QM].astype(BF16), lru, batch, seq)
    xp1 = xp1.reshape(batch, seq, D_MODEL)
    p_mk, p_mv, mkv = _mem_kv(mem_prompt.reshape(batch * N_MEM, D_MODEL), row(mem_norm[l]),
                              w_mem_kv[l].astype(BF16), row_tile=2 * N_MEM)
    proj = _p_proj(xp1, mixg, w_in[l][:, :QKV_W].astype(BF16))
    qkv, wins = proj[0:9], proj[9:15]
    xp2 = _p_mix(xp1, qkv, y_rec, mkv.reshape(batch, N_MEM, 2 * MEM_W), _prompt_bias(rel_bias, MIX_TILE), mixg,
                 w_in[l][:, COL_QM:].astype(BF16), *outw)
    y_prompt = _ffn(xp2.reshape(batch * seq, D_MODEL), *ffn2, final_g=fin).reshape(batch, seq, D_MODEL)

    n_s = sbatch * steps
    xs1 = _ffn(x_sample.reshape(n_s, D_MODEL), *ffn1)
    z = _norm_mm(xs1, mixg, w_in[l].astype(BF16), row_tile=n_s, col_tile=IN_COLS // 2)
    caches = [_to_pos_minor(c) for c in
              (cache_win_k0, cache_win_v0, cache_win_k1, cache_win_v1, cache_win_k2, cache_win_v2)]
    bias_c, bias_n = _sample_bias(rel_bias, steps)
    mem_rows = lambda c: c[l].reshape(sbatch, N_MEM * MEM_HEADS, MEM_HEAD_DIM)
    mix = _s_mix(z.reshape(sbatch, steps, IN_COLS), caches, state_conv[l], state_lru[l].reshape(sbatch, 1, LRU_W),
                 mem_rows(cache_mem_k), mem_rows(cache_mem_v), bias_c, bias_n, lru)
    s_att, s_rec, s_mem = (a.reshape(1, n_s, a.shape[-1]) for a in mix[0:3])
    xs2 = _merge(xs1.reshape(1, n_s, D_MODEL), s_att, s_rec, s_mem, mixg, w_in[l][:, COL_GATE:].astype(BF16),
                 *outw)
    y_sample = _ffn(xs2.reshape(n_s, D_MODEL), *ffn2, final_g=fin).reshape(sbatch, steps, D_MODEL)

    mem_state = lambda a: a.reshape(1, batch, N_MEM, MEM_HEADS, MEM_HEAD_DIM)
    return (y_prompt, y_sample, *[_from_pos_minor(w) for w in wins],
            p_conv[None], p_lru.reshape(1, batch, LRU_W), mem_state(p_mk), mem_state(p_mv),
            *[_from_pos_minor(w) for w in mix[3:9]], mix[9][None], mix[10].reshape(1, sbatch, LRU_W))
```

```python
import functools
import math

import numpy as np
import jax
import jax.numpy as jnp
from jax import lax
from jax.experimental import pallas as pl
from jax.experimental.pallas import tpu as pltpu

F32 = jnp.float32
BF16 = jnp.bfloat16

D_MODEL = 1024
D_FF = 2816
HEAD_DIM = 64
HEADS = 4
GROUPS = ((128, 1), (512, 4), (2048, 16))
N_REL = 128
ATT_W = HEADS * HEAD_DIM
ATT_SCALE = HEAD_DIM ** -0.5
NUM_BUCKETS = 32
MAX_DISTANCE = 2048
LRU_W = 768
LRU_BLOCK = 64
LRU_C = 8.0
CONV_W = 4
N_MEM = 256
MEM_HEADS = 4
MEM_HEAD_DIM = 128
MEM_W = MEM_HEADS * MEM_HEAD_DIM
MEM_SCALE = MEM_HEAD_DIM ** -0.5
RMS_EPS = 1e-6
QKV_W = 3 * len(GROUPS) * ATT_W
COL_XR = QKV_W
COL_YR = COL_XR + LRU_W
COL_QM = COL_YR + LRU_W
COL_GATE = COL_QM + MEM_W
IN_COLS = COL_GATE + 3 * D_MODEL

V7X_VMEM_LIMIT = 62 * 1024 * 1024
MXU_TILE = 256
SUBLANES = 8
LANES = 128
FFN_TILE = 512
FFN_CHUNK = 256
PROJ_TILE = 512
SCAN_SEGS = SUBLANES
SCAN_STEPS = FFN_TILE // SCAN_SEGS
SCAN_PITCH = SCAN_STEPS + SUBLANES
MIX_TILE = 512
ATTN_DEPTH = 6
LRU_CHUNKS = LRU_W // LANES
ATT_CHUNKS = ATT_W // LANES


def _dot(a, b):
    return jnp.dot(a, b, preferred_element_type=F32)


def _dot_nt(a, b):
    return lax.dot_general(a, b, (((1,), (1,)), ((), ())), preferred_element_type=F32)


def _rms(x, g):
    return x * lax.rsqrt(jnp.mean(x * x, axis=-1, keepdims=True) + RMS_EPS) * g


def _stack_heads(q):
    lane = lax.broadcasted_iota(jnp.int32, q.shape, 1)
    return jnp.concatenate(
        [jnp.where((lane >= h * HEAD_DIM) & (lane < (h + 1) * HEAD_DIM), q, 0.0) for h in range(HEADS)], axis=0)


def _unstack_heads(x, rows):
    low = lax.broadcasted_iota(jnp.int32, (rows, LANES), 1) < HEAD_DIM
    per = LANES // HEAD_DIM
    return [jnp.where(low,
                      x[(per * c) * rows:(per * c + 1) * rows, c * LANES:(c + 1) * LANES],
                      x[(per * c + 1) * rows:(per * c + 2) * rows, c * LANES:(c + 1) * LANES])
            for c in range(ATT_CHUNKS)]


def _linear_scan(a, u, rows):
    row = lax.broadcasted_iota(jnp.int32, (rows, 1), 0)
    d = 1
    while d < rows:
        a_s = pltpu.roll(a, d, 0)
        u_s = pltpu.roll(u, d, 0)
        keep = row >= d
        u = jnp.where(keep, a * u_s + u, u)
        a = jnp.where(keep, a * a_s, a)
        d *= 2
    return a, u


def _zero_after(x):
    bits = pltpu.bitcast(x, jnp.uint32)
    half = jnp.uint32(17)
    return pltpu.bitcast(lax.shift_right_logical(lax.shift_right_logical(bits, half), half), F32)


def _lru_preact(xc, wa_ref, wi_ref):
    xcb = xc.astype(BF16)
    ra, ia = [], []
    for j in range(LRU_W // MXU_TILE):
        blk = xcb[:, j * MXU_TILE:(j + 1) * MXU_TILE]
        ra.append(_dot(blk, wa_ref[j]))
        ia.append(_dot(blk, wi_ref[j]))
    return jnp.concatenate(ra, axis=-1), jnp.concatenate(ia, axis=-1)


def _lru_decay_input(xc, ra, ia, ba_ref, bi_ref, lam_ref):
    r = jax.nn.sigmoid(ra + ba_ref[...])
    i = jax.nn.sigmoid(ia + bi_ref[...])
    nl = -lam_ref[...]
    softplus = jnp.maximum(nl, 0.0) + jnp.log1p(jnp.exp(-jnp.abs(nl)))
    log_a = (-LRU_C * r) * softplus
    a = jnp.exp(log_a)
    u = jnp.sqrt(1.0 - a * a) * (i * xc)
    return a, u


def _lru_gates(xc, wa_ref, wi_ref, ba_ref, bi_ref, lam_ref):
    return _lru_decay_input(xc, *_lru_preact(xc, wa_ref, wi_ref), ba_ref, bi_ref, lam_ref)


def _conv_taps(cw_ref, cb_ref, taps):
    xc = cb_ref[...] + cw_ref[CONV_W - 1:CONV_W, :] * taps[0]
    for j in range(1, CONV_W):
        xc = xc + cw_ref[CONV_W - 1 - j:CONV_W - j, :] * taps[j]
    return xc


def _mem_attention(qm, mk_heads, mv_heads):
    heads = []
    for h in range(MEM_HEADS):
        q = qm[:, h * MEM_HEAD_DIM:(h + 1) * MEM_HEAD_DIM].astype(BF16)
        s = _dot_nt(q, mk_heads[h]) * MEM_SCALE
        m = jnp.max(s, axis=-1, keepdims=True)
        p = jnp.exp(s - m)
        l = jnp.sum(p, axis=-1, keepdims=True)
        heads.append(_dot(p.astype(BF16), mv_heads[h]) * (1.0 / l))
    return jnp.concatenate(heads, axis=-1)


def _ffn_body(x_ref, g_ref, wgu_ref, wd_ref, *rest, final_norm):
    if final_norm:
        fg_ref, o_ref, act_ref = rest
    else:
        o_ref, act_ref = rest
    x = x_ref[...]
    hn = _rms(x, g_ref[...]).astype(BF16)
    for c in range(D_FF // FFN_CHUNK):
        lo = c * FFN_CHUNK
        gate = _dot(hn, wgu_ref[:, lo:lo + FFN_CHUNK])
        up = _dot(hn, wgu_ref[:, D_FF + lo:D_FF + lo + FFN_CHUNK])
        act_ref[:, lo:lo + FFN_CHUNK] = (gate * jax.nn.sigmoid(gate) * up).astype(BF16)
    y = x + 0.5 * _dot(act_ref[...], wd_ref[...])
    if final_norm:
        y = _rms(y, fg_ref[...])
    o_ref[...] = y


def _norm_mm_body(x_ref, g_ref, w_ref, o_ref):
    hn = _rms(x_ref[...], g_ref[...]).astype(BF16)
    o_ref[...] = _dot(hn, w_ref[...])


def _mem_kv_body(x_ref, g_ref, w_ref, mk_ref, mv_ref, mkv_ref):
    rows = x_ref.shape[0]
    z = _dot(_rms(x_ref[...], g_ref[...]).astype(BF16), w_ref[...])
    mkv_ref[...] = z.astype(BF16)
    for h in range(MEM_HEADS):
        dst = pl.ds(h, rows, stride=MEM_HEADS)
        mk_ref[dst, :] = z[:, h * MEM_HEAD_DIM:(h + 1) * MEM_HEAD_DIM]
        mv_ref[dst, :] = z[:, MEM_W + h * MEM_HEAD_DIM:MEM_W + (h + 1) * MEM_HEAD_DIM]


def _ffn_rec_body(x_ref, g_ref, wgu_ref, wd_ref, mg_ref, wxy_ref, cw_ref, cb_ref, wa_ref, wi_ref, ba_ref, bi_ref,
                  lam_ref, o_ref, yrec_ref, pconv_ref, plru_ref, act_ref, xbuf_ref, xpad_ref, hpad_ref, hc_ref,
                  *, tiles_per_seq):
    rows = FFN_TILE
    i = pl.program_id(0)
    t = lax.rem(jnp.maximum(i - 1, 0), tiles_per_seq)
    slot = lax.rem(i, 2)
    tail = slice(SCAN_STEPS - SUBLANES, SCAN_STEPS)

    @pl.when(i == 0)
    def _():
        xbuf_ref[...] = jnp.zeros(xbuf_ref.shape, F32)

    @pl.when(t == 0)
    def _():
        xpad_ref[:, tail, :] = jnp.zeros((LRU_CHUNKS, SUBLANES, LANES), F32)
        hc_ref[...] = jnp.zeros((1, LRU_W), F32)

    hn_p = _rms(xbuf_ref[1 - slot], mg_ref[...]).astype(BF16)
    xr = _dot(hn_p, wxy_ref[:, 0:LRU_W])
    for c in range(LRU_CHUNKS):
        for s in range(SCAN_SEGS):
            lo = SCAN_PITCH * (s + 1)
            xpad_ref[c, lo:lo + SCAN_STEPS, :] = xr[s * SCAN_STEPS:(s + 1) * SCAN_STEPS, c * LANES:(c + 1) * LANES]

    x = x_ref[...]
    hn = _rms(x, g_ref[...]).astype(BF16)

    def ffn_chunk(c):
        lo = c * FFN_CHUNK
        gate = _dot(hn, wgu_ref[:, lo:lo + FFN_CHUNK])
        up = _dot(hn, wgu_ref[:, D_FF + lo:D_FF + lo + FFN_CHUNK])
        act_ref[:, lo:lo + FFN_CHUNK] = (gate * jax.nn.sigmoid(gate) * up).astype(BF16)
        return gate

    n_chunks = D_FF // FFN_CHUNK
    lead = 2
    for c in range(lead):
        ffn_chunk(c)

    def step_rows(j):
        start = SCAN_PITCH + j if j >= 0 else SCAN_STEPS + j
        sel = pl.ds(start, SCAN_SEGS, stride=SCAN_PITCH)
        return jnp.concatenate([xpad_ref[c, sel, :] for c in range(LRU_CHUNKS)], axis=-1)

    hist = CONV_W - 1
    xp = jnp.concatenate([step_rows(j) for j in range(-hist, SCAN_STEPS)], axis=0)
    n = SCAN_STEPS * SCAN_SEGS
    taps = [xp[(hist - j) * SCAN_SEGS:(hist - j) * SCAN_SEGS + n, :] for j in range(CONV_W)]
    xc = _conv_taps(cw_ref, cb_ref, taps)
    ra, ia = _lru_preact(xc, wa_ref, wi_ref)

    yr = _dot(hn_p, wxy_ref[:, LRU_W:2 * LRU_W])
    a_parts, u_parts = [], []
    per = n // SCAN_SEGS
    for k in range(SCAN_SEGS):
        gate = ffn_chunk(lead + k)
        tie = jnp.concatenate([_zero_after(gate[0:per, 0:LANES])] * LRU_CHUNKS, axis=-1)
        sl = slice(k * per, (k + 1) * per)
        a_k, u_k = _lru_decay_input(xc[sl] + tie, ra[sl], ia[sl], ba_ref, bi_ref, lam_ref)
        a_parts.append(a_k)
        u_parts.append(u_k)
    a = jnp.concatenate(a_parts, axis=0)
    u = jnp.concatenate(u_parts, axis=0)
    for c in range(lead + SCAN_SEGS, n_chunks):
        ffn_chunk(c)

    def rows_of(v, j):
        return v[j * SCAN_SEGS:(j + 1) * SCAN_SEGS, :]

    h = jnp.zeros((SCAN_SEGS, LRU_W), F32)
    prod = jnp.ones((SCAN_SEGS, LRU_W), F32)
    for j in range(SCAN_STEPS):
        a_j = rows_of(a, j)
        h = a_j * h + rows_of(u, j)
        prod = a_j * prod
    prod_c, h_c = _linear_scan(prod, h, SCAN_SEGS)
    h_end = prod_c * hc_ref[...] + h_c
    seg = lax.broadcasted_iota(jnp.int32, (SCAN_SEGS, 1), 0)
    h = jnp.where(seg == 0, hc_ref[...], pltpu.roll(h_end, 1, 0))
    for j in range(SCAN_STEPS):
        h = rows_of(a, j) * h + rows_of(u, j)
        dst = pl.ds(j, SCAN_SEGS, stride=SCAN_PITCH)
        for c in range(LRU_CHUNKS):
            hpad_ref[c, dst, :] = h[:, c * LANES:(c + 1) * LANES]
    hs = jnp.concatenate(
        [jnp.concatenate([hpad_ref[c, SCAN_PITCH * s:SCAN_PITCH * s + SCAN_STEPS, :] for c in range(LRU_CHUNKS)],
                         axis=-1) for s in range(SCAN_SEGS)], axis=0)
    yrec_ref[0] = (jax.nn.gelu(yr) * hs).astype(BF16)
    hc_ref[...] = h_end[SCAN_SEGS - 1:SCAN_SEGS, :]
    plru_ref[0] = h_end[SCAN_SEGS - 1:SCAN_SEGS, :]
    pconv_ref[0] = xr[rows - hist:rows, :]
    for c in range(LRU_CHUNKS):
        xpad_ref[c, tail, :] = xr[rows - SUBLANES:rows, c * LANES:(c + 1) * LANES]

    y = x + 0.5 * _dot(act_ref[...], wd_ref[...])
    o_ref[...] = y
    xbuf_ref[slot] = y


def _p_proj_body(x_ref, g_ref, w_ref, q0, q1, q2, k0, k1, k2, v0, v1, v2, pk0, pv0, pk1, pv1, pk2, pv2, zs_ref):
    rows = PROJ_TILE
    hn = _rms(x_ref[0], g_ref[...]).astype(BF16)
    outs = ((q0, q1, q2), (k0, k1, k2), (v0, v1, v2))
    wins = (None, (pk0, pk1, pk2), (pv0, pv1, pv2))
    for kind in range(3):
        for g, (window, dil) in enumerate(GROUPS):
            col = (kind * len(GROUPS) + g) * ATT_W
            z = _dot(hn, w_ref[:, col:col + ATT_W])
            if kind == 0:
                z = z * ATT_SCALE
            else:
                keep = min(window, rows)
                wins[kind][g][0] = jnp.transpose(z[rows - keep:, :])
            def emit(r, part):
                if kind == 0:
                    qrows = min(rows // dil, N_REL)
                    part = jnp.concatenate([_stack_heads(part[lo:lo + qrows, :])
                                            for lo in range(0, rows // dil, qrows)], axis=0)
                outs[kind][g][0, r] = part.astype(BF16)

            if dil == 1:
                emit(0, z)
            else:
                stage = zs_ref.at[kind * (len(GROUPS) - 1) + g - 1]
                for c in range(ATT_CHUNKS):
                    stage[c] = z[:, c * LANES:(c + 1) * LANES]
                for r in range(dil):
                    sel = pl.ds(r, rows // dil, stride=dil)
                    emit(r, jnp.concatenate([stage[c, sel, :] for c in range(ATT_CHUNKS)], axis=-1))


def _softmax_parts(s):
    m = jnp.max(s, axis=-1, keepdims=True)
    p = jnp.exp(s - m)
    return p.astype(BF16), jnp.sum(p, axis=-1, keepdims=True), m


def _gate(hn, wc_ref, bg_ref, gate_col, idx):
    lo = idx * D_MODEL
    return jax.nn.sigmoid(_dot(hn, wc_ref[:, gate_col + lo:gate_col + lo + D_MODEL]) + bg_ref[:, lo:lo + D_MODEL])


def _p_mix_body(x_ref, q0, q1, q2, k0, k1, k2, v0, v1, v2, yr_ref, mkv_ref, b0, b1, b2,
                g_ref, wc_ref, bg_ref, wao_ref, wro_ref, wmo_ref, wout_ref,
                o_ref, kh0, kh1, kh2, vh0, vh1, vh2, o_scr, l_scr, *, tiles_per_seq):
    tile = MIX_TILE
    t = pl.program_id(1)
    slot = lax.rem(t, 2)
    qs_, ks_, vs_ = (q0, q1, q2), (k0, k1, k2), (v0, v1, v2)
    khs, vhs, biases = (kh0, kh1, kh2), (vh0, vh1, vh2), (b0, b1, b2)
    ring = 2 * N_REL

    def in_ring(g):
        return tile // GROUPS[g][1] < N_REL

    @pl.when(t == 0)
    def _():
        for g, (window, dil) in enumerate(GROUPS):
            for h_ref in (khs[g], vhs[g]):
                if in_ring(g):
                    h_ref[...] = jnp.zeros(h_ref.shape, BF16)
                else:
                    cur = tile // dil
                    for r in range(dil):
                        h_ref[0, r * (N_REL + cur):r * (N_REL + cur) + N_REL, :] = jnp.zeros((N_REL, ATT_W), BF16)

    res = {}

    windows, carried = [], []

    def window_block(g, dil, r, lo, qrows, keys, tab, new_rows):
        def rows_of(h_ref, sel):
            return h_ref.at[sel] if in_ring(g) else h_ref.at[slot, sel]

        def scores():
            if new_rows is not None:
                rows_of(khs[g], new_rows)[...] = ks_[g][0, r]
                rows_of(vhs[g], new_rows)[...] = vs_[g][0, r]
            q = qs_[g][0, r, HEADS * lo:HEADS * (lo + qrows), :]
            return _softmax_parts(_dot_nt(q, rows_of(khs[g], keys)[...]) + tab)

        def values(parts):
            p, l, m = parts
            o = _dot(p, rows_of(vhs[g], keys)[...]) * (1.0 / l)
            lse = jnp.broadcast_to(m + jnp.log(l), o.shape)
            o_c, l_c = _unstack_heads(o, qrows), _unstack_heads(lse, qrows)
            dst = slice(lo, lo + qrows) if dil == 1 else pl.ds(lo * dil + r, qrows, stride=dil)
            for c in range(ATT_CHUNKS):
                o_scr[g * ATT_CHUNKS + c, dst, :] = o_c[c]
                l_scr[g * ATT_CHUNKS + c, dst, :] = l_c[c]
        return scores, values

    for g, (window, dil) in enumerate(GROUPS):
        cur = tile // dil
        if in_ring(g):
            slots = ring // cur
            tabs = biases[g].shape[0]
            tab = biases[g][t if tabs >= tiles_per_seq else jnp.where(t < slots, t, slots + lax.rem(t, slots))]
            newest = lax.rem(t, slots) * cur
            for r in range(dil):
                base = r * ring
                windows.append(window_block(g, dil, r, 0, cur, slice(base, base + ring), tab,
                                            pl.ds(pl.multiple_of(base + newest, cur), cur)))
        else:
            for r in range(dil):
                base = r * (N_REL + cur)
                carried.append((g, base, cur))
                for j in range(cur // N_REL):
                    tab = biases[g][jnp.where(t == 0, 1, 0)] if j == 0 else biases[g][0]
                    lo = j * N_REL
                    windows.append(window_block(g, dil, r, lo, N_REL, slice(base + lo, base + lo + ring), tab,
                                                slice(base + N_REL, base + N_REL + cur) if j == 0 else None))

    def memory_block(h):
        sl = slice(h * MEM_HEAD_DIM, (h + 1) * MEM_HEAD_DIM)

        def scores():
            return _softmax_parts(_dot_nt(res["qm"][:, sl].astype(BF16), mkv_ref[0, :, sl]) * MEM_SCALE)

        def values(parts):
            p, l, _ = parts
            res["mem", h] = _dot(p, mkv_ref[0, :, MEM_W + h * MEM_HEAD_DIM:MEM_W + (h + 1) * MEM_HEAD_DIM]) * (1.0 / l)
        return scores, values

    def normed_input():
        res["x"] = x_ref[0]
        res["hn"] = _rms(res["x"], g_ref[...]).astype(BF16)
        res["qm"] = _dot(res["hn"], wc_ref[:, 0:MEM_W])

    def rec_term():
        res["rec"] = _gate(res["hn"], wc_ref, bg_ref, MEM_W, 1) * _dot(yr_ref[0], wro_ref[...])

    def att_gate():
        res["att_gate"] = _gate(res["hn"], wc_ref, bg_ref, MEM_W, 0)

    def mem_gate():
        res["mem_gate"] = _gate(res["hn"], wc_ref, bg_ref, MEM_W, 2)

    normed_input()
    blocks = windows + [memory_block(h) for h in range(MEM_HEADS)]
    after = {6: rec_term, 13: att_gate, 20: mem_gate}

    pending = []
    for i, (scores, values) in enumerate(blocks):
        pending.append((values, scores()))
        if len(pending) > ATTN_DEPTH:
            done, parts = pending.pop(0)
            done(parts)
        if i in after:
            after[i]()
    for done, parts in pending:
        done(parts)

    mixed = []
    for c in range(ATT_CHUNKS):
        l0, l1, l2 = (l_scr[g * ATT_CHUNKS + c] for g in range(len(GROUPS)))
        o0, o1, o2 = (o_scr[g * ATT_CHUNKS + c] for g in range(len(GROUPS)))
        m = jnp.maximum(jnp.maximum(l0, l1), l2)
        w0, w1, w2 = jnp.exp(l0 - m), jnp.exp(l1 - m), jnp.exp(l2 - m)
        mixed.append((w0 * o0 + w1 * o1 + w2 * o2) / (w0 + w1 + w2))
    ya = jnp.concatenate(mixed, axis=-1).astype(BF16)
    ym = jnp.concatenate([res["mem", h] for h in range(MEM_HEADS)], axis=-1).astype(BF16)
    merged = (res["att_gate"] * _dot(ya, wao_ref[...]) + res["rec"]
              + res["mem_gate"] * _dot(ym, wmo_ref[...]))
    o_ref[0] = res["x"] + _dot(merged.astype(BF16), wout_ref[...])
    for g, base, cur in carried:
        for h_ref in (khs[g], vhs[g]):
            h_ref[1 - slot, base:base + N_REL, :] = h_ref[slot, base + cur:base + cur + N_REL, :]


def _merge_out(x, hn, branches, wc_ref, gate_col, bg_ref, wout_ref):
    merged = None
    for idx, (y, w_ref) in enumerate(branches):
        term = _gate(hn, wc_ref, bg_ref, gate_col, idx) * _dot(y, w_ref[...])
        merged = term if merged is None else merged + term
    return x + _dot(merged.astype(BF16), wout_ref[...])


def _merge_body(x_ref, ya_ref, yr_ref, ym_ref, g_ref, wc_ref, bg_ref, wao_ref, wro_ref, wmo_ref, wout_ref, o_ref):
    x = x_ref[0]
    hn = _rms(x, g_ref[...]).astype(BF16)
    o_ref[0] = _merge_out(x, hn, ((ya_ref[0], wao_ref), (yr_ref[0], wro_ref), (ym_ref[0], wmo_ref)),
                          wc_ref, 0, bg_ref, wout_ref)


def _s_mix_body(z_ref, ck0, cv0, ck1, cv1, ck2, cv2, sconv_ref, slru_ref, cmk_ref, cmv_ref,
                bc0, bc1, bc2, bn_ref, cw_ref, cb_ref, wa_ref, wi_ref, ba_ref, bi_ref, lam_ref,
                ya_ref, yrec_ref, ym_ref, ok0, ov0, ok1, ov1, ok2, ov2, oconv_ref, olru_ref,
                ext_ref, *, steps):
    cks, cvs = (ck0, ck1, ck2), (cv0, cv1, cv2)
    oks, ovs = (ok0, ok1, ok2), (ov0, ov1, ov2)
    bcs = (bc0, bc1, bc2)
    zeros = jnp.zeros((N_REL - steps, ATT_W), F32)
    new_lanes = lax.broadcasted_iota(jnp.int32, (ATT_W, LANES), 1) >= LANES - steps
    outs, lses = [], []
    for g, (window, dil) in enumerate(GROUPS):
        q = z_ref[0, :, g * ATT_W:(g + 1) * ATT_W] * ATT_SCALE
        kn = z_ref[0, :, (3 + g) * ATT_W:(4 + g) * ATT_W]
        vn = z_ref[0, :, (6 + g) * ATT_W:(7 + g) * ATT_W]
        qs = _stack_heads(q).astype(BF16)
        kt = cks[g][0]
        vt = cvs[g][0]
        s_c = _dot(qs, kt.astype(BF16)) + bcs[g][...]
        s_n = _dot_nt(qs, jnp.concatenate([kn, zeros], axis=0).astype(BF16)) + bn_ref[g]
        m = jnp.maximum(jnp.max(s_c, axis=-1, keepdims=True), jnp.max(s_n, axis=-1, keepdims=True))
        p_c = jnp.exp(s_c - m)
        p_n = jnp.exp(s_n - m)
        l = jnp.sum(p_c, axis=-1, keepdims=True) + jnp.sum(p_n, axis=-1, keepdims=True)
        o = (_dot_nt(p_c.astype(BF16), vt.astype(BF16))
             + _dot(p_n.astype(BF16), jnp.concatenate([vn, zeros], axis=0).astype(BF16))) * (1.0 / l)
        lse = jnp.broadcast_to(m + jnp.log(l), o.shape)
        outs.append(jnp.concatenate(_unstack_heads(o, steps), axis=-1))
        lses.append(jnp.concatenate(_unstack_heads(lse, steps), axis=-1))
        for old, new, dst in ((kt, kn, oks[g]), (vt, vn, ovs[g])):
            rolled = pltpu.roll(old, window - steps, 1)
            new_t = jnp.transpose(jnp.concatenate([zeros, new], axis=0))
            if window > LANES:
                dst[0, :, 0:window - LANES] = rolled[:, 0:window - LANES]
            dst[0, :, window - LANES:window] = jnp.where(new_lanes, new_t, rolled[:, window - LANES:window])
    m = jnp.maximum(jnp.maximum(lses[0], lses[1]), lses[2])
    ws = [jnp.exp(l - m) for l in lses]
    ya_ref[0] = ((ws[0] * outs[0] + ws[1] * outs[1] + ws[2] * outs[2]) / (ws[0] + ws[1] + ws[2])).astype(BF16)

    hist = CONV_W - 1
    ext_ref[0:SUBLANES, :] = jnp.zeros((SUBLANES, LRU_W), F32)
    ext_ref[SUBLANES - hist:SUBLANES, :] = sconv_ref[0]
    ext_ref[SUBLANES:SUBLANES + steps, :] = z_ref[0, :, COL_XR:COL_XR + LRU_W]
    taps = [ext_ref[SUBLANES - j:SUBLANES - j + steps, :] for j in range(CONV_W)]
    a, u = _lru_gates(_conv_taps(cw_ref, cb_ref, taps), wa_ref, wi_ref, ba_ref, bi_ref, lam_ref)
    a_c, u_c = _linear_scan(a, u, steps)
    h = a_c * slru_ref[0] + u_c
    yrec_ref[0] = (jax.nn.gelu(z_ref[0, :, COL_YR:COL_YR + LRU_W]) * h).astype(BF16)
    olru_ref[0] = h[steps - 1:steps, :]
    oconv_ref[0] = ext_ref[SUBLANES + steps - hist:SUBLANES + steps, :]

    qm = z_ref[0, :, COL_QM:COL_QM + MEM_W]
    mk = [cmk_ref[0, pl.ds(h, N_MEM, stride=MEM_HEADS), :].astype(BF16) for h in range(MEM_HEADS)]
    mv = [cmv_ref[0, pl.ds(h, N_MEM, stride=MEM_HEADS), :].astype(BF16) for h in range(MEM_HEADS)]
    ym_ref[0] = _mem_attention(qm, mk, mv).astype(BF16)


def _resident(shape):
    zeros = (0,) * len(shape)
    return pl.BlockSpec(shape, lambda *_: zeros, pipeline_mode=pl.Buffered(1))


def _params(*semantics):
    return pltpu.CompilerParams(dimension_semantics=semantics, vmem_limit_bytes=V7X_VMEM_LIMIT)


def _ffn(x, norm_g, w_gu, w_down, final_g=None):
    n = x.shape[0]
    tile = min(FFN_TILE, n)
    row = pl.BlockSpec((tile, D_MODEL), lambda i: (i, 0))
    in_specs = [row, _resident((1, D_MODEL)), _resident(w_gu.shape), _resident(w_down.shape)]
    args = [x, norm_g, w_gu, w_down]
    if final_g is not None:
        in_specs.append(_resident((1, D_MODEL)))
        args.append(final_g)
    return pl.pallas_call(
        functools.partial(_ffn_body, final_norm=final_g is not None),
        grid=(n // tile,),
        in_specs=in_specs,
        out_specs=row,
        out_shape=jax.ShapeDtypeStruct((n, D_MODEL), F32),
        scratch_shapes=[pltpu.VMEM((tile, D_FF), BF16)],
        compiler_params=_params("parallel"),
        name="ffn_final" if final_g is not None else "ffn",
    )(*args)


def _norm_mm(x, norm_g, w, row_tile, col_tile):
    n, cols = x.shape[0], w.shape[1]
    return pl.pallas_call(
        _norm_mm_body,
        grid=(n // row_tile, cols // col_tile),
        in_specs=[pl.BlockSpec((row_tile, D_MODEL), lambda i, j: (i, 0)),
                  _resident((1, D_MODEL)),
                  pl.BlockSpec((D_MODEL, col_tile), lambda i, j: (0, j))],
        out_specs=pl.BlockSpec((row_tile, col_tile), lambda i, j: (i, j)),
        out_shape=jax.ShapeDtypeStruct((n, cols), F32),
        compiler_params=_params("parallel", "parallel"),
        name="norm_mm",
    )(x, norm_g, w)


def _mem_kv(mem, norm_g, w_kv, row_tile):
    n = mem.shape[0]
    state = jax.ShapeDtypeStruct((n * MEM_HEADS, MEM_HEAD_DIM), F32)
    state_spec = pl.BlockSpec((row_tile * MEM_HEADS, MEM_HEAD_DIM), lambda i: (i, 0))
    return pl.pallas_call(
        _mem_kv_body,
        grid=(n // row_tile,),
        in_specs=[pl.BlockSpec((row_tile, D_MODEL), lambda i: (i, 0)),
                  _resident((1, D_MODEL)), _resident(w_kv.shape)],
        out_specs=[state_spec, state_spec, pl.BlockSpec((row_tile, 2 * MEM_W), lambda i: (i, 0))],
        out_shape=[state, state, jax.ShapeDtypeStruct((n, 2 * MEM_W), BF16)],
        compiler_params=_params("parallel"),
        name="mem_kv",
    )(mem, norm_g, w_kv)


def _ffn_rec(x, norm_g, w_gu, w_down, mix_g, w_xy, lru, batch, seq):
    tile = FFN_TILE
    nt = seq // tile
    n = batch * nt
    last = n - 1

    def prev(i):
        return jnp.maximum(i - 1, 0)

    weights = [norm_g, w_gu, w_down, mix_g, w_xy] + list(lru)
    return pl.pallas_call(
        functools.partial(_ffn_rec_body, tiles_per_seq=nt),
        grid=(n + 1,),
        in_specs=[pl.BlockSpec((tile, D_MODEL), lambda i: (jnp.minimum(i, last), 0))]
        + [_resident(w.shape) for w in weights],
        out_specs=[pl.BlockSpec((tile, D_MODEL), lambda i: (jnp.minimum(i, last), 0)),
                   pl.BlockSpec((1, tile, LRU_W), lambda i: (prev(i) // nt, prev(i) % nt, 0)),
                   pl.BlockSpec((1, CONV_W - 1, LRU_W), lambda i: (prev(i) // nt, 0, 0)),
                   pl.BlockSpec((1, 1, LRU_W), lambda i: (prev(i) // nt, 0, 0))],
        out_shape=[jax.ShapeDtypeStruct((n * tile, D_MODEL), F32),
                   jax.ShapeDtypeStruct((batch, seq, LRU_W), BF16),
                   jax.ShapeDtypeStruct((batch, CONV_W - 1, LRU_W), F32),
                   jax.ShapeDtypeStruct((batch, 1, LRU_W), F32)],
        scratch_shapes=[pltpu.VMEM((tile, D_FF), BF16),
                        pltpu.VMEM((2, tile, D_MODEL), F32),
                        pltpu.VMEM((LRU_CHUNKS, SCAN_PITCH * (SCAN_SEGS + 1), LANES), F32),
                        pltpu.VMEM((LRU_CHUNKS, SCAN_PITCH * SCAN_SEGS, LANES), F32),
                        pltpu.VMEM((1, LRU_W), F32)],
        compiler_params=_params("arbitrary"),
        name="ffn_rec",
    )(x, *weights)


def _p_proj(x1, norm_g, w_qkv):
    batch, seq, _ = x1.shape
    tile = PROJ_TILE
    nt = seq // tile
    qkv_shapes, qkv_specs = [], []
    for _kind in range(3):
        for window, dil in GROUPS:
            copies = HEADS if _kind == 0 else 1
            qkv_shapes.append(jax.ShapeDtypeStruct((batch, dil, copies * seq // dil, ATT_W), BF16))
            qkv_specs.append(pl.BlockSpec((1, dil, copies * tile // dil, ATT_W), lambda b, t: (b, 0, t, 0)))
    win_shapes, win_specs = [], []
    for window, dil in GROUPS:
        keep = min(window, seq)
        blk = min(keep, tile)
        first = nt - max(keep // tile, 1)
        for _kv in range(2):
            win_shapes.append(jax.ShapeDtypeStruct((batch, ATT_W, keep), F32))
            win_specs.append(pl.BlockSpec((1, ATT_W, blk),
                                          lambda b, t, first=first: (b, 0, jnp.maximum(t - first, 0))))
    return pl.pallas_call(
        _p_proj_body,
        grid=(batch, nt),
        in_specs=[pl.BlockSpec((1, tile, D_MODEL), lambda b, t: (b, t, 0)),
                  _resident((1, D_MODEL)), _resident(w_qkv.shape)],
        out_specs=qkv_specs + win_specs,
        out_shape=qkv_shapes + win_shapes,
        scratch_shapes=[pltpu.VMEM((3 * (len(GROUPS) - 1), ATT_CHUNKS, tile, LANES), F32)],
        compiler_params=_params("parallel", "arbitrary"),
        name="p_proj",
    )(x1, norm_g, w_qkv)


def _p_mix(x1, qkv, y_rec, mkv, biases, norm_g, w_c, b_gate, w_att_o, w_rec_o, w_mem_o, w_out):
    batch, seq, _ = x1.shape
    tile = MIX_TILE

    def tok(width):
        return pl.BlockSpec((1, tile, width), lambda b, t: (b, t, 0))

    q_specs, kv_specs, hist = [], [], []
    for window, dil in GROUPS:
        cur = tile // dil
        q_specs.append(pl.BlockSpec((1, dil, HEADS * cur, ATT_W), lambda b, t: (b, 0, t, 0)))
        kv_specs.append(pl.BlockSpec((1, dil, cur, ATT_W), lambda b, t: (b, 0, t, 0)))
        hist.append(pltpu.VMEM((dil * 2 * N_REL, ATT_W) if cur < N_REL else (2, dil * (N_REL + cur), ATT_W), BF16))
    weights = [norm_g, w_c, b_gate, w_att_o, w_rec_o, w_mem_o, w_out]
    in_specs = ([tok(D_MODEL)] + q_specs + kv_specs * 2
                + [tok(LRU_W), pl.BlockSpec((1, N_MEM, 2 * MEM_W), lambda b, t: (b, 0, 0))]
                + [_resident(a.shape) for a in list(biases) + weights])
    return pl.pallas_call(
        functools.partial(_p_mix_body, tiles_per_seq=seq // tile),
        grid=(batch, seq // tile),
        in_specs=in_specs,
        out_specs=tok(D_MODEL),
        out_shape=jax.ShapeDtypeStruct((batch, seq, D_MODEL), F32),
        scratch_shapes=hist * 2 + [pltpu.VMEM((len(GROUPS) * ATT_CHUNKS, tile, LANES), F32)] * 2,
        compiler_params=_params("parallel", "arbitrary"),
        name="p_mix",
    )(x1, *qkv, y_rec, mkv, *biases, *weights)


def _merge(x1, y_att, y_rec, y_mem, norm_g, w_c, b_gate, w_att_o, w_rec_o, w_mem_o, w_out):
    batch, seq, _ = x1.shape
    tile = seq

    def tok(width):
        return pl.BlockSpec((1, tile, width), lambda b, t: (b, t, 0))

    weights = [norm_g, w_c, b_gate, w_att_o, w_rec_o, w_mem_o, w_out]
    return pl.pallas_call(
        _merge_body,
        grid=(batch, seq // tile),
        in_specs=[tok(D_MODEL), tok(ATT_W), tok(LRU_W), tok(MEM_W)] + [_resident(w.shape) for w in weights],
        out_specs=tok(D_MODEL),
        out_shape=jax.ShapeDtypeStruct((batch, seq, D_MODEL), F32),
        compiler_params=_params("parallel", "parallel"),
        name="merge_s",
    )(x1, y_att, y_rec, y_mem, *weights)


def _s_mix(z, caches, state_conv, state_lru, cmk, cmv, bias_c, bias_n, lru):
    batch, steps, _ = z.shape

    def per_batch(shape):
        return pl.BlockSpec((1,) + tuple(shape[1:]), lambda b: (b,) + (0,) * (len(shape) - 1))

    ins = [z] + list(caches) + [state_conv, state_lru, cmk, cmv]
    in_specs = [per_batch(a.shape) for a in ins]
    consts = list(bias_c) + [bias_n] + list(lru)
    in_specs += [_resident(a.shape) for a in consts]
    out_shapes = [jax.ShapeDtypeStruct((batch, steps, ATT_W), BF16),
                  jax.ShapeDtypeStruct((batch, steps, LRU_W), BF16),
                  jax.ShapeDtypeStruct((batch, steps, MEM_W), BF16)]
    out_shapes += [jax.ShapeDtypeStruct(c.shape, F32) for c in caches]
    out_shapes += [jax.ShapeDtypeStruct(state_conv.shape, F32), jax.ShapeDtypeStruct(state_lru.shape, F32)]
    return pl.pallas_call(
        functools.partial(_s_mix_body, steps=steps),
        grid=(batch,),
        in_specs=in_specs,
        out_specs=[per_batch(s.shape) for s in out_shapes],
        out_shape=out_shapes,
        scratch_shapes=[pltpu.VMEM((2 * SUBLANES, LRU_W), F32)],
        compiler_params=_params("parallel"),
        name="s_mix",
    )(*ins, *consts)


def _bucket(n):
    n = np.maximum(n, 0)
    max_exact = NUM_BUCKETS // 2
    nf = np.maximum(n, 1).astype(np.float32)
    large = max_exact + (np.log(nf / max_exact) / np.float32(math.log(MAX_DISTANCE / max_exact))
                         * (NUM_BUCKETS - max_exact)).astype(np.int32)
    return np.where(n < max_exact, n, np.minimum(large, NUM_BUCKETS - 1))


def _dist_bias(rel_bias, g, dist, valid):
    tab = rel_bias[:, g * HEADS:(g + 1) * HEADS].T[:, _bucket(dist)]
    return jnp.where(valid[None], tab, -jnp.inf).astype(F32)


def _toeplitz(u, rows, cols):
    lu = rows + cols - 1
    ue = jnp.concatenate([u, u[:, :1]], axis=1)
    m = jnp.tile(ue, (1, rows))[:, :rows * lu].reshape(-1, rows, lu)
    return m[:, :, rows - 1:rows - 1 + cols].reshape(-1, cols)


def _prompt_bias(rel_bias, tile, tiles_per_seq):
    ring = 2 * N_REL
    col = np.arange(ring)
    stacks = []
    for g, (_, dil) in enumerate(GROUPS):
        cur = tile // dil
        qrows = min(cur, N_REL)
        hist = ring - qrows
        delta = np.arange(-(qrows - 1), ring)
        v = _dist_bias(rel_bias, g, dil * delta, (delta >= 0) & (delta <= N_REL))
        base = _toeplitz(v[:, ::-1], qrows, ring)
        if cur >= N_REL:
            tabs = [base, jnp.where(col >= hist, base, -jnp.inf)]
        else:
            slots = ring // cur
            tabs = []
            for t in range(min(tiles_per_seq, 2 * slots)):
                cols = []
                for s in range(slots):
                    age = (t - s) % slots
                    blk = base[:, hist - cur * age:hist - cur * age + cur]
                    cols.append(blk if t - age >= 0 else jnp.full_like(blk, -jnp.inf))
                tabs.append(jnp.concatenate(cols, axis=1))
        stacks.append(jnp.stack(tabs))
    return stacks


def _sample_bias(rel_bias, steps):
    t = np.arange(steps)[:, None]
    cache, new = [], []
    for g, (window, dil) in enumerate(GROUPS):
        dist = np.arange(window + steps)
        tab = _dist_bias(rel_bias, g, dist, (dist % dil == 0) & (dist <= window))
        cache.append(_toeplitz(tab[:, 1:][:, ::-1], steps, window))
        j = np.arange(N_REL)[None, :]
        dist = t - j
        valid = (j < steps) & (dist >= 0) & (dist % dil == 0)
        new.append(_dist_bias(rel_bias, g, dist, valid).reshape(HEADS * steps, N_REL))
    return cache, jnp.stack(new)


def _block_diag(w):
    per = MXU_TILE // LRU_BLOCK
    w4 = w.reshape(-1, per, LRU_BLOCK, LRU_BLOCK)
    bd = jnp.einsum('jnef,nm->jnemf', w4, jnp.eye(per, dtype=w.dtype))
    return bd.reshape(-1, MXU_TILE, MXU_TILE).astype(BF16)


def _to_pos_minor(c):
    return jnp.transpose(c[0], (0, 2, 3, 1)).reshape(c.shape[1], ATT_W, c.shape[2])


def _from_pos_minor(c):
    return jnp.transpose(c.reshape(c.shape[0], HEADS, HEAD_DIM, c.shape[2]), (0, 3, 1, 2))[None]


def kernel(x_prompt, x_sample, mem_prompt, cache_win_k0, cache_win_v0, cache_win_k1, cache_win_v1, cache_win_k2, cache_win_v2, state_conv, state_lru, cache_mem_k, cache_mem_v, rel_bias, ffn1_norm, ffn1_w_gu, ffn1_w_down, mix_norm, w_in, b_gate, conv_w, conv_b, lru_w_a, lru_b_a, lru_w_i, lru_b_i, lru_lambda, mem_norm, w_mem_kv, w_att_o, w_rec_o, w_mem_o, w_out, ffn2_norm, ffn2_w_gu, ffn2_w_down, final_norm):
    depth = ffn1_norm.shape[0]
    assert depth == 1
    l = 0
    batch, seq, _ = x_prompt.shape
    sbatch, steps, _ = x_sample.shape
    row = lambda v: v.reshape(1, -1)

    lru = (conv_w[l], row(conv_b[l]), _block_diag(lru_w_a[l]), _block_diag(lru_w_i[l]),
           row(lru_b_a[l]), row(lru_b_i[l]), row(lru_lambda[l]))
    ffn1 = (row(ffn1_norm[l]), ffn1_w_gu[l].astype(BF16), ffn1_w_down[l].astype(BF16))
    ffn2 = (row(ffn2_norm[l]), ffn2_w_gu[l].astype(BF16), ffn2_w_down[l].astype(BF16))
    outw = (row(b_gate[l]), w_att_o[l].astype(BF16), w_rec_o[l].astype(BF16), w_mem_o[l].astype(BF16),
            w_out[l].astype(BF16))
    mixg = row(mix_norm[l])
    fin = row(final_norm)

    xp1, y_rec, p_conv, p_lru = _ffn_rec(x_prompt.reshape(batch * seq, D_MODEL), *ffn1, mixg,
                                         w_in[l][:, COL_XR:COL_QM].astype(BF16), lru, batch, seq)
    xp1 = xp1.reshape(batch, seq, D_MODEL)
    p_mk, p_mv, mkv = _mem_kv(mem_prompt.reshape(batch * N_MEM, D_MODEL), row(mem_norm[l]),
                              w_mem_kv[l].astype(BF16), row_tile=2 * N_MEM)
    proj = _p_proj(xp1, mixg, w_in[l][:, :QKV_W].astype(BF16))
    qkv, wins = proj[0:9], proj[9:15]
    xp2 = _p_mix(xp1, qkv, y_rec, mkv.reshape(batch, N_MEM, 2 * MEM_W), _prompt_bias(rel_bias, MIX_TILE, seq // MIX_TILE), mixg,
                 w_in[l][:, COL_QM:].astype(BF16), *outw)
    y_prompt = _ffn(xp2.reshape(batch * seq, D_MODEL), *ffn2, final_g=fin).reshape(batch, seq, D_MODEL)

    n_s = sbatch * steps
    xs1 = _ffn(x_sample.reshape(n_s, D_MODEL), *ffn1)
    z = _norm_mm(xs1, mixg, w_in[l].astype(BF16), row_tile=n_s, col_tile=IN_COLS // 2)
    caches = [_to_pos_minor(c) for c in
              (cache_win_k0, cache_win_v0, cache_win_k1, cache_win_v1, cache_win_k2, cache_win_v2)]
    bias_c, bias_n = _sample_bias(rel_bias, steps)
    mem_rows = lambda c: c[l].reshape(sbatch, N_MEM * MEM_HEADS, MEM_HEAD_DIM)
    mix = _s_mix(z.reshape(sbatch, steps, IN_COLS), caches, state_conv[l], state_lru[l].reshape(sbatch, 1, LRU_W),
                 mem_rows(cache_mem_k), mem_rows(cache_mem_v), bias_c, bias_n, lru)
    s_att, s_rec, s_mem = (a.reshape(1, n_s, a.shape[-1]) for a in mix[0:3])
    xs2 = _merge(xs1.reshape(1, n_s, D_MODEL), s_att, s_rec, s_mem, mixg, w_in[l][:, COL_GATE:].astype(BF16),
                 *outw)
    y_sample = _ffn(xs2.reshape(n_s, D_MODEL), *ffn2, final_g=fin).reshape(sbatch, steps, D_MODEL)

    mem_state = lambda a: a.reshape(1, batch, N_MEM, MEM_HEADS, MEM_HEAD_DIM)
    return (y_prompt, y_sample, *[_from_pos_minor(w) for w in wins],
            p_conv[None], p_lru.reshape(1, batch, LRU_W), mem_state(p_mk), mem_state(p_mv),
            *[_from_pos_minor(w) for w in mix[3:9]], mix[9][None], mix[10].reshape(1, sbatch, LRU_W))
```

```python
import functools
import math

import numpy as np
import jax
import jax.numpy as jnp
from jax import lax
from jax.experimental import pallas as pl
from jax.experimental.pallas import tpu as pltpu

F32 = jnp.float32
BF16 = jnp.bfloat16

D_MODEL = 1024
D_FF = 2816
HEAD_DIM = 64
HEADS = 4
GROUPS = ((128, 1), (512, 4), (2048, 16))
N_REL = 128
ATT_W = HEADS * HEAD_DIM
ATT_SCALE = HEAD_DIM ** -0.5
NUM_BUCKETS = 32
MAX_DISTANCE = 2048
LRU_W = 768
LRU_BLOCK = 64
LRU_C = 8.0
CONV_W = 4
N_MEM = 256
MEM_HEADS = 4
MEM_HEAD_DIM = 128
MEM_W = MEM_HEADS * MEM_HEAD_DIM
MEM_SCALE = MEM_HEAD_DIM ** -0.5
RMS_EPS = 1e-6
QKV_W = 3 * len(GROUPS) * ATT_W
COL_XR = QKV_W
COL_YR = COL_XR + LRU_W
COL_QM = COL_YR + LRU_W
COL_GATE = COL_QM + MEM_W
IN_COLS = COL_GATE + 3 * D_MODEL

V7X_VMEM_LIMIT = 62 * 1024 * 1024
MXU_TILE = 256
SUBLANES = 8
LANES = 128
FFN_TILE = 512
FFN_CHUNK = 256
PROJ_TILE = 512
SCAN_SEGS = SUBLANES
SCAN_STEPS = FFN_TILE // SCAN_SEGS
SCAN_PITCH = SCAN_STEPS + SUBLANES
MIX_TILE = 512
ATTN_DEPTH = 6
LRU_CHUNKS = LRU_W // LANES
ATT_CHUNKS = ATT_W // LANES


def _dot(a, b):
    return jnp.dot(a, b, preferred_element_type=F32)


def _dot_nt(a, b):
    return lax.dot_general(a, b, (((1,), (1,)), ((), ())), preferred_element_type=F32)


def _rms(x, g):
    return x * lax.rsqrt(jnp.mean(x * x, axis=-1, keepdims=True) + RMS_EPS) * g


def _stack_heads(q):
    lane = lax.broadcasted_iota(jnp.int32, q.shape, 1)
    return jnp.concatenate(
        [jnp.where((lane >= h * HEAD_DIM) & (lane < (h + 1) * HEAD_DIM), q, 0.0) for h in range(HEADS)], axis=0)


def _unstack_heads(x, rows):
    low = lax.broadcasted_iota(jnp.int32, (rows, LANES), 1) < HEAD_DIM
    per = LANES // HEAD_DIM
    return [jnp.where(low,
                      x[(per * c) * rows:(per * c + 1) * rows, c * LANES:(c + 1) * LANES],
                      x[(per * c + 1) * rows:(per * c + 2) * rows, c * LANES:(c + 1) * LANES])
            for c in range(ATT_CHUNKS)]


def _linear_scan(a, u, rows):
    row = lax.broadcasted_iota(jnp.int32, (rows, 1), 0)
    d = 1
    while d < rows:
        a_s = pltpu.roll(a, d, 0)
        u_s = pltpu.roll(u, d, 0)
        keep = row >= d
        u = jnp.where(keep, a * u_s + u, u)
        a = jnp.where(keep, a * a_s, a)
        d *= 2
    return a, u


def _zero_after(x):
    bits = pltpu.bitcast(x, jnp.uint32)
    half = jnp.uint32(17)
    return pltpu.bitcast(lax.shift_right_logical(lax.shift_right_logical(bits, half), half), F32)


def _lru_preact(xc, wa_ref, wi_ref):
    xcb = xc.astype(BF16)
    ra, ia = [], []
    for j in range(LRU_W // MXU_TILE):
        blk = xcb[:, j * MXU_TILE:(j + 1) * MXU_TILE]
        ra.append(_dot(blk, wa_ref[j]))
        ia.append(_dot(blk, wi_ref[j]))
    return jnp.concatenate(ra, axis=-1), jnp.concatenate(ia, axis=-1)


def _lru_decay_input(xc, ra, ia, ba_ref, bi_ref, lam_ref):
    r = jax.nn.sigmoid(ra + ba_ref[...])
    i = jax.nn.sigmoid(ia + bi_ref[...])
    nl = -lam_ref[...]
    softplus = jnp.maximum(nl, 0.0) + jnp.log1p(jnp.exp(-jnp.abs(nl)))
    log_a = (-LRU_C * r) * softplus
    a = jnp.exp(log_a)
    u = jnp.sqrt(1.0 - a * a) * (i * xc)
    return a, u


def _lru_gates(xc, wa_ref, wi_ref, ba_ref, bi_ref, lam_ref):
    return _lru_decay_input(xc, *_lru_preact(xc, wa_ref, wi_ref), ba_ref, bi_ref, lam_ref)


def _conv_taps(cw_ref, cb_ref, taps):
    xc = cb_ref[...] + cw_ref[CONV_W - 1:CONV_W, :] * taps[0]
    for j in range(1, CONV_W):
        xc = xc + cw_ref[CONV_W - 1 - j:CONV_W - j, :] * taps[j]
    return xc


def _mem_attention(qm, mk_heads, mv_heads):
    heads = []
    for h in range(MEM_HEADS):
        q = qm[:, h * MEM_HEAD_DIM:(h + 1) * MEM_HEAD_DIM].astype(BF16)
        s = _dot_nt(q, mk_heads[h]) * MEM_SCALE
        m = jnp.max(s, axis=-1, keepdims=True)
        p = jnp.exp(s - m)
        l = jnp.sum(p, axis=-1, keepdims=True)
        heads.append(_dot(p.astype(BF16), mv_heads[h]) * (1.0 / l))
    return jnp.concatenate(heads, axis=-1)


def _ffn_body(x_ref, g_ref, wgu_ref, wd_ref, *rest, final_norm):
    if final_norm:
        fg_ref, o_ref, act_ref = rest
    else:
        o_ref, act_ref = rest
    x = x_ref[...]
    hn = _rms(x, g_ref[...]).astype(BF16)
    for c in range(D_FF // FFN_CHUNK):
        lo = c * FFN_CHUNK
        gate = _dot(hn, wgu_ref[:, lo:lo + FFN_CHUNK])
        up = _dot(hn, wgu_ref[:, D_FF + lo:D_FF + lo + FFN_CHUNK])
        act_ref[:, lo:lo + FFN_CHUNK] = (gate * jax.nn.sigmoid(gate) * up).astype(BF16)
    y = x + 0.5 * _dot(act_ref[...], wd_ref[...])
    if final_norm:
        y = _rms(y, fg_ref[...])
    o_ref[...] = y


def _norm_mm_body(x_ref, g_ref, w_ref, o_ref):
    hn = _rms(x_ref[...], g_ref[...]).astype(BF16)
    o_ref[...] = _dot(hn, w_ref[...])


def _mem_kv_body(x_ref, g_ref, w_ref, mk_ref, mv_ref, mkv_ref):
    rows = x_ref.shape[0]
    z = _dot(_rms(x_ref[...], g_ref[...]).astype(BF16), w_ref[...])
    mkv_ref[...] = z.astype(BF16)
    for h in range(MEM_HEADS):
        dst = pl.ds(h, rows, stride=MEM_HEADS)
        mk_ref[dst, :] = z[:, h * MEM_HEAD_DIM:(h + 1) * MEM_HEAD_DIM]
        mv_ref[dst, :] = z[:, MEM_W + h * MEM_HEAD_DIM:MEM_W + (h + 1) * MEM_HEAD_DIM]


def _ffn_rec_body(x_ref, g_ref, wgu_ref, wd_ref, mg_ref, wxy_ref, cw_ref, cb_ref, wa_ref, wi_ref, ba_ref, bi_ref,
                  lam_ref, o_ref, yrec_ref, pconv_ref, plru_ref, act_ref, xbuf_ref, xpad_ref, hpad_ref, hc_ref,
                  *, tiles_per_seq):
    rows = FFN_TILE
    i = pl.program_id(0)
    t = lax.rem(jnp.maximum(i - 1, 0), tiles_per_seq)
    slot = lax.rem(i, 2)
    tail = slice(SCAN_STEPS - SUBLANES, SCAN_STEPS)

    @pl.when(i == 0)
    def _():
        xbuf_ref[...] = jnp.zeros(xbuf_ref.shape, F32)

    @pl.when(t == 0)
    def _():
        xpad_ref[:, tail, :] = jnp.zeros((LRU_CHUNKS, SUBLANES, LANES), F32)
        hc_ref[...] = jnp.zeros((1, LRU_W), F32)

    hn_p = _rms(xbuf_ref[1 - slot], mg_ref[...]).astype(BF16)
    xr = _dot(hn_p, wxy_ref[:, 0:LRU_W])
    for c in range(LRU_CHUNKS):
        for s in range(SCAN_SEGS):
            lo = SCAN_PITCH * (s + 1)
            xpad_ref[c, lo:lo + SCAN_STEPS, :] = xr[s * SCAN_STEPS:(s + 1) * SCAN_STEPS, c * LANES:(c + 1) * LANES]

    x = x_ref[...]
    hn = _rms(x, g_ref[...]).astype(BF16)

    def ffn_chunk(c):
        lo = c * FFN_CHUNK
        gate = _dot(hn, wgu_ref[:, lo:lo + FFN_CHUNK])
        up = _dot(hn, wgu_ref[:, D_FF + lo:D_FF + lo + FFN_CHUNK])
        act_ref[:, lo:lo + FFN_CHUNK] = (gate * jax.nn.sigmoid(gate) * up).astype(BF16)
        return gate

    n_chunks = D_FF // FFN_CHUNK
    lead = 2
    for c in range(lead):
        ffn_chunk(c)

    def step_rows(j):
        start = SCAN_PITCH + j if j >= 0 else SCAN_STEPS + j
        sel = pl.ds(start, SCAN_SEGS, stride=SCAN_PITCH)
        return jnp.concatenate([xpad_ref[c, sel, :] for c in range(LRU_CHUNKS)], axis=-1)

    hist = CONV_W - 1
    xp = jnp.concatenate([step_rows(j) for j in range(-hist, SCAN_STEPS)], axis=0)
    n = SCAN_STEPS * SCAN_SEGS
    taps = [xp[(hist - j) * SCAN_SEGS:(hist - j) * SCAN_SEGS + n, :] for j in range(CONV_W)]
    xc = _conv_taps(cw_ref, cb_ref, taps)
    ra, ia = _lru_preact(xc, wa_ref, wi_ref)

    yr = _dot(hn_p, wxy_ref[:, LRU_W:2 * LRU_W])
    a_parts, u_parts = [], []
    per = n // SCAN_SEGS
    for k in range(SCAN_SEGS):
        gate = ffn_chunk(lead + k)
        tie = jnp.concatenate([_zero_after(gate[0:per, 0:LANES])] * LRU_CHUNKS, axis=-1)
        sl = slice(k * per, (k + 1) * per)
        a_k, u_k = _lru_decay_input(xc[sl] + tie, ra[sl], ia[sl], ba_ref, bi_ref, lam_ref)
        a_parts.append(a_k)
        u_parts.append(u_k)
    a = jnp.concatenate(a_parts, axis=0)
    u = jnp.concatenate(u_parts, axis=0)
    for c in range(lead + SCAN_SEGS, n_chunks):
        ffn_chunk(c)

    def rows_of(v, j):
        return v[j * SCAN_SEGS:(j + 1) * SCAN_SEGS, :]

    h = jnp.zeros((SCAN_SEGS, LRU_W), F32)
    prod = jnp.ones((SCAN_SEGS, LRU_W), F32)
    for j in range(SCAN_STEPS):
        a_j = rows_of(a, j)
        h = a_j * h + rows_of(u, j)
        prod = a_j * prod
    prod_c, h_c = _linear_scan(prod, h, SCAN_SEGS)
    h_end = prod_c * hc_ref[...] + h_c
    seg = lax.broadcasted_iota(jnp.int32, (SCAN_SEGS, 1), 0)
    h = jnp.where(seg == 0, hc_ref[...], pltpu.roll(h_end, 1, 0))
    for j in range(SCAN_STEPS):
        h = rows_of(a, j) * h + rows_of(u, j)
        dst = pl.ds(j, SCAN_SEGS, stride=SCAN_PITCH)
        for c in range(LRU_CHUNKS):
            hpad_ref[c, dst, :] = h[:, c * LANES:(c + 1) * LANES]
    hs = jnp.concatenate(
        [jnp.concatenate([hpad_ref[c, SCAN_PITCH * s:SCAN_PITCH * s + SCAN_STEPS, :] for c in range(LRU_CHUNKS)],
                         axis=-1) for s in range(SCAN_SEGS)], axis=0)
    yrec_ref[0] = (jax.nn.gelu(yr) * hs).astype(BF16)
    hc_ref[...] = h_end[SCAN_SEGS - 1:SCAN_SEGS, :]
    plru_ref[0] = h_end[SCAN_SEGS - 1:SCAN_SEGS, :]
    pconv_ref[0] = xr[rows - hist:rows, :]
    for c in range(LRU_CHUNKS):
        xpad_ref[c, tail, :] = xr[rows - SUBLANES:rows, c * LANES:(c + 1) * LANES]

    y = x + 0.5 * _dot(act_ref[...], wd_ref[...])
    o_ref[...] = y
    xbuf_ref[slot] = y


def _p_proj_body(x_ref, g_ref, w_ref, q0, q1, q2, k0, k1, k2, v0, v1, v2, pk0, pv0, pk1, pv1, pk2, pv2, zs_ref):
    rows = PROJ_TILE
    hn = _rms(x_ref[0], g_ref[...]).astype(BF16)
    outs = ((q0, q1, q2), (k0, k1, k2), (v0, v1, v2))
    wins = (None, (pk0, pk1, pk2), (pv0, pv1, pv2))
    for g, (window, dil) in reversed(list(enumerate(GROUPS))):
        for kind in (1, 2, 0):
            col = (kind * len(GROUPS) + g) * ATT_W
            z = _dot(hn, w_ref[:, col:col + ATT_W])
            if kind == 0:
                z = z * ATT_SCALE
            else:
                keep = min(window, rows)
                wins[kind][g][0] = jnp.transpose(z[rows - keep:, :])
            def emit(r, part):
                if kind == 0:
                    qrows = min(rows // dil, N_REL)
                    part = jnp.concatenate([_stack_heads(part[lo:lo + qrows, :])
                                            for lo in range(0, rows // dil, qrows)], axis=0)
                outs[kind][g][0, r] = part.astype(BF16)

            if dil == 1:
                emit(0, z)
            else:
                stage = zs_ref.at[kind * (len(GROUPS) - 1) + g - 1]
                for c in range(ATT_CHUNKS):
                    stage[c] = z[:, c * LANES:(c + 1) * LANES]
                for r in range(dil):
                    sel = pl.ds(r, rows // dil, stride=dil)
                    emit(r, jnp.concatenate([stage[c, sel, :] for c in range(ATT_CHUNKS)], axis=-1))


def _softmax_parts(s):
    m = jnp.max(s, axis=-1, keepdims=True)
    p = jnp.exp(s - m)
    return p.astype(BF16), jnp.sum(p, axis=-1, keepdims=True), m


def _gate(hn, wc_ref, bg_ref, gate_col, idx):
    lo = idx * D_MODEL
    return jax.nn.sigmoid(_dot(hn, wc_ref[:, gate_col + lo:gate_col + lo + D_MODEL]) + bg_ref[:, lo:lo + D_MODEL])


def _p_mix_body(x_ref, q0, q1, q2, k0, k1, k2, v0, v1, v2, yr_ref, mkv_ref, b0, b1, b2,
                g_ref, wc_ref, bg_ref, wao_ref, wro_ref, wmo_ref, wout_ref,
                o_ref, kh0, kh1, kh2, vh0, vh1, vh2, o_scr, l_scr, *, tiles_per_seq):
    tile = MIX_TILE
    t = pl.program_id(1)
    slot = lax.rem(t, 2)
    qs_, ks_, vs_ = (q0, q1, q2), (k0, k1, k2), (v0, v1, v2)
    khs, vhs, biases = (kh0, kh1, kh2), (vh0, vh1, vh2), (b0, b1, b2)
    ring = 2 * N_REL

    def in_ring(g):
        return tile // GROUPS[g][1] < N_REL

    @pl.when(t == 0)
    def _():
        for g, (window, dil) in enumerate(GROUPS):
            for h_ref in (khs[g], vhs[g]):
                if in_ring(g):
                    h_ref[...] = jnp.zeros(h_ref.shape, BF16)
                else:
                    cur = tile // dil
                    for r in range(dil):
                        h_ref[0, r * (N_REL + cur):r * (N_REL + cur) + N_REL, :] = jnp.zeros((N_REL, ATT_W), BF16)

    res = {}

    windows, carried = [], []

    def window_block(g, dil, r, lo, qrows, keys, tab, new_rows):
        def rows_of(h_ref, sel):
            return h_ref.at[sel] if in_ring(g) else h_ref.at[slot, sel]

        def scores():
            if new_rows is not None:
                rows_of(khs[g], new_rows)[...] = ks_[g][0, r]
                rows_of(vhs[g], new_rows)[...] = vs_[g][0, r]
            q = qs_[g][0, r, HEADS * lo:HEADS * (lo + qrows), :]
            return _softmax_parts(_dot_nt(q, rows_of(khs[g], keys)[...]) + tab)

        def values(parts):
            p, l, m = parts
            o = _dot(p, rows_of(vhs[g], keys)[...]) * (1.0 / l)
            lse = jnp.broadcast_to(m + jnp.log(l), o.shape)
            o_c, l_c = _unstack_heads(o, qrows), _unstack_heads(lse, qrows)
            dst = slice(lo, lo + qrows) if dil == 1 else pl.ds(lo * dil + r, qrows, stride=dil)
            for c in range(ATT_CHUNKS):
                o_scr[g * ATT_CHUNKS + c, dst, :] = o_c[c]
                l_scr[g * ATT_CHUNKS + c, dst, :] = l_c[c]
        return scores, values

    for g, (window, dil) in enumerate(GROUPS):
        cur = tile // dil
        if in_ring(g):
            slots = ring // cur
            tabs = biases[g].shape[0]
            tab = biases[g][t if tabs >= tiles_per_seq else jnp.where(t < slots, t, slots + lax.rem(t, slots))]
            newest = lax.rem(t, slots) * cur
            for r in range(dil):
                base = r * ring
                windows.append(window_block(g, dil, r, 0, cur, slice(base, base + ring), tab,
                                            pl.ds(pl.multiple_of(base + newest, cur), cur)))
        else:
            for r in range(dil):
                base = r * (N_REL + cur)
                carried.append((g, base, cur))
                for j in range(cur // N_REL):
                    tab = biases[g][jnp.where(t == 0, 1, 0)] if j == 0 else biases[g][0]
                    lo = j * N_REL
                    windows.append(window_block(g, dil, r, lo, N_REL, slice(base + lo, base + lo + ring), tab,
                                                slice(base + N_REL, base + N_REL + cur) if j == 0 else None))

    def memory_block(h):
        sl = slice(h * MEM_HEAD_DIM, (h + 1) * MEM_HEAD_DIM)

        def scores():
            return _softmax_parts(_dot_nt(res["qm"][:, sl].astype(BF16), mkv_ref[0, :, sl]) * MEM_SCALE)

        def values(parts):
            p, l, _ = parts
            res["mem", h] = _dot(p, mkv_ref[0, :, MEM_W + h * MEM_HEAD_DIM:MEM_W + (h + 1) * MEM_HEAD_DIM]) * (1.0 / l)
        return scores, values

    def normed_input():
        res["x"] = x_ref[0]
        res["hn"] = _rms(res["x"], g_ref[...]).astype(BF16)
        res["qm"] = _dot(res["hn"], wc_ref[:, 0:MEM_W])

    def rec_term():
        res["rec"] = _gate(res["hn"], wc_ref, bg_ref, MEM_W, 1) * _dot(yr_ref[0], wro_ref[...])

    def att_gate():
        res["att_gate"] = _gate(res["hn"], wc_ref, bg_ref, MEM_W, 0)

    def mem_gate():
        res["mem_gate"] = _gate(res["hn"], wc_ref, bg_ref, MEM_W, 2)

    normed_input()
    blocks = windows + [memory_block(h) for h in range(MEM_HEADS)]
    after = {6: rec_term, 13: att_gate, 20: mem_gate}

    pending = []
    for i, (scores, values) in enumerate(blocks):
        pending.append((values, scores()))
        if len(pending) > ATTN_DEPTH:
            done, parts = pending.pop(0)
            done(parts)
        if i in after:
            after[i]()
    for done, parts in pending:
        done(parts)

    mixed = []
    for c in range(ATT_CHUNKS):
        l0, l1, l2 = (l_scr[g * ATT_CHUNKS + c] for g in range(len(GROUPS)))
        o0, o1, o2 = (o_scr[g * ATT_CHUNKS + c] for g in range(len(GROUPS)))
        m = jnp.maximum(jnp.maximum(l0, l1), l2)
        w0, w1, w2 = jnp.exp(l0 - m), jnp.exp(l1 - m), jnp.exp(l2 - m)
        mixed.append((w0 * o0 + w1 * o1 + w2 * o2) / (w0 + w1 + w2))
    ya = jnp.concatenate(mixed, axis=-1).astype(BF16)
    ym = jnp.concatenate([res["mem", h] for h in range(MEM_HEADS)], axis=-1).astype(BF16)
    merged = (res["att_gate"] * _dot(ya, wao_ref[...]) + res["rec"]
              + res["mem_gate"] * _dot(ym, wmo_ref[...]))
    o_ref[0] = res["x"] + _dot(merged.astype(BF16), wout_ref[...])
    for g, base, cur in carried:
        for h_ref in (khs[g], vhs[g]):
            h_ref[1 - slot, base:base + N_REL, :] = h_ref[slot, base + cur:base + cur + N_REL, :]


def _merge_out(x, hn, branches, wc_ref, gate_col, bg_ref, wout_ref):
    merged = None
    for idx, (y, w_ref) in enumerate(branches):
        term = _gate(hn, wc_ref, bg_ref, gate_col, idx) * _dot(y, w_ref[...])
        merged = term if merged is None else merged + term
    return x + _dot(merged.astype(BF16), wout_ref[...])


def _merge_body(x_ref, ya_ref, yr_ref, ym_ref, g_ref, wc_ref, bg_ref, wao_ref, wro_ref, wmo_ref, wout_ref, o_ref):
    x = x_ref[0]
    hn = _rms(x, g_ref[...]).astype(BF16)
    o_ref[0] = _merge_out(x, hn, ((ya_ref[0], wao_ref), (yr_ref[0], wro_ref), (ym_ref[0], wmo_ref)),
                          wc_ref, 0, bg_ref, wout_ref)


def _s_mix_body(z_ref, ck0, cv0, ck1, cv1, ck2, cv2, sconv_ref, slru_ref, cmk_ref, cmv_ref,
                bc0, bc1, bc2, bn_ref, cw_ref, cb_ref, wa_ref, wi_ref, ba_ref, bi_ref, lam_ref,
                ya_ref, yrec_ref, ym_ref, ok0, ov0, ok1, ov1, ok2, ov2, oconv_ref, olru_ref,
                ext_ref, *, steps):
    cks, cvs = (ck0, ck1, ck2), (cv0, cv1, cv2)
    oks, ovs = (ok0, ok1, ok2), (ov0, ov1, ov2)
    bcs = (bc0, bc1, bc2)
    zeros = jnp.zeros((N_REL - steps, ATT_W), F32)
    new_lanes = lax.broadcasted_iota(jnp.int32, (ATT_W, LANES), 1) >= LANES - steps
    outs, lses = [], []
    for g, (window, dil) in enumerate(GROUPS):
        q = z_ref[0, :, g * ATT_W:(g + 1) * ATT_W] * ATT_SCALE
        kn = z_ref[0, :, (3 + g) * ATT_W:(4 + g) * ATT_W]
        vn = z_ref[0, :, (6 + g) * ATT_W:(7 + g) * ATT_W]
        qs = _stack_heads(q).astype(BF16)
        kt = cks[g][0]
        vt = cvs[g][0]
        s_c = _dot(qs, kt.astype(BF16)) + bcs[g][...]
        s_n = _dot_nt(qs, jnp.concatenate([kn, zeros], axis=0).astype(BF16)) + bn_ref[g]
        m = jnp.maximum(jnp.max(s_c, axis=-1, keepdims=True), jnp.max(s_n, axis=-1, keepdims=True))
        p_c = jnp.exp(s_c - m)
        p_n = jnp.exp(s_n - m)
        l = jnp.sum(p_c, axis=-1, keepdims=True) + jnp.sum(p_n, axis=-1, keepdims=True)
        o = (_dot_nt(p_c.astype(BF16), vt.astype(BF16))
             + _dot(p_n.astype(BF16), jnp.concatenate([vn, zeros], axis=0).astype(BF16))) * (1.0 / l)
        lse = jnp.broadcast_to(m + jnp.log(l), o.shape)
        outs.append(jnp.concatenate(_unstack_heads(o, steps), axis=-1))
        lses.append(jnp.concatenate(_unstack_heads(lse, steps), axis=-1))
        for old, new, dst in ((kt, kn, oks[g]), (vt, vn, ovs[g])):
            rolled = pltpu.roll(old, window - steps, 1)
            new_t = jnp.transpose(jnp.concatenate([zeros, new], axis=0))
            if window > LANES:
                dst[0, :, 0:window - LANES] = rolled[:, 0:window - LANES]
            dst[0, :, window - LANES:window] = jnp.where(new_lanes, new_t, rolled[:, window - LANES:window])
    m = jnp.maximum(jnp.maximum(lses[0], lses[1]), lses[2])
    ws = [jnp.exp(l - m) for l in lses]
    ya_ref[0] = ((ws[0] * outs[0] + ws[1] * outs[1] + ws[2] * outs[2]) / (ws[0] + ws[1] + ws[2])).astype(BF16)

    hist = CONV_W - 1
    ext_ref[0:SUBLANES, :] = jnp.zeros((SUBLANES, LRU_W), F32)
    ext_ref[SUBLANES - hist:SUBLANES, :] = sconv_ref[0]
    ext_ref[SUBLANES:SUBLANES + steps, :] = z_ref[0, :, COL_XR:COL_XR + LRU_W]
    taps = [ext_ref[SUBLANES - j:SUBLANES - j + steps, :] for j in range(CONV_W)]
    a, u = _lru_gates(_conv_taps(cw_ref, cb_ref, taps), wa_ref, wi_ref, ba_ref, bi_ref, lam_ref)
    a_c, u_c = _linear_scan(a, u, steps)
    h = a_c * slru_ref[0] + u_c
    yrec_ref[0] = (jax.nn.gelu(z_ref[0, :, COL_YR:COL_YR + LRU_W]) * h).astype(BF16)
    olru_ref[0] = h[steps - 1:steps, :]
    oconv_ref[0] = ext_ref[SUBLANES + steps - hist:SUBLANES + steps, :]

    qm = z_ref[0, :, COL_QM:COL_QM + MEM_W]
    mk = [cmk_ref[0, pl.ds(h, N_MEM, stride=MEM_HEADS), :].astype(BF16) for h in range(MEM_HEADS)]
    mv = [cmv_ref[0, pl.ds(h, N_MEM, stride=MEM_HEADS), :].astype(BF16) for h in range(MEM_HEADS)]
    ym_ref[0] = _mem_attention(qm, mk, mv).astype(BF16)


def _resident(shape):
    zeros = (0,) * len(shape)
    return pl.BlockSpec(shape, lambda *_: zeros, pipeline_mode=pl.Buffered(1))


def _params(*semantics):
    return pltpu.CompilerParams(dimension_semantics=semantics, vmem_limit_bytes=V7X_VMEM_LIMIT)


def _ffn(x, norm_g, w_gu, w_down, final_g=None):
    n = x.shape[0]
    tile = min(FFN_TILE, n)
    row = pl.BlockSpec((tile, D_MODEL), lambda i: (i, 0))
    in_specs = [row, _resident((1, D_MODEL)), _resident(w_gu.shape), _resident(w_down.shape)]
    args = [x, norm_g, w_gu, w_down]
    if final_g is not None:
        in_specs.append(_resident((1, D_MODEL)))
        args.append(final_g)
    return pl.pallas_call(
        functools.partial(_ffn_body, final_norm=final_g is not None),
        grid=(n // tile,),
        in_specs=in_specs,
        out_specs=row,
        out_shape=jax.ShapeDtypeStruct((n, D_MODEL), F32),
        scratch_shapes=[pltpu.VMEM((tile, D_FF), BF16)],
        compiler_params=_params("parallel"),
        name="ffn_final" if final_g is not None else "ffn",
    )(*args)


def _norm_mm(x, norm_g, w, row_tile, col_tile):
    n, cols = x.shape[0], w.shape[1]
    return pl.pallas_call(
        _norm_mm_body,
        grid=(n // row_tile, cols // col_tile),
        in_specs=[pl.BlockSpec((row_tile, D_MODEL), lambda i, j: (i, 0)),
                  _resident((1, D_MODEL)),
                  pl.BlockSpec((D_MODEL, col_tile), lambda i, j: (0, j))],
        out_specs=pl.BlockSpec((row_tile, col_tile), lambda i, j: (i, j)),
        out_shape=jax.ShapeDtypeStruct((n, cols), F32),
        compiler_params=_params("parallel", "parallel"),
        name="norm_mm",
    )(x, norm_g, w)


def _mem_kv(mem, norm_g, w_kv, row_tile):
    n = mem.shape[0]
    state = jax.ShapeDtypeStruct((n * MEM_HEADS, MEM_HEAD_DIM), F32)
    state_spec = pl.BlockSpec((row_tile * MEM_HEADS, MEM_HEAD_DIM), lambda i: (i, 0))
    return pl.pallas_call(
        _mem_kv_body,
        grid=(n // row_tile,),
        in_specs=[pl.BlockSpec((row_tile, D_MODEL), lambda i: (i, 0)),
                  _resident((1, D_MODEL)), _resident(w_kv.shape)],
        out_specs=[state_spec, state_spec, pl.BlockSpec((row_tile, 2 * MEM_W), lambda i: (i, 0))],
        out_shape=[state, state, jax.ShapeDtypeStruct((n, 2 * MEM_W), BF16)],
        compiler_params=_params("parallel"),
        name="mem_kv",
    )(mem, norm_g, w_kv)


def _ffn_rec(x, norm_g, w_gu, w_down, mix_g, w_xy, lru, batch, seq):
    tile = FFN_TILE
    nt = seq // tile
    n = batch * nt
    last = n - 1

    def prev(i):
        return jnp.maximum(i - 1, 0)

    weights = [norm_g, w_gu, w_down, mix_g, w_xy] + list(lru)
    return pl.pallas_call(
        functools.partial(_ffn_rec_body, tiles_per_seq=nt),
        grid=(n + 1,),
        in_specs=[pl.BlockSpec((tile, D_MODEL), lambda i: (jnp.minimum(i, last), 0))]
        + [_resident(w.shape) for w in weights],
        out_specs=[pl.BlockSpec((tile, D_MODEL), lambda i: (jnp.minimum(i, last), 0)),
                   pl.BlockSpec((1, tile, LRU_W), lambda i: (prev(i) // nt, prev(i) % nt, 0)),
                   pl.BlockSpec((1, CONV_W - 1, LRU_W), lambda i: (prev(i) // nt, 0, 0)),
                   pl.BlockSpec((1, 1, LRU_W), lambda i: (prev(i) // nt, 0, 0))],
        out_shape=[jax.ShapeDtypeStruct((n * tile, D_MODEL), F32),
                   jax.ShapeDtypeStruct((batch, seq, LRU_W), BF16),
                   jax.ShapeDtypeStruct((batch, CONV_W - 1, LRU_W), F32),
                   jax.ShapeDtypeStruct((batch, 1, LRU_W), F32)],
        scratch_shapes=[pltpu.VMEM((tile, D_FF), BF16),
                        pltpu.VMEM((2, tile, D_MODEL), F32),
                        pltpu.VMEM((LRU_CHUNKS, SCAN_PITCH * (SCAN_SEGS + 1), LANES), F32),
                        pltpu.VMEM((LRU_CHUNKS, SCAN_PITCH * SCAN_SEGS, LANES), F32),
                        pltpu.VMEM((1, LRU_W), F32)],
        compiler_params=_params("arbitrary"),
        name="ffn_rec",
    )(x, *weights)


def _p_proj(x1, norm_g, w_qkv):
    batch, seq, _ = x1.shape
    tile = PROJ_TILE
    nt = seq // tile
    qkv_shapes, qkv_specs = [], []
    for _kind in range(3):
        for window, dil in GROUPS:
            copies = HEADS if _kind == 0 else 1
            qkv_shapes.append(jax.ShapeDtypeStruct((batch, dil, copies * seq // dil, ATT_W), BF16))
            qkv_specs.append(pl.BlockSpec((1, dil, copies * tile // dil, ATT_W), lambda b, t: (b, 0, t, 0)))
    win_shapes, win_specs = [], []
    for window, dil in GROUPS:
        keep = min(window, seq)
        blk = min(keep, tile)
        first = nt - max(keep // tile, 1)
        for _kv in range(2):
            win_shapes.append(jax.ShapeDtypeStruct((batch, ATT_W, keep), F32))
            win_specs.append(pl.BlockSpec((1, ATT_W, blk),
                                          lambda b, t, first=first: (b, 0, jnp.maximum(t - first, 0))))
    return pl.pallas_call(
        _p_proj_body,
        grid=(batch, nt),
        in_specs=[pl.BlockSpec((1, tile, D_MODEL), lambda b, t: (b, t, 0)),
                  _resident((1, D_MODEL)), _resident(w_qkv.shape)],
        out_specs=qkv_specs + win_specs,
        out_shape=qkv_shapes + win_shapes,
        scratch_shapes=[pltpu.VMEM((3 * (len(GROUPS) - 1), ATT_CHUNKS, tile, LANES), F32)],
        compiler_params=_params("parallel", "arbitrary"),
        name="p_proj",
    )(x1, norm_g, w_qkv)


def _p_mix(x1, qkv, y_rec, mkv, biases, norm_g, w_c, b_gate, w_att_o, w_rec_o, w_mem_o, w_out):
    batch, seq, _ = x1.shape
    tile = MIX_TILE

    def tok(width):
        return pl.BlockSpec((1, tile, width), lambda b, t: (b, t, 0))

    q_specs, kv_specs, hist = [], [], []
    for window, dil in GROUPS:
        cur = tile // dil
        q_specs.append(pl.BlockSpec((1, dil, HEADS * cur, ATT_W), lambda b, t: (b, 0, t, 0)))
        kv_specs.append(pl.BlockSpec((1, dil, cur, ATT_W), lambda b, t: (b, 0, t, 0)))
        hist.append(pltpu.VMEM((dil * 2 * N_REL, ATT_W) if cur < N_REL else (2, dil * (N_REL + cur), ATT_W), BF16))
    weights = [norm_g, w_c, b_gate, w_att_o, w_rec_o, w_mem_o, w_out]
    in_specs = ([tok(D_MODEL)] + q_specs + kv_specs * 2
                + [tok(LRU_W), pl.BlockSpec((1, N_MEM, 2 * MEM_W), lambda b, t: (b, 0, 0))]
                + [_resident(a.shape) for a in list(biases) + weights])
    return pl.pallas_call(
        functools.partial(_p_mix_body, tiles_per_seq=seq // tile),
        grid=(batch, seq // tile),
        in_specs=in_specs,
        out_specs=tok(D_MODEL),
        out_shape=jax.ShapeDtypeStruct((batch, seq, D_MODEL), F32),
        scratch_shapes=hist * 2 + [pltpu.VMEM((len(GROUPS) * ATT_CHUNKS, tile, LANES), F32)] * 2,
        compiler_params=_params("parallel", "arbitrary"),
        name="p_mix",
    )(x1, *qkv, y_rec, mkv, *biases, *weights)


def _merge(x1, y_att, y_rec, y_mem, norm_g, w_c, b_gate, w_att_o, w_rec_o, w_mem_o, w_out):
    batch, seq, _ = x1.shape
    tile = seq

    def tok(width):
        return pl.BlockSpec((1, tile, width), lambda b, t: (b, t, 0))

    weights = [norm_g, w_c, b_gate, w_att_o, w_rec_o, w_mem_o, w_out]
    return pl.pallas_call(
        _merge_body,
        grid=(batch, seq // tile),
        in_specs=[tok(D_MODEL), tok(ATT_W), tok(LRU_W), tok(MEM_W)] + [_resident(w.shape) for w in weights],
        out_specs=tok(D_MODEL),
        out_shape=jax.ShapeDtypeStruct((batch, seq, D_MODEL), F32),
        compiler_params=_params("parallel", "parallel"),
        name="merge_s",
    )(x1, y_att, y_rec, y_mem, *weights)


def _s_mix(z, caches, state_conv, state_lru, cmk, cmv, bias_c, bias_n, lru):
    batch, steps, _ = z.shape

    def per_batch(shape):
        return pl.BlockSpec((1,) + tuple(shape[1:]), lambda b: (b,) + (0,) * (len(shape) - 1))

    ins = [z] + list(caches) + [state_conv, state_lru, cmk, cmv]
    in_specs = [per_batch(a.shape) for a in ins]
    consts = list(bias_c) + [bias_n] + list(lru)
    in_specs += [_resident(a.shape) for a in consts]
    out_shapes = [jax.ShapeDtypeStruct((batch, steps, ATT_W), BF16),
                  jax.ShapeDtypeStruct((batch, steps, LRU_W), BF16),
                  jax.ShapeDtypeStruct((batch, steps, MEM_W), BF16)]
    out_shapes += [jax.ShapeDtypeStruct(c.shape, F32) for c in caches]
    out_shapes += [jax.ShapeDtypeStruct(state_conv.shape, F32), jax.ShapeDtypeStruct(state_lru.shape, F32)]
    return pl.pallas_call(
        functools.partial(_s_mix_body, steps=steps),
        grid=(batch,),
        in_specs=in_specs,
        out_specs=[per_batch(s.shape) for s in out_shapes],
        out_shape=out_shapes,
        scratch_shapes=[pltpu.VMEM((2 * SUBLANES, LRU_W), F32)],
        compiler_params=_params("parallel"),
        name="s_mix",
    )(*ins, *consts)


def _bucket(n):
    n = np.maximum(n, 0)
    max_exact = NUM_BUCKETS // 2
    nf = np.maximum(n, 1).astype(np.float32)
    large = max_exact + (np.log(nf / max_exact) / np.float32(math.log(MAX_DISTANCE / max_exact))
                         * (NUM_BUCKETS - max_exact)).astype(np.int32)
    return np.where(n < max_exact, n, np.minimum(large, NUM_BUCKETS - 1))


def _dist_bias(rel_bias, g, dist, valid):
    tab = rel_bias[:, g * HEADS:(g + 1) * HEADS].T[:, _bucket(dist)]
    return jnp.where(valid[None], tab, -jnp.inf).astype(F32)


def _toeplitz(u, rows, cols):
    lu = rows + cols - 1
    ue = jnp.concatenate([u, u[:, :1]], axis=1)
    m = jnp.tile(ue, (1, rows))[:, :rows * lu].reshape(-1, rows, lu)
    return m[:, :, rows - 1:rows - 1 + cols].reshape(-1, cols)


def _prompt_bias(rel_bias, tile, tiles_per_seq):
    ring = 2 * N_REL
    col = np.arange(ring)
    stacks = []
    for g, (_, dil) in enumerate(GROUPS):
        cur = tile // dil
        qrows = min(cur, N_REL)
        hist = ring - qrows
        delta = np.arange(-(qrows - 1), ring)
        v = _dist_bias(rel_bias, g, dil * delta, (delta >= 0) & (delta <= N_REL))
        base = _toeplitz(v[:, ::-1], qrows, ring)
        if cur >= N_REL:
            tabs = [base, jnp.where(col >= hist, base, -jnp.inf)]
        else:
            slots = ring // cur
            tabs = []
            for t in range(min(tiles_per_seq, 2 * slots)):
                cols = []
                for s in range(slots):
                    age = (t - s) % slots
                    blk = base[:, hist - cur * age:hist - cur * age + cur]
                    cols.append(blk if t - age >= 0 else jnp.full_like(blk, -jnp.inf))
                tabs.append(jnp.concatenate(cols, axis=1))
        stacks.append(jnp.stack(tabs))
    return stacks


def _sample_bias(rel_bias, steps):
    t = np.arange(steps)[:, None]
    cache, new = [], []
    for g, (window, dil) in enumerate(GROUPS):
        dist = np.arange(window + steps)
        tab = _dist_bias(rel_bias, g, dist, (dist % dil == 0) & (dist <= window))
        cache.append(_toeplitz(tab[:, 1:][:, ::-1], steps, window))
        j = np.arange(N_REL)[None, :]
        dist = t - j
        valid = (j < steps) & (dist >= 0) & (dist % dil == 0)
        new.append(_dist_bias(rel_bias, g, dist, valid).reshape(HEADS * steps, N_REL))
    return cache, jnp.stack(new)


def _block_diag(w):
    per = MXU_TILE // LRU_BLOCK
    w4 = w.reshape(-1, per, LRU_BLOCK, LRU_BLOCK)
    bd = jnp.einsum('jnef,nm->jnemf', w4, jnp.eye(per, dtype=w.dtype))
    return bd.reshape(-1, MXU_TILE, MXU_TILE).astype(BF16)


def _to_pos_minor(c):
    return jnp.transpose(c[0], (0, 2, 3, 1)).reshape(c.shape[1], ATT_W, c.shape[2])


def _from_pos_minor(c):
    return jnp.transpose(c.reshape(c.shape[0], HEADS, HEAD_DIM, c.shape[2]), (0, 3, 1, 2))[None]


def kernel(x_prompt, x_sample, mem_prompt, cache_win_k0, cache_win_v0, cache_win_k1, cache_win_v1, cache_win_k2, cache_win_v2, state_conv, state_lru, cache_mem_k, cache_mem_v, rel_bias, ffn1_norm, ffn1_w_gu, ffn1_w_down, mix_norm, w_in, b_gate, conv_w, conv_b, lru_w_a, lru_b_a, lru_w_i, lru_b_i, lru_lambda, mem_norm, w_mem_kv, w_att_o, w_rec_o, w_mem_o, w_out, ffn2_norm, ffn2_w_gu, ffn2_w_down, final_norm):
    depth = ffn1_norm.shape[0]
    assert depth == 1
    l = 0
    batch, seq, _ = x_prompt.shape
    sbatch, steps, _ = x_sample.shape
    row = lambda v: v.reshape(1, -1)

    lru = (conv_w[l], row(conv_b[l]), _block_diag(lru_w_a[l]), _block_diag(lru_w_i[l]),
           row(lru_b_a[l]), row(lru_b_i[l]), row(lru_lambda[l]))
    ffn1 = (row(ffn1_norm[l]), ffn1_w_gu[l].astype(BF16), ffn1_w_down[l].astype(BF16))
    ffn2 = (row(ffn2_norm[l]), ffn2_w_gu[l].astype(BF16), ffn2_w_down[l].astype(BF16))
    outw = (row(b_gate[l]), w_att_o[l].astype(BF16), w_rec_o[l].astype(BF16), w_mem_o[l].astype(BF16),
            w_out[l].astype(BF16))
    mixg = row(mix_norm[l])
    fin = row(final_norm)

    xp1, y_rec, p_conv, p_lru = _ffn_rec(x_prompt.reshape(batch * seq, D_MODEL), *ffn1, mixg,
                                         w_in[l][:, COL_XR:COL_QM].astype(BF16), lru, batch, seq)
    xp1 = xp1.reshape(batch, seq, D_MODEL)
    p_mk, p_mv, mkv = _mem_kv(mem_prompt.reshape(batch * N_MEM, D_MODEL), row(mem_norm[l]),
                              w_mem_kv[l].astype(BF16), row_tile=2 * N_MEM)
    proj = _p_proj(xp1, mixg, w_in[l][:, :QKV_W].astype(BF16))
    qkv, wins = proj[0:9], proj[9:15]
    xp2 = _p_mix(xp1, qkv, y_rec, mkv.reshape(batch, N_MEM, 2 * MEM_W), _prompt_bias(rel_bias, MIX_TILE, seq // MIX_TILE), mixg,
                 w_in[l][:, COL_QM:].astype(BF16), *outw)
    y_prompt = _ffn(xp2.reshape(batch * seq, D_MODEL), *ffn2, final_g=fin).reshape(batch, seq, D_MODEL)

    n_s = sbatch * steps
    xs1 = _ffn(x_sample.reshape(n_s, D_MODEL), *ffn1)
    z = _norm_mm(xs1, mixg, w_in[l].astype(BF16), row_tile=n_s, col_tile=IN_COLS // 2)
    caches = [_to_pos_minor(c) for c in
              (cache_win_k0, cache_win_v0, cache_win_k1, cache_win_v1, cache_win_k2, cache_win_v2)]
    bias_c, bias_n = _sample_bias(rel_bias, steps)
    mem_rows = lambda c: c[l].reshape(sbatch, N_MEM * MEM_HEADS, MEM_HEAD_DIM)
    mix = _s_mix(z.reshape(sbatch, steps, IN_COLS), caches, state_conv[l], state_lru[l].reshape(sbatch, 1, LRU_W),
                 mem_rows(cache_mem_k), mem_rows(cache_mem_v), bias_c, bias_n, lru)
    s_att, s_rec, s_mem = (a.reshape(1, n_s, a.shape[-1]) for a in mix[0:3])
    xs2 = _merge(xs1.reshape(1, n_s, D_MODEL), s_att, s_rec, s_mem, mixg, w_in[l][:, COL_GATE:].astype(BF16),
                 *outw)
    y_sample = _ffn(xs2.reshape(n_s, D_MODEL), *ffn2, final_g=fin).reshape(sbatch, steps, D_MODEL)

    mem_state = lambda a: a.reshape(1, batch, N_MEM, MEM_HEADS, MEM_HEAD_DIM)
    return (y_prompt, y_sample, *[_from_pos_minor(w) for w in wins],
            p_conv[None], p_lru.reshape(1, batch, LRU_W), mem_state(p_mk), mem_state(p_mv),
            *[_from_pos_minor(w) for w in mix[3:9]], mix[9][None], mix[10].reshape(1, sbatch, LRU_W))
```

```python
import functools
import math

import numpy as np
import jax
import jax.numpy as jnp
from jax import lax
from jax.experimental import pallas as pl
from jax.experimental.pallas import tpu as pltpu

F32 = jnp.float32
BF16 = jnp.bfloat16

D_MODEL = 1024
D_FF = 2816
HEAD_DIM = 64
HEADS = 4
GROUPS = ((128, 1), (512, 4), (2048, 16))
N_REL = 128
ATT_W = HEADS * HEAD_DIM
ATT_SCALE = HEAD_DIM ** -0.5
NUM_BUCKETS = 32
MAX_DISTANCE = 2048
LRU_W = 768
LRU_BLOCK = 64
LRU_C = 8.0
CONV_W = 4
N_MEM = 256
MEM_HEADS = 4
MEM_HEAD_DIM = 128
MEM_W = MEM_HEADS * MEM_HEAD_DIM
MEM_SCALE = MEM_HEAD_DIM ** -0.5
RMS_EPS = 1e-6
QKV_W = 3 * len(GROUPS) * ATT_W
COL_XR = QKV_W
COL_YR = COL_XR + LRU_W
COL_QM = COL_YR + LRU_W
COL_GATE = COL_QM + MEM_W
IN_COLS = COL_GATE + 3 * D_MODEL

V7X_VMEM_LIMIT = 62 * 1024 * 1024
MXU_TILE = 256
SUBLANES = 8
LANES = 128
FFN_TILE = 512
FFN_CHUNK = 256
PROJ_TILE = 512
SCAN_SEGS = SUBLANES
SCAN_STEPS = FFN_TILE // SCAN_SEGS
SCAN_PITCH = SCAN_STEPS + SUBLANES
MIX_TILE = 512
ATTN_DEPTH = 8
FILL_COLS = 256
LRU_CHUNKS = LRU_W // LANES
ATT_CHUNKS = ATT_W // LANES


def _dot(a, b):
    return jnp.dot(a, b, preferred_element_type=F32)


def _dot_nt(a, b):
    return lax.dot_general(a, b, (((1,), (1,)), ((), ())), preferred_element_type=F32)


def _rms(x, g):
    return x * lax.rsqrt(jnp.mean(x * x, axis=-1, keepdims=True) + RMS_EPS) * g


def _stack_heads(q):
    lane = lax.broadcasted_iota(jnp.int32, q.shape, 1)
    return jnp.concatenate(
        [jnp.where((lane >= h * HEAD_DIM) & (lane < (h + 1) * HEAD_DIM), q, 0.0) for h in range(HEADS)], axis=0)


def _unstack_heads(x, rows):
    low = lax.broadcasted_iota(jnp.int32, (rows, LANES), 1) < HEAD_DIM
    per = LANES // HEAD_DIM
    return [jnp.where(low,
                      x[(per * c) * rows:(per * c + 1) * rows, c * LANES:(c + 1) * LANES],
                      x[(per * c + 1) * rows:(per * c + 2) * rows, c * LANES:(c + 1) * LANES])
            for c in range(ATT_CHUNKS)]


def _linear_scan(a, u, rows):
    row = lax.broadcasted_iota(jnp.int32, (rows, 1), 0)
    d = 1
    while d < rows:
        a_s = pltpu.roll(a, d, 0)
        u_s = pltpu.roll(u, d, 0)
        keep = row >= d
        u = jnp.where(keep, a * u_s + u, u)
        a = jnp.where(keep, a * a_s, a)
        d *= 2
    return a, u


def _zero_after(x):
    bits = pltpu.bitcast(x, jnp.uint32)
    half = jnp.uint32(17)
    return pltpu.bitcast(lax.shift_right_logical(lax.shift_right_logical(bits, half), half), F32)


def _lru_preact(xc, wa_ref, wi_ref):
    xcb = xc.astype(BF16)
    ra, ia = [], []
    for j in range(LRU_W // MXU_TILE):
        blk = xcb[:, j * MXU_TILE:(j + 1) * MXU_TILE]
        ra.append(_dot(blk, wa_ref[j]))
        ia.append(_dot(blk, wi_ref[j]))
    return jnp.concatenate(ra, axis=-1), jnp.concatenate(ia, axis=-1)


def _lru_decay_input(xc, ra, ia, ba_ref, bi_ref, lam_ref):
    r = jax.nn.sigmoid(ra + ba_ref[...])
    i = jax.nn.sigmoid(ia + bi_ref[...])
    nl = -lam_ref[...]
    softplus = jnp.maximum(nl, 0.0) + jnp.log1p(jnp.exp(-jnp.abs(nl)))
    log_a = (-LRU_C * r) * softplus
    a = jnp.exp(log_a)
    u = jnp.sqrt(1.0 - a * a) * (i * xc)
    return a, u


def _lru_gates(xc, wa_ref, wi_ref, ba_ref, bi_ref, lam_ref):
    return _lru_decay_input(xc, *_lru_preact(xc, wa_ref, wi_ref), ba_ref, bi_ref, lam_ref)


def _conv_taps(cw_ref, cb_ref, taps):
    xc = cb_ref[...] + cw_ref[CONV_W - 1:CONV_W, :] * taps[0]
    for j in range(1, CONV_W):
        xc = xc + cw_ref[CONV_W - 1 - j:CONV_W - j, :] * taps[j]
    return xc


def _mem_attention(qm, mk_heads, mv_heads):
    heads = []
    for h in range(MEM_HEADS):
        q = qm[:, h * MEM_HEAD_DIM:(h + 1) * MEM_HEAD_DIM].astype(BF16)
        s = _dot_nt(q, mk_heads[h]) * MEM_SCALE
        m = jnp.max(s, axis=-1, keepdims=True)
        p = jnp.exp(s - m)
        l = jnp.sum(p, axis=-1, keepdims=True)
        heads.append(_dot(p.astype(BF16), mv_heads[h]) * (1.0 / l))
    return jnp.concatenate(heads, axis=-1)


def _ffn_body(x_ref, g_ref, wgu_ref, wd_ref, *rest, final_norm):
    if final_norm:
        fg_ref, o_ref, act_ref = rest
    else:
        o_ref, act_ref = rest
    x = x_ref[...]
    hn = _rms(x, g_ref[...]).astype(BF16)
    for c in range(D_FF // FFN_CHUNK):
        lo = c * FFN_CHUNK
        gate = _dot(hn, wgu_ref[:, lo:lo + FFN_CHUNK])
        up = _dot(hn, wgu_ref[:, D_FF + lo:D_FF + lo + FFN_CHUNK])
        act_ref[:, lo:lo + FFN_CHUNK] = (gate * jax.nn.sigmoid(gate) * up).astype(BF16)
    y = x + 0.5 * _dot(act_ref[...], wd_ref[...])
    if final_norm:
        y = _rms(y, fg_ref[...])
    o_ref[...] = y


def _norm_mm_body(x_ref, g_ref, w_ref, o_ref):
    hn = _rms(x_ref[...], g_ref[...]).astype(BF16)
    o_ref[...] = _dot(hn, w_ref[...])


def _mem_kv_body(x_ref, g_ref, w_ref, mk_ref, mv_ref, mkv_ref):
    rows = x_ref.shape[0]
    z = _dot(_rms(x_ref[...], g_ref[...]).astype(BF16), w_ref[...])
    mkv_ref[...] = z.astype(BF16)
    for h in range(MEM_HEADS):
        dst = pl.ds(h, rows, stride=MEM_HEADS)
        mk_ref[dst, :] = z[:, h * MEM_HEAD_DIM:(h + 1) * MEM_HEAD_DIM]
        mv_ref[dst, :] = z[:, MEM_W + h * MEM_HEAD_DIM:MEM_W + (h + 1) * MEM_HEAD_DIM]


def _ffn_rec_body(x_ref, g_ref, wgu_ref, wd_ref, mg_ref, wxy_ref, cw_ref, cb_ref, wa_ref, wi_ref, ba_ref, bi_ref,
                  lam_ref, o_ref, yrec_ref, pconv_ref, plru_ref, act_ref, xbuf_ref, xpad_ref, hpad_ref, hc_ref,
                  *, tiles_per_seq):
    rows = FFN_TILE
    i = pl.program_id(0)
    t = lax.rem(jnp.maximum(i - 1, 0), tiles_per_seq)
    slot = lax.rem(i, 2)
    tail = slice(SCAN_STEPS - SUBLANES, SCAN_STEPS)

    @pl.when(i == 0)
    def _():
        xbuf_ref[...] = jnp.zeros(xbuf_ref.shape, F32)

    @pl.when(t == 0)
    def _():
        xpad_ref[:, tail, :] = jnp.zeros((LRU_CHUNKS, SUBLANES, LANES), F32)
        hc_ref[...] = jnp.zeros((1, LRU_W), F32)

    hn_p = _rms(xbuf_ref[1 - slot], mg_ref[...]).astype(BF16)
    xr = _dot(hn_p, wxy_ref[:, 0:LRU_W])
    for c in range(LRU_CHUNKS):
        for s in range(SCAN_SEGS):
            lo = SCAN_PITCH * (s + 1)
            xpad_ref[c, lo:lo + SCAN_STEPS, :] = xr[s * SCAN_STEPS:(s + 1) * SCAN_STEPS, c * LANES:(c + 1) * LANES]

    x = x_ref[...]
    hn = _rms(x, g_ref[...]).astype(BF16)

    def ffn_chunk(c):
        lo = c * FFN_CHUNK
        gate = _dot(hn, wgu_ref[:, lo:lo + FFN_CHUNK])
        up = _dot(hn, wgu_ref[:, D_FF + lo:D_FF + lo + FFN_CHUNK])
        act_ref[:, lo:lo + FFN_CHUNK] = (gate * jax.nn.sigmoid(gate) * up).astype(BF16)
        return gate

    n_chunks = D_FF // FFN_CHUNK
    lead = 2
    for c in range(lead):
        ffn_chunk(c)

    def step_rows(j):
        start = SCAN_PITCH + j if j >= 0 else SCAN_STEPS + j
        sel = pl.ds(start, SCAN_SEGS, stride=SCAN_PITCH)
        return jnp.concatenate([xpad_ref[c, sel, :] for c in range(LRU_CHUNKS)], axis=-1)

    hist = CONV_W - 1
    xp = jnp.concatenate([step_rows(j) for j in range(-hist, SCAN_STEPS)], axis=0)
    n = SCAN_STEPS * SCAN_SEGS
    taps = [xp[(hist - j) * SCAN_SEGS:(hist - j) * SCAN_SEGS + n, :] for j in range(CONV_W)]
    xc = _conv_taps(cw_ref, cb_ref, taps)
    ra, ia = _lru_preact(xc, wa_ref, wi_ref)

    yr = _dot(hn_p, wxy_ref[:, LRU_W:2 * LRU_W])
    a_parts, u_parts = [], []
    per = n // SCAN_SEGS
    for k in range(SCAN_SEGS):
        gate = ffn_chunk(lead + k)
        tie = jnp.concatenate([_zero_after(gate[0:per, 0:LANES])] * LRU_CHUNKS, axis=-1)
        sl = slice(k * per, (k + 1) * per)
        a_k, u_k = _lru_decay_input(xc[sl] + tie, ra[sl], ia[sl], ba_ref, bi_ref, lam_ref)
        a_parts.append(a_k)
        u_parts.append(u_k)
    a = jnp.concatenate(a_parts, axis=0)
    u = jnp.concatenate(u_parts, axis=0)
    for c in range(lead + SCAN_SEGS, n_chunks):
        ffn_chunk(c)

    def rows_of(v, j):
        return v[j * SCAN_SEGS:(j + 1) * SCAN_SEGS, :]

    h = jnp.zeros((SCAN_SEGS, LRU_W), F32)
    prod = jnp.ones((SCAN_SEGS, LRU_W), F32)
    for j in range(SCAN_STEPS):
        a_j = rows_of(a, j)
        h = a_j * h + rows_of(u, j)
        prod = a_j * prod
    prod_c, h_c = _linear_scan(prod, h, SCAN_SEGS)
    h_end = prod_c * hc_ref[...] + h_c
    seg = lax.broadcasted_iota(jnp.int32, (SCAN_SEGS, 1), 0)
    h = jnp.where(seg == 0, hc_ref[...], pltpu.roll(h_end, 1, 0))
    for j in range(SCAN_STEPS):
        h = rows_of(a, j) * h + rows_of(u, j)
        dst = pl.ds(j, SCAN_SEGS, stride=SCAN_PITCH)
        for c in range(LRU_CHUNKS):
            hpad_ref[c, dst, :] = h[:, c * LANES:(c + 1) * LANES]
    hs = jnp.concatenate(
        [jnp.concatenate([hpad_ref[c, SCAN_PITCH * s:SCAN_PITCH * s + SCAN_STEPS, :] for c in range(LRU_CHUNKS)],
                         axis=-1) for s in range(SCAN_SEGS)], axis=0)
    yrec_ref[0] = (jax.nn.gelu(yr) * hs).astype(BF16)
    hc_ref[...] = h_end[SCAN_SEGS - 1:SCAN_SEGS, :]
    plru_ref[0] = h_end[SCAN_SEGS - 1:SCAN_SEGS, :]
    pconv_ref[0] = xr[rows - hist:rows, :]
    for c in range(LRU_CHUNKS):
        xpad_ref[c, tail, :] = xr[rows - SUBLANES:rows, c * LANES:(c + 1) * LANES]

    y = x + 0.5 * _dot(act_ref[...], wd_ref[...])
    o_ref[...] = y
    xbuf_ref[slot] = y


def _p_proj_body(x_ref, g_ref, w_ref, q0, q1, q2, k0, k1, k2, v0, v1, v2, pk0, pv0, pk1, pv1, pk2, pv2, zs_ref):
    rows = PROJ_TILE
    hn = _rms(x_ref[0], g_ref[...]).astype(BF16)
    outs = ((q0, q1, q2), (k0, k1, k2), (v0, v1, v2))
    wins = (None, (pk0, pk1, pk2), (pv0, pv1, pv2))
    for g, (window, dil) in reversed(list(enumerate(GROUPS))):
        for kind in (1, 2, 0):
            col = (kind * len(GROUPS) + g) * ATT_W
            z = _dot(hn, w_ref[:, col:col + ATT_W])
            if kind == 0:
                z = z * ATT_SCALE
            else:
                keep = min(window, rows)
                wins[kind][g][0] = jnp.transpose(z[rows - keep:, :])
            def emit(r, part):
                if kind == 0:
                    qrows = min(rows // dil, N_REL)
                    part = jnp.concatenate([_stack_heads(part[lo:lo + qrows, :])
                                            for lo in range(0, rows // dil, qrows)], axis=0)
                outs[kind][g][0, r] = part.astype(BF16)

            if dil == 1:
                emit(0, z)
            else:
                stage = zs_ref.at[kind * (len(GROUPS) - 1) + g - 1]
                for c in range(ATT_CHUNKS):
                    stage[c] = z[:, c * LANES:(c + 1) * LANES]
                for r in range(dil):
                    sel = pl.ds(r, rows // dil, stride=dil)
                    emit(r, jnp.concatenate([stage[c, sel, :] for c in range(ATT_CHUNKS)], axis=-1))


def _softmax_parts(s):
    m = jnp.max(s, axis=-1, keepdims=True)
    p = jnp.exp(s - m)
    return p.astype(BF16), jnp.sum(p, axis=-1, keepdims=True), m


def _gate(hn, wc_ref, bg_ref, gate_col, idx):
    lo = idx * D_MODEL
    return jax.nn.sigmoid(_dot(hn, wc_ref[:, gate_col + lo:gate_col + lo + D_MODEL]) + bg_ref[:, lo:lo + D_MODEL])


def _p_mix_body(x_ref, q0, q1, q2, k0, k1, k2, v0, v1, v2, yr_ref, mkv_ref, b0, b1, b2,
                g_ref, wc_ref, bg_ref, wao_ref, wro_ref, wmo_ref, wout_ref,
                o_ref, kh0, kh1, kh2, vh0, vh1, vh2, o_scr, l_scr, *, tiles_per_seq):
    tile = MIX_TILE
    t = pl.program_id(1)
    slot = lax.rem(t, 2)
    qs_, ks_, vs_ = (q0, q1, q2), (k0, k1, k2), (v0, v1, v2)
    khs, vhs, biases = (kh0, kh1, kh2), (vh0, vh1, vh2), (b0, b1, b2)
    ring = 2 * N_REL

    def in_ring(g):
        return tile // GROUPS[g][1] < N_REL

    @pl.when(t == 0)
    def _():
        for g, (window, dil) in enumerate(GROUPS):
            for h_ref in (khs[g], vhs[g]):
                if in_ring(g):
                    h_ref[...] = jnp.zeros(h_ref.shape, BF16)
                else:
                    cur = tile // dil
                    for r in range(dil):
                        h_ref[0, r * (N_REL + cur):r * (N_REL + cur) + N_REL, :] = jnp.zeros((N_REL, ATT_W), BF16)

    res = {}

    windows, carried = [], []

    def window_block(g, dil, r, lo, qrows, keys, tab, new_rows):
        def rows_of(h_ref, sel):
            return h_ref.at[sel] if in_ring(g) else h_ref.at[slot, sel]

        def scores():
            if new_rows is not None:
                rows_of(khs[g], new_rows)[...] = ks_[g][0, r]
                rows_of(vhs[g], new_rows)[...] = vs_[g][0, r]
            q = qs_[g][0, r, HEADS * lo:HEADS * (lo + qrows), :]
            return _softmax_parts(_dot_nt(q, rows_of(khs[g], keys)[...]) + tab)

        def values(parts):
            p, l, m = parts
            o = _dot(p, rows_of(vhs[g], keys)[...]) * (1.0 / l)
            lse = jnp.broadcast_to(m + jnp.log(l), o.shape)
            o_c, l_c = _unstack_heads(o, qrows), _unstack_heads(lse, qrows)
            dst = slice(lo, lo + qrows) if dil == 1 else pl.ds(lo * dil + r, qrows, stride=dil)
            for c in range(ATT_CHUNKS):
                o_scr[g * ATT_CHUNKS + c, dst, :] = o_c[c]
                l_scr[g * ATT_CHUNKS + c, dst, :] = l_c[c]
        return scores, values

    for g, (window, dil) in enumerate(GROUPS):
        cur = tile // dil
        if in_ring(g):
            slots = ring // cur
            tabs = biases[g].shape[0]
            tab = biases[g][t if tabs >= tiles_per_seq else jnp.where(t < slots, t, slots + lax.rem(t, slots))]
            newest = lax.rem(t, slots) * cur
            for r in range(dil):
                base = r * ring
                windows.append(window_block(g, dil, r, 0, cur, slice(base, base + ring), tab,
                                            pl.ds(pl.multiple_of(base + newest, cur), cur)))
        else:
            for r in range(dil):
                base = r * (N_REL + cur)
                carried.append((g, base, cur))
                for j in range(cur // N_REL):
                    tab = biases[g][jnp.where(t == 0, 1, 0)] if j == 0 else biases[g][0]
                    lo = j * N_REL
                    windows.append(window_block(g, dil, r, lo, N_REL, slice(base + lo, base + lo + ring), tab,
                                                slice(base + N_REL, base + N_REL + cur) if j == 0 else None))

    def memory_block(h):
        sl = slice(h * MEM_HEAD_DIM, (h + 1) * MEM_HEAD_DIM)

        def scores():
            return _softmax_parts(_dot_nt(res["qm"][:, sl].astype(BF16), mkv_ref[0, :, sl]) * MEM_SCALE)

        def values(parts):
            p, l, _ = parts
            res["mem", h] = _dot(p, mkv_ref[0, :, MEM_W + h * MEM_HEAD_DIM:MEM_W + (h + 1) * MEM_HEAD_DIM]) * (1.0 / l)
        return scores, values

    def normed_input():
        res["x"] = x_ref[0]
        res["hn"] = _rms(res["x"], g_ref[...]).astype(BF16)
        res["qm"] = _dot(res["hn"], wc_ref[:, 0:MEM_W])

    def gate_piece(idx, piece):
        def emit():
            lo = idx * D_MODEL + piece * FILL_COLS
            cols = slice(piece * FILL_COLS, (piece + 1) * FILL_COLS)
            gate = jax.nn.sigmoid(_dot(res["hn"], wc_ref[:, MEM_W + lo:MEM_W + lo + FILL_COLS])
                                  + bg_ref[:, lo:lo + FILL_COLS])
            res["gate", idx, piece] = gate * _dot(yr_ref[0], wro_ref[:, cols]) if idx == 1 else gate
        return emit

    normed_input()
    blocks = windows + [memory_block(h) for h in range(MEM_HEADS)]
    pieces = D_MODEL // FILL_COLS
    fillers = [gate_piece(idx, piece) for idx in (1, 0, 2) for piece in range(pieces)]
    every = max(1, len(blocks) // len(fillers))

    pending = []
    for i, (scores, values) in enumerate(blocks):
        pending.append((values, scores()))
        if len(pending) > ATTN_DEPTH:
            done, parts = pending.pop(0)
            done(parts)
        if fillers and (i + 1) % every == 0:
            fillers.pop(0)()
    for done, parts in pending:
        done(parts)
    for emit in fillers:
        emit()
    gates = [jnp.concatenate([res["gate", idx, piece] for piece in range(pieces)], axis=-1) for idx in range(3)]

    mixed = []
    for c in range(ATT_CHUNKS):
        l0, l1, l2 = (l_scr[g * ATT_CHUNKS + c] for g in range(len(GROUPS)))
        o0, o1, o2 = (o_scr[g * ATT_CHUNKS + c] for g in range(len(GROUPS)))
        m = jnp.maximum(jnp.maximum(l0, l1), l2)
        w0, w1, w2 = jnp.exp(l0 - m), jnp.exp(l1 - m), jnp.exp(l2 - m)
        mixed.append((w0 * o0 + w1 * o1 + w2 * o2) / (w0 + w1 + w2))
    ya = jnp.concatenate(mixed, axis=-1).astype(BF16)
    ym = jnp.concatenate([res["mem", h] for h in range(MEM_HEADS)], axis=-1).astype(BF16)
    merged = gates[0] * _dot(ya, wao_ref[...]) + gates[1] + gates[2] * _dot(ym, wmo_ref[...])
    o_ref[0] = res["x"] + _dot(merged.astype(BF16), wout_ref[...])
    for g, base, cur in carried:
        for h_ref in (khs[g], vhs[g]):
            h_ref[1 - slot, base:base + N_REL, :] = h_ref[slot, base + cur:base + cur + N_REL, :]


def _merge_out(x, hn, branches, wc_ref, gate_col, bg_ref, wout_ref):
    merged = None
    for idx, (y, w_ref) in enumerate(branches):
        term = _gate(hn, wc_ref, bg_ref, gate_col, idx) * _dot(y, w_ref[...])
        merged = term if merged is None else merged + term
    return x + _dot(merged.astype(BF16), wout_ref[...])


def _merge_body(x_ref, ya_ref, yr_ref, ym_ref, g_ref, wc_ref, bg_ref, wao_ref, wro_ref, wmo_ref, wout_ref, o_ref):
    x = x_ref[0]
    hn = _rms(x, g_ref[...]).astype(BF16)
    o_ref[0] = _merge_out(x, hn, ((ya_ref[0], wao_ref), (yr_ref[0], wro_ref), (ym_ref[0], wmo_ref)),
                          wc_ref, 0, bg_ref, wout_ref)


def _s_mix_body(z_ref, ck0, cv0, ck1, cv1, ck2, cv2, sconv_ref, slru_ref, cmk_ref, cmv_ref,
                bc0, bc1, bc2, bn_ref, cw_ref, cb_ref, wa_ref, wi_ref, ba_ref, bi_ref, lam_ref,
                ya_ref, yrec_ref, ym_ref, ok0, ov0, ok1, ov1, ok2, ov2, oconv_ref, olru_ref,
                ext_ref, *, steps):
    cks, cvs = (ck0, ck1, ck2), (cv0, cv1, cv2)
    oks, ovs = (ok0, ok1, ok2), (ov0, ov1, ov2)
    bcs = (bc0, bc1, bc2)
    zeros = jnp.zeros((N_REL - steps, ATT_W), F32)
    new_lanes = lax.broadcasted_iota(jnp.int32, (ATT_W, LANES), 1) >= LANES - steps
    outs, lses = [], []
    for g, (window, dil) in enumerate(GROUPS):
        q = z_ref[0, :, g * ATT_W:(g + 1) * ATT_W] * ATT_SCALE
        kn = z_ref[0, :, (3 + g) * ATT_W:(4 + g) * ATT_W]
        vn = z_ref[0, :, (6 + g) * ATT_W:(7 + g) * ATT_W]
        qs = _stack_heads(q).astype(BF16)
        kt = cks[g][0]
        vt = cvs[g][0]
        s_c = _dot(qs, kt.astype(BF16)) + bcs[g][...]
        s_n = _dot_nt(qs, jnp.concatenate([kn, zeros], axis=0).astype(BF16)) + bn_ref[g]
        m = jnp.maximum(jnp.max(s_c, axis=-1, keepdims=True), jnp.max(s_n, axis=-1, keepdims=True))
        p_c = jnp.exp(s_c - m)
        p_n = jnp.exp(s_n - m)
        l = jnp.sum(p_c, axis=-1, keepdims=True) + jnp.sum(p_n, axis=-1, keepdims=True)
        o = (_dot_nt(p_c.astype(BF16), vt.astype(BF16))
             + _dot(p_n.astype(BF16), jnp.concatenate([vn, zeros], axis=0).astype(BF16))) * (1.0 / l)
        lse = jnp.broadcast_to(m + jnp.log(l), o.shape)
        outs.append(jnp.concatenate(_unstack_heads(o, steps), axis=-1))
        lses.append(jnp.concatenate(_unstack_heads(lse, steps), axis=-1))
        for old, new, dst in ((kt, kn, oks[g]), (vt, vn, ovs[g])):
            rolled = pltpu.roll(old, window - steps, 1)
            new_t = jnp.transpose(jnp.concatenate([zeros, new], axis=0))
            if window > LANES:
                dst[0, :, 0:window - LANES] = rolled[:, 0:window - LANES]
            dst[0, :, window - LANES:window] = jnp.where(new_lanes, new_t, rolled[:, window - LANES:window])
    m = jnp.maximum(jnp.maximum(lses[0], lses[1]), lses[2])
    ws = [jnp.exp(l - m) for l in lses]
    ya_ref[0] = ((ws[0] * outs[0] + ws[1] * outs[1] + ws[2] * outs[2]) / (ws[0] + ws[1] + ws[2])).astype(BF16)

    hist = CONV_W - 1
    ext_ref[0:SUBLANES, :] = jnp.zeros((SUBLANES, LRU_W), F32)
    ext_ref[SUBLANES - hist:SUBLANES, :] = sconv_ref[0]
    ext_ref[SUBLANES:SUBLANES + steps, :] = z_ref[0, :, COL_XR:COL_XR + LRU_W]
    taps = [ext_ref[SUBLANES - j:SUBLANES - j + steps, :] for j in range(CONV_W)]
    a, u = _lru_gates(_conv_taps(cw_ref, cb_ref, taps), wa_ref, wi_ref, ba_ref, bi_ref, lam_ref)
    a_c, u_c = _linear_scan(a, u, steps)
    h = a_c * slru_ref[0] + u_c
    yrec_ref[0] = (jax.nn.gelu(z_ref[0, :, COL_YR:COL_YR + LRU_W]) * h).astype(BF16)
    olru_ref[0] = h[steps - 1:steps, :]
    oconv_ref[0] = ext_ref[SUBLANES + steps - hist:SUBLANES + steps, :]

    qm = z_ref[0, :, COL_QM:COL_QM + MEM_W]
    mk = [cmk_ref[0, pl.ds(h, N_MEM, stride=MEM_HEADS), :].astype(BF16) for h in range(MEM_HEADS)]
    mv = [cmv_ref[0, pl.ds(h, N_MEM, stride=MEM_HEADS), :].astype(BF16) for h in range(MEM_HEADS)]
    ym_ref[0] = _mem_attention(qm, mk, mv).astype(BF16)


def _resident(shape):
    zeros = (0,) * len(shape)
    return pl.BlockSpec(shape, lambda *_: zeros, pipeline_mode=pl.Buffered(1))


def _params(*semantics):
    return pltpu.CompilerParams(dimension_semantics=semantics, vmem_limit_bytes=V7X_VMEM_LIMIT)


def _ffn(x, norm_g, w_gu, w_down, final_g=None):
    n = x.shape[0]
    tile = min(FFN_TILE, n)
    row = pl.BlockSpec((tile, D_MODEL), lambda i: (i, 0))
    in_specs = [row, _resident((1, D_MODEL)), _resident(w_gu.shape), _resident(w_down.shape)]
    args = [x, norm_g, w_gu, w_down]
    if final_g is not None:
        in_specs.append(_resident((1, D_MODEL)))
        args.append(final_g)
    return pl.pallas_call(
        functools.partial(_ffn_body, final_norm=final_g is not None),
        grid=(n // tile,),
        in_specs=in_specs,
        out_specs=row,
        out_shape=jax.ShapeDtypeStruct((n, D_MODEL), F32),
        scratch_shapes=[pltpu.VMEM((tile, D_FF), BF16)],
        compiler_params=_params("parallel"),
        name="ffn_final" if final_g is not None else "ffn",
    )(*args)


def _norm_mm(x, norm_g, w, row_tile, col_tile):
    n, cols = x.shape[0], w.shape[1]
    return pl.pallas_call(
        _norm_mm_body,
        grid=(n // row_tile, cols // col_tile),
        in_specs=[pl.BlockSpec((row_tile, D_MODEL), lambda i, j: (i, 0)),
                  _resident((1, D_MODEL)),
                  pl.BlockSpec((D_MODEL, col_tile), lambda i, j: (0, j))],
        out_specs=pl.BlockSpec((row_tile, col_tile), lambda i, j: (i, j)),
        out_shape=jax.ShapeDtypeStruct((n, cols), F32),
        compiler_params=_params("parallel", "parallel"),
        name="norm_mm",
    )(x, norm_g, w)


def _mem_kv(mem, norm_g, w_kv, row_tile):
    n = mem.shape[0]
    state = jax.ShapeDtypeStruct((n * MEM_HEADS, MEM_HEAD_DIM), F32)
    state_spec = pl.BlockSpec((row_tile * MEM_HEADS, MEM_HEAD_DIM), lambda i: (i, 0))
    return pl.pallas_call(
        _mem_kv_body,
        grid=(n // row_tile,),
        in_specs=[pl.BlockSpec((row_tile, D_MODEL), lambda i: (i, 0)),
                  _resident((1, D_MODEL)), _resident(w_kv.shape)],
        out_specs=[state_spec, state_spec, pl.BlockSpec((row_tile, 2 * MEM_W), lambda i: (i, 0))],
        out_shape=[state, state, jax.ShapeDtypeStruct((n, 2 * MEM_W), BF16)],
        compiler_params=_params("parallel"),
        name="mem_kv",
    )(mem, norm_g, w_kv)


def _ffn_rec(x, norm_g, w_gu, w_down, mix_g, w_xy, lru, batch, seq):
    tile = FFN_TILE
    nt = seq // tile
    n = batch * nt
    last = n - 1

    def prev(i):
        return jnp.maximum(i - 1, 0)

    weights = [norm_g, w_gu, w_down, mix_g, w_xy] + list(lru)
    return pl.pallas_call(
        functools.partial(_ffn_rec_body, tiles_per_seq=nt),
        grid=(n + 1,),
        in_specs=[pl.BlockSpec((tile, D_MODEL), lambda i: (jnp.minimum(i, last), 0))]
        + [_resident(w.shape) for w in weights],
        out_specs=[pl.BlockSpec((tile, D_MODEL), lambda i: (jnp.minimum(i, last), 0)),
                   pl.BlockSpec((1, tile, LRU_W), lambda i: (prev(i) // nt, prev(i) % nt, 0)),
                   pl.BlockSpec((1, CONV_W - 1, LRU_W), lambda i: (prev(i) // nt, 0, 0)),
                   pl.BlockSpec((1, 1, LRU_W), lambda i: (prev(i) // nt, 0, 0))],
        out_shape=[jax.ShapeDtypeStruct((n * tile, D_MODEL), F32),
                   jax.ShapeDtypeStruct((batch, seq, LRU_W), BF16),
                   jax.ShapeDtypeStruct((batch, CONV_W - 1, LRU_W), F32),
                   jax.ShapeDtypeStruct((batch, 1, LRU_W), F32)],
        scratch_shapes=[pltpu.VMEM((tile, D_FF), BF16),
                        pltpu.VMEM((2, tile, D_MODEL), F32),
                        pltpu.VMEM((LRU_CHUNKS, SCAN_PITCH * (SCAN_SEGS + 1), LANES), F32),
                        pltpu.VMEM((LRU_CHUNKS, SCAN_PITCH * SCAN_SEGS, LANES), F32),
                        pltpu.VMEM((1, LRU_W), F32)],
        compiler_params=_params("arbitrary"),
        name="ffn_rec",
    )(x, *weights)


def _p_proj(x1, norm_g, w_qkv):
    batch, seq, _ = x1.shape
    tile = PROJ_TILE
    nt = seq // tile
    qkv_shapes, qkv_specs = [], []
    for _kind in range(3):
        for window, dil in GROUPS:
            copies = HEADS if _kind == 0 else 1
            qkv_shapes.append(jax.ShapeDtypeStruct((batch, dil, copies * seq // dil, ATT_W), BF16))
            qkv_specs.append(pl.BlockSpec((1, dil, copies * tile // dil, ATT_W), lambda b, t: (b, 0, t, 0)))
    win_shapes, win_specs = [], []
    for window, dil in GROUPS:
        keep = min(window, seq)
        blk = min(keep, tile)
        first = nt - max(keep // tile, 1)
        for _kv in range(2):
            win_shapes.append(jax.ShapeDtypeStruct((batch, ATT_W, keep), F32))
            win_specs.append(pl.BlockSpec((1, ATT_W, blk),
                                          lambda b, t, first=first: (b, 0, jnp.maximum(t - first, 0))))
    return pl.pallas_call(
        _p_proj_body,
        grid=(batch, nt),
        in_specs=[pl.BlockSpec((1, tile, D_MODEL), lambda b, t: (b, t, 0)),
                  _resident((1, D_MODEL)), _resident(w_qkv.shape)],
        out_specs=qkv_specs + win_specs,
        out_shape=qkv_shapes + win_shapes,
        scratch_shapes=[pltpu.VMEM((3 * (len(GROUPS) - 1), ATT_CHUNKS, tile, LANES), F32)],
        compiler_params=_params("parallel", "arbitrary"),
        name="p_proj",
    )(x1, norm_g, w_qkv)


def _p_mix(x1, qkv, y_rec, mkv, biases, norm_g, w_c, b_gate, w_att_o, w_rec_o, w_mem_o, w_out):
    batch, seq, _ = x1.shape
    tile = MIX_TILE

    def tok(width):
        return pl.BlockSpec((1, tile, width), lambda b, t: (b, t, 0))

    q_specs, kv_specs, hist = [], [], []
    for window, dil in GROUPS:
        cur = tile // dil
        q_specs.append(pl.BlockSpec((1, dil, HEADS * cur, ATT_W), lambda b, t: (b, 0, t, 0)))
        kv_specs.append(pl.BlockSpec((1, dil, cur, ATT_W), lambda b, t: (b, 0, t, 0)))
        hist.append(pltpu.VMEM((dil * 2 * N_REL, ATT_W) if cur < N_REL else (2, dil * (N_REL + cur), ATT_W), BF16))
    weights = [norm_g, w_c, b_gate, w_att_o, w_rec_o, w_mem_o, w_out]
    in_specs = ([tok(D_MODEL)] + q_specs + kv_specs * 2
                + [tok(LRU_W), pl.BlockSpec((1, N_MEM, 2 * MEM_W), lambda b, t: (b, 0, 0))]
                + [_resident(a.shape) for a in list(biases) + weights])
    return pl.pallas_call(
        functools.partial(_p_mix_body, tiles_per_seq=seq // tile),
        grid=(batch, seq // tile),
        in_specs=in_specs,
        out_specs=tok(D_MODEL),
        out_shape=jax.ShapeDtypeStruct((batch, seq, D_MODEL), F32),
        scratch_shapes=hist * 2 + [pltpu.VMEM((len(GROUPS) * ATT_CHUNKS, tile, LANES), F32)] * 2,
        compiler_params=_params("parallel", "arbitrary"),
        name="p_mix",
    )(x1, *qkv, y_rec, mkv, *biases, *weights)


def _merge(x1, y_att, y_rec, y_mem, norm_g, w_c, b_gate, w_att_o, w_rec_o, w_mem_o, w_out):
    batch, seq, _ = x1.shape
    tile = seq

    def tok(width):
        return pl.BlockSpec((1, tile, width), lambda b, t: (b, t, 0))

    weights = [norm_g, w_c, b_gate, w_att_o, w_rec_o, w_mem_o, w_out]
    return pl.pallas_call(
        _merge_body,
        grid=(batch, seq // tile),
        in_specs=[tok(D_MODEL), tok(ATT_W), tok(LRU_W), tok(MEM_W)] + [_resident(w.shape) for w in weights],
        out_specs=tok(D_MODEL),
        out_shape=jax.ShapeDtypeStruct((batch, seq, D_MODEL), F32),
        compiler_params=_params("parallel", "parallel"),
        name="merge_s",
    )(x1, y_att, y_rec, y_mem, *weights)


def _s_mix(z, caches, state_conv, state_lru, cmk, cmv, bias_c, bias_n, lru):
    batch, steps, _ = z.shape

    def per_batch(shape):
        return pl.BlockSpec((1,) + tuple(shape[1:]), lambda b: (b,) + (0,) * (len(shape) - 1))

    ins = [z] + list(caches) + [state_conv, state_lru, cmk, cmv]
    in_specs = [per_batch(a.shape) for a in ins]
    consts = list(bias_c) + [bias_n] + list(lru)
    in_specs += [_resident(a.shape) for a in consts]
    out_shapes = [jax.ShapeDtypeStruct((batch, steps, ATT_W), BF16),
                  jax.ShapeDtypeStruct((batch, steps, LRU_W), BF16),
                  jax.ShapeDtypeStruct((batch, steps, MEM_W), BF16)]
    out_shapes += [jax.ShapeDtypeStruct(c.shape, F32) for c in caches]
    out_shapes += [jax.ShapeDtypeStruct(state_conv.shape, F32), jax.ShapeDtypeStruct(state_lru.shape, F32)]
    return pl.pallas_call(
        functools.partial(_s_mix_body, steps=steps),
        grid=(batch,),
        in_specs=in_specs,
        out_specs=[per_batch(s.shape) for s in out_shapes],
        out_shape=out_shapes,
        scratch_shapes=[pltpu.VMEM((2 * SUBLANES, LRU_W), F32)],
        compiler_params=_params("parallel"),
        name="s_mix",
    )(*ins, *consts)


def _bucket(n):
    n = np.maximum(n, 0)
    max_exact = NUM_BUCKETS // 2
    nf = np.maximum(n, 1).astype(np.float32)
    large = max_exact + (np.log(nf / max_exact) / np.float32(math.log(MAX_DISTANCE / max_exact))
                         * (NUM_BUCKETS - max_exact)).astype(np.int32)
    return np.where(n < max_exact, n, np.minimum(large, NUM_BUCKETS - 1))


def _dist_bias(rel_bias, g, dist, valid):
    tab = rel_bias[:, g * HEADS:(g + 1) * HEADS].T[:, _bucket(dist)]
    return jnp.where(valid[None], tab, -jnp.inf).astype(F32)


def _toeplitz(u, rows, cols):
    lu = rows + cols - 1
    ue = jnp.concatenate([u, u[:, :1]], axis=1)
    m = jnp.tile(ue, (1, rows))[:, :rows * lu].reshape(-1, rows, lu)
    return m[:, :, rows - 1:rows - 1 + cols].reshape(-1, cols)


def _prompt_bias(rel_bias, tile, tiles_per_seq):
    ring = 2 * N_REL
    col = np.arange(ring)
    stacks = []
    for g, (_, dil) in enumerate(GROUPS):
        cur = tile // dil
        qrows = min(cur, N_REL)
        hist = ring - qrows
        delta = np.arange(-(qrows - 1), ring)
        v = _dist_bias(rel_bias, g, dil * delta, (delta >= 0) & (delta <= N_REL))
        base = _toeplitz(v[:, ::-1], qrows, ring)
        if cur >= N_REL:
            tabs = [base, jnp.where(col >= hist, base, -jnp.inf)]
        else:
            slots = ring // cur
            tabs = []
            for t in range(min(tiles_per_seq, 2 * slots)):
                cols = []
                for s in range(slots):
                    age = (t - s) % slots
                    blk = base[:, hist - cur * age:hist - cur * age + cur]
                    cols.append(blk if t - age >= 0 else jnp.full_like(blk, -jnp.inf))
                tabs.append(jnp.concatenate(cols, axis=1))
        stacks.append(jnp.stack(tabs))
    return stacks


def _sample_bias(rel_bias, steps):
    t = np.arange(steps)[:, None]
    cache, new = [], []
    for g, (window, dil) in enumerate(GROUPS):
        dist = np.arange(window + steps)
        tab = _dist_bias(rel_bias, g, dist, (dist % dil == 0) & (dist <= window))
        cache.append(_toeplitz(tab[:, 1:][:, ::-1], steps, window))
        j = np.arange(N_REL)[None, :]
        dist = t - j
        valid = (j < steps) & (dist >= 0) & (dist % dil == 0)
        new.append(_dist_bias(rel_bias, g, dist, valid).reshape(HEADS * steps, N_REL))
    return cache, jnp.stack(new)


def _block_diag(w):
    per = MXU_TILE // LRU_BLOCK
    w4 = w.reshape(-1, per, LRU_BLOCK, LRU_BLOCK)
    bd = jnp.einsum('jnef,nm->jnemf', w4, jnp.eye(per, dtype=w.dtype))
    return bd.reshape(-1, MXU_TILE, MXU_TILE).astype(BF16)


def _to_pos_minor(c):
    return jnp.transpose(c[0], (0, 2, 3, 1)).reshape(c.shape[1], ATT_W, c.shape[2])


def _from_pos_minor(c):
    return jnp.transpose(c.reshape(c.shape[0], HEADS, HEAD_DIM, c.shape[2]), (0, 3, 1, 2))[None]


def kernel(x_prompt, x_sample, mem_prompt, cache_win_k0, cache_win_v0, cache_win_k1, cache_win_v1, cache_win_k2, cache_win_v2, state_conv, state_lru, cache_mem_k, cache_mem_v, rel_bias, ffn1_norm, ffn1_w_gu, ffn1_w_down, mix_norm, w_in, b_gate, conv_w, conv_b, lru_w_a, lru_b_a, lru_w_i, lru_b_i, lru_lambda, mem_norm, w_mem_kv, w_att_o, w_rec_o, w_mem_o, w_out, ffn2_norm, ffn2_w_gu, ffn2_w_down, final_norm):
    depth = ffn1_norm.shape[0]
    assert depth == 1
    l = 0
    batch, seq, _ = x_prompt.shape
    sbatch, steps, _ = x_sample.shape
    row = lambda v: v.reshape(1, -1)

    lru = (conv_w[l], row(conv_b[l]), _block_diag(lru_w_a[l]), _block_diag(lru_w_i[l]),
           row(lru_b_a[l]), row(lru_b_i[l]), row(lru_lambda[l]))
    ffn1 = (row(ffn1_norm[l]), ffn1_w_gu[l].astype(BF16), ffn1_w_down[l].astype(BF16))
    ffn2 = (row(ffn2_norm[l]), ffn2_w_gu[l].astype(BF16), ffn2_w_down[l].astype(BF16))
    outw = (row(b_gate[l]), w_att_o[l].astype(BF16), w_rec_o[l].astype(BF16), w_mem_o[l].astype(BF16),
            w_out[l].astype(BF16))
    mixg = row(mix_norm[l])
    fin = row(final_norm)

    xp1, y_rec, p_conv, p_lru = _ffn_rec(x_prompt.reshape(batch * seq, D_MODEL), *ffn1, mixg,
                                         w_in[l][:, COL_XR:COL_QM].astype(BF16), lru, batch, seq)
    xp1 = xp1.reshape(batch, seq, D_MODEL)
    p_mk, p_mv, mkv = _mem_kv(mem_prompt.reshape(batch * N_MEM, D_MODEL), row(mem_norm[l]),
                              w_mem_kv[l].astype(BF16), row_tile=2 * N_MEM)
    proj = _p_proj(xp1, mixg, w_in[l][:, :QKV_W].astype(BF16))
    qkv, wins = proj[0:9], proj[9:15]
    xp2 = _p_mix(xp1, qkv, y_rec, mkv.reshape(batch, N_MEM, 2 * MEM_W), _prompt_bias(rel_bias, MIX_TILE, seq // MIX_TILE), mixg,
                 w_in[l][:, COL_QM:].astype(BF16), *outw)
    y_prompt = _ffn(xp2.reshape(batch * seq, D_MODEL), *ffn2, final_g=fin).reshape(batch, seq, D_MODEL)

    n_s = sbatch * steps
    xs1 = _ffn(x_sample.reshape(n_s, D_MODEL), *ffn1)
    z = _norm_mm(xs1, mixg, w_in[l].astype(BF16), row_tile=n_s, col_tile=IN_COLS // 2)
    caches = [_to_pos_minor(c) for c in
              (cache_win_k0, cache_win_v0, cache_win_k1, cache_win_v1, cache_win_k2, cache_win_v2)]
    bias_c, bias_n = _sample_bias(rel_bias, steps)
    mem_rows = lambda c: c[l].reshape(sbatch, N_MEM * MEM_HEADS, MEM_HEAD_DIM)
    mix = _s_mix(z.reshape(sbatch, steps, IN_COLS), caches, state_conv[l], state_lru[l].reshape(sbatch, 1, LRU_W),
                 mem_rows(cache_mem_k), mem_rows(cache_mem_v), bias_c, bias_n, lru)
    s_att, s_rec, s_mem = (a.reshape(1, n_s, a.shape[-1]) for a in mix[0:3])
    xs2 = _merge(xs1.reshape(1, n_s, D_MODEL), s_att, s_rec, s_mem, mixg, w_in[l][:, COL_GATE:].astype(BF16),
                 *outw)
    y_sample = _ffn(xs2.reshape(n_s, D_MODEL), *ffn2, final_g=fin).reshape(sbatch, steps, D_MODEL)

    mem_state = lambda a: a.reshape(1, batch, N_MEM, MEM_HEADS, MEM_HEAD_DIM)
    return (y_prompt, y_sample, *[_from_pos_minor(w) for w in wins],
            p_conv[None], p_lru.reshape(1, batch, LRU_W), mem_state(p_mk), mem_state(p_mv),
            *[_from_pos_minor(w) for w in mix[3:9]], mix[9][None], mix[10].reshape(1, sbatch, LRU_W))
```

```python
import functools
import math

import numpy as np
import jax
import jax.numpy as jnp
from jax import lax
from jax.experimental import pallas as pl
from jax.experimental.pallas import tpu as pltpu

F32 = jnp.float32
BF16 = jnp.bfloat16

D_MODEL = 1024
D_FF = 2816
HEAD_DIM = 64
HEADS = 4
GROUPS = ((128, 1), (512, 4), (2048, 16))
N_REL = 128
ATT_W = HEADS * HEAD_DIM
ATT_SCALE = HEAD_DIM ** -0.5
NUM_BUCKETS = 32
MAX_DISTANCE = 2048
LRU_W = 768
LRU_BLOCK = 64
LRU_C = 8.0
CONV_W = 4
N_MEM = 256
MEM_HEADS = 4
MEM_HEAD_DIM = 128
MEM_W = MEM_HEADS * MEM_HEAD_DIM
MEM_SCALE = MEM_HEAD_DIM ** -0.5
RMS_EPS = 1e-6
QKV_W = 3 * len(GROUPS) * ATT_W
COL_XR = QKV_W
COL_YR = COL_XR + LRU_W
COL_QM = COL_YR + LRU_W
COL_GATE = COL_QM + MEM_W
IN_COLS = COL_GATE + 3 * D_MODEL

V7X_VMEM_LIMIT = 62 * 1024 * 1024
MXU_TILE = 256
SUBLANES = 8
LANES = 128
FFN_TILE = 512
FFN_CHUNK = 256
PROJ_TILE = 512
SCAN_SEGS = SUBLANES
SCAN_STEPS = FFN_TILE // SCAN_SEGS
SCAN_PITCH = SCAN_STEPS + SUBLANES
MIX_TILE = 512
ATTN_DEPTH = 8
FILL_COLS = 256
LRU_CHUNKS = LRU_W // LANES
ATT_CHUNKS = ATT_W // LANES


def _dot(a, b):
    return jnp.dot(a, b, preferred_element_type=F32)


def _dot_nt(a, b):
    return lax.dot_general(a, b, (((1,), (1,)), ((), ())), preferred_element_type=F32)


def _rms(x, g):
    return x * lax.rsqrt(jnp.mean(x * x, axis=-1, keepdims=True) + RMS_EPS) * g


def _stack_heads(q):
    lane = lax.broadcasted_iota(jnp.int32, q.shape, 1)
    return jnp.concatenate(
        [jnp.where((lane >= h * HEAD_DIM) & (lane < (h + 1) * HEAD_DIM), q, 0.0) for h in range(HEADS)], axis=0)


def _unstack_heads(x, rows):
    low = lax.broadcasted_iota(jnp.int32, (rows, LANES), 1) < HEAD_DIM
    per = LANES // HEAD_DIM
    return [jnp.where(low,
                      x[(per * c) * rows:(per * c + 1) * rows, c * LANES:(c + 1) * LANES],
                      x[(per * c + 1) * rows:(per * c + 2) * rows, c * LANES:(c + 1) * LANES])
            for c in range(ATT_CHUNKS)]


def _linear_scan(a, u, rows):
    row = lax.broadcasted_iota(jnp.int32, (rows, 1), 0)
    d = 1
    while d < rows:
        a_s = pltpu.roll(a, d, 0)
        u_s = pltpu.roll(u, d, 0)
        keep = row >= d
        u = jnp.where(keep, a * u_s + u, u)
        a = jnp.where(keep, a * a_s, a)
        d *= 2
    return a, u


def _zero_after(x):
    bits = pltpu.bitcast(x, jnp.uint32)
    half = jnp.uint32(17)
    return pltpu.bitcast(lax.shift_right_logical(lax.shift_right_logical(bits, half), half), F32)


def _lru_preact(xc, wa_ref, wi_ref):
    xcb = xc.astype(BF16)
    ra, ia = [], []
    for j in range(LRU_W // MXU_TILE):
        blk = xcb[:, j * MXU_TILE:(j + 1) * MXU_TILE]
        ra.append(_dot(blk, wa_ref[j]))
        ia.append(_dot(blk, wi_ref[j]))
    return jnp.concatenate(ra, axis=-1), jnp.concatenate(ia, axis=-1)


def _lru_decay_input(xc, ra, ia, ba_ref, bi_ref, lam_ref):
    r = jax.nn.sigmoid(ra + ba_ref[...])
    i = jax.nn.sigmoid(ia + bi_ref[...])
    nl = -lam_ref[...]
    softplus = jnp.maximum(nl, 0.0) + jnp.log1p(jnp.exp(-jnp.abs(nl)))
    log_a = (-LRU_C * r) * softplus
    a = jnp.exp(log_a)
    u = jnp.sqrt(1.0 - a * a) * (i * xc)
    return a, u


def _lru_gates(xc, wa_ref, wi_ref, ba_ref, bi_ref, lam_ref):
    return _lru_decay_input(xc, *_lru_preact(xc, wa_ref, wi_ref), ba_ref, bi_ref, lam_ref)


def _conv_taps(cw_ref, cb_ref, taps):
    xc = cb_ref[...] + cw_ref[CONV_W - 1:CONV_W, :] * taps[0]
    for j in range(1, CONV_W):
        xc = xc + cw_ref[CONV_W - 1 - j:CONV_W - j, :] * taps[j]
    return xc


def _mem_attention(qm, mk_heads, mv_heads):
    heads = []
    for h in range(MEM_HEADS):
        q = qm[:, h * MEM_HEAD_DIM:(h + 1) * MEM_HEAD_DIM].astype(BF16)
        s = _dot_nt(q, mk_heads[h]) * MEM_SCALE
        m = jnp.max(s, axis=-1, keepdims=True)
        p = jnp.exp(s - m)
        l = jnp.sum(p, axis=-1, keepdims=True)
        heads.append(_dot(p.astype(BF16), mv_heads[h]) * (1.0 / l))
    return jnp.concatenate(heads, axis=-1)


def _ffn_body(x_ref, g_ref, wgu_ref, wd_ref, *rest, final_norm):
    if final_norm:
        fg_ref, o_ref, act_ref = rest
    else:
        o_ref, act_ref = rest
    x = x_ref[...]
    hn = _rms(x, g_ref[...]).astype(BF16)
    for c in range(D_FF // FFN_CHUNK):
        lo = c * FFN_CHUNK
        gate = _dot(hn, wgu_ref[:, lo:lo + FFN_CHUNK])
        up = _dot(hn, wgu_ref[:, D_FF + lo:D_FF + lo + FFN_CHUNK])
        act_ref[:, lo:lo + FFN_CHUNK] = (gate * jax.nn.sigmoid(gate) * up).astype(BF16)
    y = x + 0.5 * _dot(act_ref[...], wd_ref[...])
    if final_norm:
        y = _rms(y, fg_ref[...])
    o_ref[...] = y


def _norm_mm_body(x_ref, g_ref, w_ref, o_ref):
    hn = _rms(x_ref[...], g_ref[...]).astype(BF16)
    o_ref[...] = _dot(hn, w_ref[...])


def _mem_kv_body(x_ref, g_ref, w_ref, mk_ref, mv_ref, mkv_ref):
    rows = x_ref.shape[0]
    z = _dot(_rms(x_ref[...], g_ref[...]).astype(BF16), w_ref[...])
    mkv_ref[...] = z.astype(BF16)
    for h in range(MEM_HEADS):
        dst = pl.ds(h, rows, stride=MEM_HEADS)
        mk_ref[dst, :] = z[:, h * MEM_HEAD_DIM:(h + 1) * MEM_HEAD_DIM]
        mv_ref[dst, :] = z[:, MEM_W + h * MEM_HEAD_DIM:MEM_W + (h + 1) * MEM_HEAD_DIM]


def _ffn_rec_body(x_ref, g_ref, wgu_ref, wd_ref, mg_ref, wxr_ref, wyr_ref, cw_ref, cb_ref, wa_ref, wi_ref, ba_ref, bi_ref,
                  lam_ref, o_ref, yrec_ref, pconv_ref, plru_ref, act_ref, xbuf_ref, xpad_ref, hpad_ref, hc_ref,
                  *, tiles_per_seq):
    rows = FFN_TILE
    i = pl.program_id(0)
    t = lax.rem(jnp.maximum(i - 1, 0), tiles_per_seq)
    slot = lax.rem(i, 2)
    tail = slice(SCAN_STEPS - SUBLANES, SCAN_STEPS)

    @pl.when(i == 0)
    def _():
        xbuf_ref[...] = jnp.zeros(xbuf_ref.shape, F32)

    @pl.when(t == 0)
    def _():
        xpad_ref[:, tail, :] = jnp.zeros((LRU_CHUNKS, SUBLANES, LANES), F32)
        hc_ref[...] = jnp.zeros((1, LRU_W), F32)

    hn_p = _rms(xbuf_ref[1 - slot], mg_ref[...]).astype(BF16)
    xr = _dot(hn_p, wxr_ref[...])
    for c in range(LRU_CHUNKS):
        for s in range(SCAN_SEGS):
            lo = SCAN_PITCH * (s + 1)
            xpad_ref[c, lo:lo + SCAN_STEPS, :] = xr[s * SCAN_STEPS:(s + 1) * SCAN_STEPS, c * LANES:(c + 1) * LANES]

    x = x_ref[...]
    hn = _rms(x, g_ref[...]).astype(BF16)

    def ffn_chunk(c):
        lo = c * FFN_CHUNK
        gate = _dot(hn, wgu_ref[:, lo:lo + FFN_CHUNK])
        up = _dot(hn, wgu_ref[:, D_FF + lo:D_FF + lo + FFN_CHUNK])
        act_ref[:, lo:lo + FFN_CHUNK] = (gate * jax.nn.sigmoid(gate) * up).astype(BF16)
        return gate

    n_chunks = D_FF // FFN_CHUNK
    lead = 2
    for c in range(lead):
        ffn_chunk(c)

    def step_rows(j):
        start = SCAN_PITCH + j if j >= 0 else SCAN_STEPS + j
        sel = pl.ds(start, SCAN_SEGS, stride=SCAN_PITCH)
        return jnp.concatenate([xpad_ref[c, sel, :] for c in range(LRU_CHUNKS)], axis=-1)

    hist = CONV_W - 1
    xp = jnp.concatenate([step_rows(j) for j in range(-hist, SCAN_STEPS)], axis=0)
    n = SCAN_STEPS * SCAN_SEGS
    taps = [xp[(hist - j) * SCAN_SEGS:(hist - j) * SCAN_SEGS + n, :] for j in range(CONV_W)]
    xc = _conv_taps(cw_ref, cb_ref, taps)
    ra, ia = _lru_preact(xc, wa_ref, wi_ref)

    yr = _dot(hn_p, wyr_ref[...])
    a_parts, u_parts = [], []
    per = n // SCAN_SEGS
    for k in range(SCAN_SEGS):
        gate = ffn_chunk(lead + k)
        tie = jnp.concatenate([_zero_after(gate[0:per, 0:LANES])] * LRU_CHUNKS, axis=-1)
        sl = slice(k * per, (k + 1) * per)
        a_k, u_k = _lru_decay_input(xc[sl] + tie, ra[sl], ia[sl], ba_ref, bi_ref, lam_ref)
        a_parts.append(a_k)
        u_parts.append(u_k)
    a = jnp.concatenate(a_parts, axis=0)
    u = jnp.concatenate(u_parts, axis=0)
    for c in range(lead + SCAN_SEGS, n_chunks):
        ffn_chunk(c)

    def rows_of(v, j):
        return v[j * SCAN_SEGS:(j + 1) * SCAN_SEGS, :]

    h = jnp.zeros((SCAN_SEGS, LRU_W), F32)
    prod = jnp.ones((SCAN_SEGS, LRU_W), F32)
    for j in range(SCAN_STEPS):
        a_j = rows_of(a, j)
        h = a_j * h + rows_of(u, j)
        prod = a_j * prod
    prod_c, h_c = _linear_scan(prod, h, SCAN_SEGS)
    h_end = prod_c * hc_ref[...] + h_c
    seg = lax.broadcasted_iota(jnp.int32, (SCAN_SEGS, 1), 0)
    h = jnp.where(seg == 0, hc_ref[...], pltpu.roll(h_end, 1, 0))
    for j in range(SCAN_STEPS):
        h = rows_of(a, j) * h + rows_of(u, j)
        dst = pl.ds(j, SCAN_SEGS, stride=SCAN_PITCH)
        for c in range(LRU_CHUNKS):
            hpad_ref[c, dst, :] = h[:, c * LANES:(c + 1) * LANES]
    hs = jnp.concatenate(
        [jnp.concatenate([hpad_ref[c, SCAN_PITCH * s:SCAN_PITCH * s + SCAN_STEPS, :] for c in range(LRU_CHUNKS)],
                         axis=-1) for s in range(SCAN_SEGS)], axis=0)
    yrec_ref[0] = (jax.nn.gelu(yr) * hs).astype(BF16)
    hc_ref[...] = h_end[SCAN_SEGS - 1:SCAN_SEGS, :]
    plru_ref[0] = h_end[SCAN_SEGS - 1:SCAN_SEGS, :]
    pconv_ref[0] = xr[rows - hist:rows, :]
    for c in range(LRU_CHUNKS):
        xpad_ref[c, tail, :] = xr[rows - SUBLANES:rows, c * LANES:(c + 1) * LANES]

    y = x + 0.5 * _dot(act_ref[...], wd_ref[...])
    o_ref[...] = y
    xbuf_ref[slot] = y


def _p_proj_body(x_ref, g_ref, w_ref, q0, q1, q2, k0, k1, k2, v0, v1, v2, pk0, pv0, pk1, pv1, pk2, pv2, zs_ref):
    rows = PROJ_TILE
    hn = _rms(x_ref[0], g_ref[...]).astype(BF16)
    outs = ((q0, q1, q2), (k0, k1, k2), (v0, v1, v2))
    wins = (None, (pk0, pk1, pk2), (pv0, pv1, pv2))
    for g, (window, dil) in reversed(list(enumerate(GROUPS))):
        for kind in (1, 2, 0):
            col = (kind * len(GROUPS) + g) * ATT_W
            z = _dot(hn, w_ref[:, col:col + ATT_W])
            if kind == 0:
                z = z * ATT_SCALE
            else:
                keep = min(window, rows)
                wins[kind][g][0] = jnp.transpose(z[rows - keep:, :])
            def emit(r, part):
                if kind == 0:
                    qrows = min(rows // dil, N_REL)
                    part = jnp.concatenate([_stack_heads(part[lo:lo + qrows, :])
                                            for lo in range(0, rows // dil, qrows)], axis=0)
                outs[kind][g][0, r] = part.astype(BF16)

            if dil == 1:
                emit(0, z)
            else:
                stage = zs_ref.at[kind * (len(GROUPS) - 1) + g - 1]
                for c in range(ATT_CHUNKS):
                    stage[c] = z[:, c * LANES:(c + 1) * LANES]
                for r in range(dil):
                    sel = pl.ds(r, rows // dil, stride=dil)
                    emit(r, jnp.concatenate([stage[c, sel, :] for c in range(ATT_CHUNKS)], axis=-1))


def _softmax_parts(s):
    m = jnp.max(s, axis=-1, keepdims=True)
    p = jnp.exp(s - m)
    return p.astype(BF16), jnp.sum(p, axis=-1, keepdims=True), m


def _gate(hn, wc_ref, bg_ref, gate_col, idx):
    lo = idx * D_MODEL
    return jax.nn.sigmoid(_dot(hn, wc_ref[:, gate_col + lo:gate_col + lo + D_MODEL]) + bg_ref[:, lo:lo + D_MODEL])


def _p_mix_body(x_ref, q0, q1, q2, k0, k1, k2, v0, v1, v2, yr_ref, mkv_ref, b0, b1, b2, g_ref, *rest,
                tiles_per_seq):
    n_win = (IN_COLS - COL_QM) // FILL_COLS
    wc_refs = rest[:n_win]
    bg_ref, wao_ref, wro_ref, wmo_ref, wout_ref, o_ref, kh0, kh1, kh2, vh0, vh1, vh2, o_scr, l_scr = rest[n_win:]
    qm_wins = MEM_W // FILL_COLS
    tile = MIX_TILE
    t = pl.program_id(1)
    slot = lax.rem(t, 2)
    qs_, ks_, vs_ = (q0, q1, q2), (k0, k1, k2), (v0, v1, v2)
    khs, vhs, biases = (kh0, kh1, kh2), (vh0, vh1, vh2), (b0, b1, b2)
    ring = 2 * N_REL

    def in_ring(g):
        return tile // GROUPS[g][1] < N_REL

    @pl.when(t == 0)
    def _():
        for g, (window, dil) in enumerate(GROUPS):
            for h_ref in (khs[g], vhs[g]):
                if in_ring(g):
                    h_ref[...] = jnp.zeros(h_ref.shape, BF16)
                else:
                    cur = tile // dil
                    for r in range(dil):
                        h_ref[0, r * (N_REL + cur):r * (N_REL + cur) + N_REL, :] = jnp.zeros((N_REL, ATT_W), BF16)

    res = {}

    windows, carried = [], []

    def window_block(g, dil, r, lo, qrows, keys, tab, new_rows):
        def rows_of(h_ref, sel):
            return h_ref.at[sel] if in_ring(g) else h_ref.at[slot, sel]

        def scores():
            if new_rows is not None:
                rows_of(khs[g], new_rows)[...] = ks_[g][0, r]
                rows_of(vhs[g], new_rows)[...] = vs_[g][0, r]
            q = qs_[g][0, r, HEADS * lo:HEADS * (lo + qrows), :]
            return _softmax_parts(_dot_nt(q, rows_of(khs[g], keys)[...]) + tab)

        def values(parts):
            p, l, m = parts
            o = _dot(p, rows_of(vhs[g], keys)[...]) * (1.0 / l)
            lse = jnp.broadcast_to(m + jnp.log(l), o.shape)
            o_c, l_c = _unstack_heads(o, qrows), _unstack_heads(lse, qrows)
            dst = slice(lo, lo + qrows) if dil == 1 else pl.ds(lo * dil + r, qrows, stride=dil)
            for c in range(ATT_CHUNKS):
                o_scr[g * ATT_CHUNKS + c, dst, :] = o_c[c]
                l_scr[g * ATT_CHUNKS + c, dst, :] = l_c[c]
        return scores, values

    for g, (window, dil) in enumerate(GROUPS):
        cur = tile // dil
        if in_ring(g):
            slots = ring // cur
            tabs = biases[g].shape[0]
            tab = biases[g][t if tabs >= tiles_per_seq else jnp.where(t < slots, t, slots + lax.rem(t, slots))]
            newest = lax.rem(t, slots) * cur
            for r in range(dil):
                base = r * ring
                windows.append(window_block(g, dil, r, 0, cur, slice(base, base + ring), tab,
                                            pl.ds(pl.multiple_of(base + newest, cur), cur)))
        else:
            for r in range(dil):
                base = r * (N_REL + cur)
                carried.append((g, base, cur))
                for j in range(cur // N_REL):
                    tab = biases[g][jnp.where(t == 0, 1, 0)] if j == 0 else biases[g][0]
                    lo = j * N_REL
                    windows.append(window_block(g, dil, r, lo, N_REL, slice(base + lo, base + lo + ring), tab,
                                                slice(base + N_REL, base + N_REL + cur) if j == 0 else None))

    def memory_block(h):
        sl = slice(h * MEM_HEAD_DIM, (h + 1) * MEM_HEAD_DIM)

        def scores():
            return _softmax_parts(_dot_nt(res["qm"][:, sl].astype(BF16), mkv_ref[0, :, sl]) * MEM_SCALE)

        def values(parts):
            p, l, _ = parts
            res["mem", h] = _dot(p, mkv_ref[0, :, MEM_W + h * MEM_HEAD_DIM:MEM_W + (h + 1) * MEM_HEAD_DIM]) * (1.0 / l)
        return scores, values

    def normed_input():
        res["x"] = x_ref[0]
        res["hn"] = _rms(res["x"], g_ref[...]).astype(BF16)
        res["qm"] = jnp.concatenate([_dot(res["hn"], w[...]) for w in wc_refs[:qm_wins]], axis=-1)

    def gate_piece(idx, piece):
        def emit():
            lo = idx * D_MODEL + piece * FILL_COLS
            cols = slice(piece * FILL_COLS, (piece + 1) * FILL_COLS)
            gate = jax.nn.sigmoid(_dot(res["hn"], wc_refs[qm_wins + lo // FILL_COLS][...])
                                  + bg_ref[:, lo:lo + FILL_COLS])
            res["gate", idx, piece] = gate * _dot(yr_ref[0], wro_ref[:, cols]) if idx == 1 else gate
        return emit

    normed_input()
    blocks = windows + [memory_block(h) for h in range(MEM_HEADS)]
    pieces = D_MODEL // FILL_COLS
    fillers = [gate_piece(idx, piece) for idx in (1, 0, 2) for piece in range(pieces)]
    every = max(1, len(blocks) // len(fillers))

    pending = []
    for i, (scores, values) in enumerate(blocks):
        pending.append((values, scores()))
        if len(pending) > ATTN_DEPTH:
            done, parts = pending.pop(0)
            done(parts)
        if fillers and (i + 1) % every == 0:
            fillers.pop(0)()
    for done, parts in pending:
        done(parts)
    for emit in fillers:
        emit()
    gates = [jnp.concatenate([res["gate", idx, piece] for piece in range(pieces)], axis=-1) for idx in range(3)]

    mixed = []
    for c in range(ATT_CHUNKS):
        l0, l1, l2 = (l_scr[g * ATT_CHUNKS + c] for g in range(len(GROUPS)))
        o0, o1, o2 = (o_scr[g * ATT_CHUNKS + c] for g in range(len(GROUPS)))
        m = jnp.maximum(jnp.maximum(l0, l1), l2)
        w0, w1, w2 = jnp.exp(l0 - m), jnp.exp(l1 - m), jnp.exp(l2 - m)
        mixed.append((w0 * o0 + w1 * o1 + w2 * o2) / (w0 + w1 + w2))
    ya = jnp.concatenate(mixed, axis=-1).astype(BF16)
    ym = jnp.concatenate([res["mem", h] for h in range(MEM_HEADS)], axis=-1).astype(BF16)
    merged = gates[0] * _dot(ya, wao_ref[...]) + gates[1] + gates[2] * _dot(ym, wmo_ref[...])
    o_ref[0] = res["x"] + _dot(merged.astype(BF16), wout_ref[...])
    for g, base, cur in carried:
        for h_ref in (khs[g], vhs[g]):
            h_ref[1 - slot, base:base + N_REL, :] = h_ref[slot, base + cur:base + cur + N_REL, :]


def _merge_out(x, hn, branches, wc_ref, gate_col, bg_ref, wout_ref):
    merged = None
    for idx, (y, w_ref) in enumerate(branches):
        term = _gate(hn, wc_ref, bg_ref, gate_col, idx) * _dot(y, w_ref[...])
        merged = term if merged is None else merged + term
    return x + _dot(merged.astype(BF16), wout_ref[...])


def _merge_body(x_ref, ya_ref, yr_ref, ym_ref, g_ref, wc_ref, bg_ref, wao_ref, wro_ref, wmo_ref, wout_ref, o_ref):
    x = x_ref[0]
    hn = _rms(x, g_ref[...]).astype(BF16)
    o_ref[0] = _merge_out(x, hn, ((ya_ref[0], wao_ref), (yr_ref[0], wro_ref), (ym_ref[0], wmo_ref)),
                          wc_ref, 0, bg_ref, wout_ref)


def _s_mix_body(z_ref, ck0, cv0, ck1, cv1, ck2, cv2, sconv_ref, slru_ref, cmk_ref, cmv_ref,
                bc0, bc1, bc2, bn_ref, cw_ref, cb_ref, wa_ref, wi_ref, ba_ref, bi_ref, lam_ref,
                ya_ref, yrec_ref, ym_ref, ok0, ov0, ok1, ov1, ok2, ov2, oconv_ref, olru_ref,
                ext_ref, *, steps):
    cks, cvs = (ck0, ck1, ck2), (cv0, cv1, cv2)
    oks, ovs = (ok0, ok1, ok2), (ov0, ov1, ov2)
    bcs = (bc0, bc1, bc2)
    zeros = jnp.zeros((N_REL - steps, ATT_W), F32)
    new_lanes = lax.broadcasted_iota(jnp.int32, (ATT_W, LANES), 1) >= LANES - steps
    outs, lses = [], []
    for g, (window, dil) in enumerate(GROUPS):
        q = z_ref[0, :, g * ATT_W:(g + 1) * ATT_W] * ATT_SCALE
        kn = z_ref[0, :, (3 + g) * ATT_W:(4 + g) * ATT_W]
        vn = z_ref[0, :, (6 + g) * ATT_W:(7 + g) * ATT_W]
        qs = _stack_heads(q).astype(BF16)
        kt = cks[g][0]
        vt = cvs[g][0]
        s_c = _dot(qs, kt.astype(BF16)) + bcs[g][...]
        s_n = _dot_nt(qs, jnp.concatenate([kn, zeros], axis=0).astype(BF16)) + bn_ref[g]
        m = jnp.maximum(jnp.max(s_c, axis=-1, keepdims=True), jnp.max(s_n, axis=-1, keepdims=True))
        p_c = jnp.exp(s_c - m)
        p_n = jnp.exp(s_n - m)
        l = jnp.sum(p_c, axis=-1, keepdims=True) + jnp.sum(p_n, axis=-1, keepdims=True)
        o = (_dot_nt(p_c.astype(BF16), vt.astype(BF16))
             + _dot(p_n.astype(BF16), jnp.concatenate([vn, zeros], axis=0).astype(BF16))) * (1.0 / l)
        lse = jnp.broadcast_to(m + jnp.log(l), o.shape)
        outs.append(jnp.concatenate(_unstack_heads(o, steps), axis=-1))
        lses.append(jnp.concatenate(_unstack_heads(lse, steps), axis=-1))
        for old, new, dst in ((kt, kn, oks[g]), (vt, vn, ovs[g])):
            rolled = pltpu.roll(old, window - steps, 1)
            new_t = jnp.transpose(jnp.concatenate([zeros, new], axis=0))
            if window > LANES:
                dst[0, :, 0:window - LANES] = rolled[:, 0:window - LANES]
            dst[0, :, window - LANES:window] = jnp.where(new_lanes, new_t, rolled[:, window - LANES:window])
    m = jnp.maximum(jnp.maximum(lses[0], lses[1]), lses[2])
    ws = [jnp.exp(l - m) for l in lses]
    ya_ref[0] = ((ws[0] * outs[0] + ws[1] * outs[1] + ws[2] * outs[2]) / (ws[0] + ws[1] + ws[2])).astype(BF16)

    hist = CONV_W - 1
    ext_ref[0:SUBLANES, :] = jnp.zeros((SUBLANES, LRU_W), F32)
    ext_ref[SUBLANES - hist:SUBLANES, :] = sconv_ref[0]
    ext_ref[SUBLANES:SUBLANES + steps, :] = z_ref[0, :, COL_XR:COL_XR + LRU_W]
    taps = [ext_ref[SUBLANES - j:SUBLANES - j + steps, :] for j in range(CONV_W)]
    a, u = _lru_gates(_conv_taps(cw_ref, cb_ref, taps), wa_ref, wi_ref, ba_ref, bi_ref, lam_ref)
    a_c, u_c = _linear_scan(a, u, steps)
    h = a_c * slru_ref[0] + u_c
    yrec_ref[0] = (jax.nn.gelu(z_ref[0, :, COL_YR:COL_YR + LRU_W]) * h).astype(BF16)
    olru_ref[0] = h[steps - 1:steps, :]
    oconv_ref[0] = ext_ref[SUBLANES + steps - hist:SUBLANES + steps, :]

    qm = z_ref[0, :, COL_QM:COL_QM + MEM_W]
    mk = [cmk_ref[0, pl.ds(h, N_MEM, stride=MEM_HEADS), :].astype(BF16) for h in range(MEM_HEADS)]
    mv = [cmv_ref[0, pl.ds(h, N_MEM, stride=MEM_HEADS), :].astype(BF16) for h in range(MEM_HEADS)]
    ym_ref[0] = _mem_attention(qm, mk, mv).astype(BF16)


def _resident(shape):
    zeros = (0,) * len(shape)
    return pl.BlockSpec(shape, lambda *_: zeros, pipeline_mode=pl.Buffered(1))


def _col_window(rows, first_col, cols):
    assert first_col % cols == 0
    block = first_col // cols
    return pl.BlockSpec((rows, cols), lambda *_: (0, block), pipeline_mode=pl.Buffered(1))


def _params(*semantics):
    return pltpu.CompilerParams(dimension_semantics=semantics, vmem_limit_bytes=V7X_VMEM_LIMIT)


def _ffn(x, norm_g, w_gu, w_down, final_g=None):
    n = x.shape[0]
    tile = min(FFN_TILE, n)
    row = pl.BlockSpec((tile, D_MODEL), lambda i: (i, 0))
    in_specs = [row, _resident((1, D_MODEL)), _resident(w_gu.shape), _resident(w_down.shape)]
    args = [x, norm_g, w_gu, w_down]
    if final_g is not None:
        in_specs.append(_resident((1, D_MODEL)))
        args.append(final_g)
    return pl.pallas_call(
        functools.partial(_ffn_body, final_norm=final_g is not None),
        grid=(n // tile,),
        in_specs=in_specs,
        out_specs=row,
        out_shape=jax.ShapeDtypeStruct((n, D_MODEL), F32),
        scratch_shapes=[pltpu.VMEM((tile, D_FF), BF16)],
        compiler_params=_params("parallel"),
        name="ffn_final" if final_g is not None else "ffn",
    )(*args)


def _norm_mm(x, norm_g, w, row_tile, col_tile):
    n, cols = x.shape[0], w.shape[1]
    return pl.pallas_call(
        _norm_mm_body,
        grid=(n // row_tile, cols // col_tile),
        in_specs=[pl.BlockSpec((row_tile, D_MODEL), lambda i, j: (i, 0)),
                  _resident((1, D_MODEL)),
                  pl.BlockSpec((D_MODEL, col_tile), lambda i, j: (0, j))],
        out_specs=pl.BlockSpec((row_tile, col_tile), lambda i, j: (i, j)),
        out_shape=jax.ShapeDtypeStruct((n, cols), F32),
        compiler_params=_params("parallel", "parallel"),
        name="norm_mm",
    )(x, norm_g, w)


def _mem_kv(mem, norm_g, w_kv, row_tile):
    n = mem.shape[0]
    state = jax.ShapeDtypeStruct((n * MEM_HEADS, MEM_HEAD_DIM), F32)
    state_spec = pl.BlockSpec((row_tile * MEM_HEADS, MEM_HEAD_DIM), lambda i: (i, 0))
    return pl.pallas_call(
        _mem_kv_body,
        grid=(n // row_tile,),
        in_specs=[pl.BlockSpec((row_tile, D_MODEL), lambda i: (i, 0)),
                  _resident((1, D_MODEL)), _resident(w_kv.shape)],
        out_specs=[state_spec, state_spec, pl.BlockSpec((row_tile, 2 * MEM_W), lambda i: (i, 0))],
        out_shape=[state, state, jax.ShapeDtypeStruct((n, 2 * MEM_W), BF16)],
        compiler_params=_params("parallel"),
        name="mem_kv",
    )(mem, norm_g, w_kv)


def _ffn_rec(x, norm_g, w_gu, w_down, mix_g, w_in, lru, batch, seq):
    tile = FFN_TILE
    nt = seq // tile
    n = batch * nt
    last = n - 1

    def prev(i):
        return jnp.maximum(i - 1, 0)

    ffn_w = [norm_g, w_gu, w_down, mix_g]
    weights = ffn_w + [w_in, w_in] + list(lru)
    return pl.pallas_call(
        functools.partial(_ffn_rec_body, tiles_per_seq=nt),
        grid=(n + 1,),
        in_specs=[pl.BlockSpec((tile, D_MODEL), lambda i: (jnp.minimum(i, last), 0))]
        + [_resident(w.shape) for w in ffn_w]
        + [_col_window(D_MODEL, COL_XR, LRU_W), _col_window(D_MODEL, COL_YR, LRU_W)]
        + [_resident(w.shape) for w in lru],
        out_specs=[pl.BlockSpec((tile, D_MODEL), lambda i: (jnp.minimum(i, last), 0)),
                   pl.BlockSpec((1, tile, LRU_W), lambda i: (prev(i) // nt, prev(i) % nt, 0)),
                   pl.BlockSpec((1, CONV_W - 1, LRU_W), lambda i: (prev(i) // nt, 0, 0)),
                   pl.BlockSpec((1, 1, LRU_W), lambda i: (prev(i) // nt, 0, 0))],
        out_shape=[jax.ShapeDtypeStruct((n * tile, D_MODEL), F32),
                   jax.ShapeDtypeStruct((batch, seq, LRU_W), BF16),
                   jax.ShapeDtypeStruct((batch, CONV_W - 1, LRU_W), F32),
                   jax.ShapeDtypeStruct((batch, 1, LRU_W), F32)],
        scratch_shapes=[pltpu.VMEM((tile, D_FF), BF16),
                        pltpu.VMEM((2, tile, D_MODEL), F32),
                        pltpu.VMEM((LRU_CHUNKS, SCAN_PITCH * (SCAN_SEGS + 1), LANES), F32),
                        pltpu.VMEM((LRU_CHUNKS, SCAN_PITCH * SCAN_SEGS, LANES), F32),
                        pltpu.VMEM((1, LRU_W), F32)],
        compiler_params=_params("arbitrary"),
        name="ffn_rec",
    )(x, *weights)


def _p_proj(x1, norm_g, w_qkv):
    batch, seq, _ = x1.shape
    tile = PROJ_TILE
    nt = seq // tile
    qkv_shapes, qkv_specs = [], []
    for _kind in range(3):
        for window, dil in GROUPS:
            copies = HEADS if _kind == 0 else 1
            qkv_shapes.append(jax.ShapeDtypeStruct((batch, dil, copies * seq // dil, ATT_W), BF16))
            qkv_specs.append(pl.BlockSpec((1, dil, copies * tile // dil, ATT_W), lambda b, t: (b, 0, t, 0)))
    win_shapes, win_specs = [], []
    for window, dil in GROUPS:
        keep = min(window, seq)
        blk = min(keep, tile)
        first = nt - max(keep // tile, 1)
        for _kv in range(2):
            win_shapes.append(jax.ShapeDtypeStruct((batch, ATT_W, keep), F32))
            win_specs.append(pl.BlockSpec((1, ATT_W, blk),
                                          lambda b, t, first=first: (b, 0, jnp.maximum(t - first, 0))))
    return pl.pallas_call(
        _p_proj_body,
        grid=(batch, nt),
        in_specs=[pl.BlockSpec((1, tile, D_MODEL), lambda b, t: (b, t, 0)),
                  _resident((1, D_MODEL)), _col_window(D_MODEL, 0, QKV_W)],
        out_specs=qkv_specs + win_specs,
        out_shape=qkv_shapes + win_shapes,
        scratch_shapes=[pltpu.VMEM((3 * (len(GROUPS) - 1), ATT_CHUNKS, tile, LANES), F32)],
        compiler_params=_params("parallel", "arbitrary"),
        name="p_proj",
    )(x1, norm_g, w_qkv)


def _p_mix(x1, qkv, y_rec, mkv, biases, norm_g, w_c, b_gate, w_att_o, w_rec_o, w_mem_o, w_out):
    batch, seq, _ = x1.shape
    tile = MIX_TILE

    def tok(width):
        return pl.BlockSpec((1, tile, width), lambda b, t: (b, t, 0))

    q_specs, kv_specs, hist = [], [], []
    for window, dil in GROUPS:
        cur = tile // dil
        q_specs.append(pl.BlockSpec((1, dil, HEADS * cur, ATT_W), lambda b, t: (b, 0, t, 0)))
        kv_specs.append(pl.BlockSpec((1, dil, cur, ATT_W), lambda b, t: (b, 0, t, 0)))
        hist.append(pltpu.VMEM((dil * 2 * N_REL, ATT_W) if cur < N_REL else (2, dil * (N_REL + cur), ATT_W), BF16))
    windows = [_col_window(D_MODEL, col, FILL_COLS) for col in range(COL_QM, IN_COLS, FILL_COLS)]
    tail_w = [b_gate, w_att_o, w_rec_o, w_mem_o, w_out]
    weights = [norm_g] + [w_c] * len(windows) + tail_w
    in_specs = ([tok(D_MODEL)] + q_specs + kv_specs * 2
                + [tok(LRU_W), pl.BlockSpec((1, N_MEM, 2 * MEM_W), lambda b, t: (b, 0, 0))]
                + [_resident(a.shape) for a in list(biases) + [norm_g]] + windows
                + [_resident(a.shape) for a in tail_w])
    return pl.pallas_call(
        functools.partial(_p_mix_body, tiles_per_seq=seq // tile),
        grid=(batch, seq // tile),
        in_specs=in_specs,
        out_specs=tok(D_MODEL),
        out_shape=jax.ShapeDtypeStruct((batch, seq, D_MODEL), F32),
        scratch_shapes=hist * 2 + [pltpu.VMEM((len(GROUPS) * ATT_CHUNKS, tile, LANES), F32)] * 2,
        compiler_params=_params("parallel", "arbitrary"),
        name="p_mix",
    )(x1, *qkv, y_rec, mkv, *biases, *weights)


def _merge(x1, y_att, y_rec, y_mem, norm_g, w_c, b_gate, w_att_o, w_rec_o, w_mem_o, w_out):
    batch, seq, _ = x1.shape
    tile = seq

    def tok(width):
        return pl.BlockSpec((1, tile, width), lambda b, t: (b, t, 0))

    weights = [norm_g, w_c, b_gate, w_att_o, w_rec_o, w_mem_o, w_out]
    return pl.pallas_call(
        _merge_body,
        grid=(batch, seq // tile),
        in_specs=[tok(D_MODEL), tok(ATT_W), tok(LRU_W), tok(MEM_W)] + [_resident(w.shape) for w in weights],
        out_specs=tok(D_MODEL),
        out_shape=jax.ShapeDtypeStruct((batch, seq, D_MODEL), F32),
        compiler_params=_params("parallel", "parallel"),
        name="merge_s",
    )(x1, y_att, y_rec, y_mem, *weights)


def _s_mix(z, caches, state_conv, state_lru, cmk, cmv, bias_c, bias_n, lru):
    batch, steps, _ = z.shape

    def per_batch(shape):
        return pl.BlockSpec((1,) + tuple(shape[1:]), lambda b: (b,) + (0,) * (len(shape) - 1))

    ins = [z] + list(caches) + [state_conv, state_lru, cmk, cmv]
    in_specs = [per_batch(a.shape) for a in ins]
    consts = list(bias_c) + [bias_n] + list(lru)
    in_specs += [_resident(a.shape) for a in consts]
    out_shapes = [jax.ShapeDtypeStruct((batch, steps, ATT_W), BF16),
                  jax.ShapeDtypeStruct((batch, steps, LRU_W), BF16),
                  jax.ShapeDtypeStruct((batch, steps, MEM_W), BF16)]
    out_shapes += [jax.ShapeDtypeStruct(c.shape, F32) for c in caches]
    out_shapes += [jax.ShapeDtypeStruct(state_conv.shape, F32), jax.ShapeDtypeStruct(state_lru.shape, F32)]
    return pl.pallas_call(
        functools.partial(_s_mix_body, steps=steps),
        grid=(batch,),
        in_specs=in_specs,
        out_specs=[per_batch(s.shape) for s in out_shapes],
        out_shape=out_shapes,
        scratch_shapes=[pltpu.VMEM((2 * SUBLANES, LRU_W), F32)],
        compiler_params=_params("parallel"),
        name="s_mix",
    )(*ins, *consts)


def _bucket(n):
    n = np.maximum(n, 0)
    max_exact = NUM_BUCKETS // 2
    nf = np.maximum(n, 1).astype(np.float32)
    large = max_exact + (np.log(nf / max_exact) / np.float32(math.log(MAX_DISTANCE / max_exact))
                         * (NUM_BUCKETS - max_exact)).astype(np.int32)
    return np.where(n < max_exact, n, np.minimum(large, NUM_BUCKETS - 1))


def _dist_bias(rel_bias, g, dist, valid):
    tab = rel_bias[:, g * HEADS:(g + 1) * HEADS].T[:, _bucket(dist)]
    return jnp.where(valid[None], tab, -jnp.inf).astype(F32)


def _toeplitz(u, rows, cols):
    lu = rows + cols - 1
    ue = jnp.concatenate([u, u[:, :1]], axis=1)
    m = jnp.tile(ue, (1, rows))[:, :rows * lu].reshape(-1, rows, lu)
    return m[:, :, rows - 1:rows - 1 + cols].reshape(-1, cols)


def _prompt_bias(rel_bias, tile, tiles_per_seq):
    ring = 2 * N_REL
    col = np.arange(ring)
    stacks = []
    for g, (_, dil) in enumerate(GROUPS):
        cur = tile // dil
        qrows = min(cur, N_REL)
        hist = ring - qrows
        delta = np.arange(-(qrows - 1), ring)
        v = _dist_bias(rel_bias, g, dil * delta, (delta >= 0) & (delta <= N_REL))
        base = _toeplitz(v[:, ::-1], qrows, ring)
        if cur >= N_REL:
            tabs = [base, jnp.where(col >= hist, base, -jnp.inf)]
        else:
            slots = ring // cur
            tabs = []
            for t in range(min(tiles_per_seq, 2 * slots)):
                cols = []
                for s in range(slots):
                    age = (t - s) % slots
                    blk = base[:, hist - cur * age:hist - cur * age + cur]
                    cols.append(blk if t - age >= 0 else jnp.full_like(blk, -jnp.inf))
                tabs.append(jnp.concatenate(cols, axis=1))
        stacks.append(jnp.stack(tabs))
    return stacks


def _sample_bias(rel_bias, steps):
    t = np.arange(steps)[:, None]
    cache, new = [], []
    for g, (window, dil) in enumerate(GROUPS):
        dist = np.arange(window + steps)
        tab = _dist_bias(rel_bias, g, dist, (dist % dil == 0) & (dist <= window))
        cache.append(_toeplitz(tab[:, 1:][:, ::-1], steps, window))
        j = np.arange(N_REL)[None, :]
        dist = t - j
        valid = (j < steps) & (dist >= 0) & (dist % dil == 0)
        new.append(_dist_bias(rel_bias, g, dist, valid).reshape(HEADS * steps, N_REL))
    return cache, jnp.stack(new)


def _block_diag(w):
    per = MXU_TILE // LRU_BLOCK
    w4 = w.reshape(-1, per, LRU_BLOCK, LRU_BLOCK)
    bd = jnp.einsum('jnef,nm->jnemf', w4, jnp.eye(per, dtype=w.dtype))
    return bd.reshape(-1, MXU_TILE, MXU_TILE).astype(BF16)


def _to_pos_minor(c):
    return jnp.transpose(c[0], (0, 2, 3, 1)).reshape(c.shape[1], ATT_W, c.shape[2])


def _from_pos_minor(c):
    return jnp.transpose(c.reshape(c.shape[0], HEADS, HEAD_DIM, c.shape[2]), (0, 3, 1, 2))[None]


def kernel(x_prompt, x_sample, mem_prompt, cache_win_k0, cache_win_v0, cache_win_k1, cache_win_v1, cache_win_k2, cache_win_v2, state_conv, state_lru, cache_mem_k, cache_mem_v, rel_bias, ffn1_norm, ffn1_w_gu, ffn1_w_down, mix_norm, w_in, b_gate, conv_w, conv_b, lru_w_a, lru_b_a, lru_w_i, lru_b_i, lru_lambda, mem_norm, w_mem_kv, w_att_o, w_rec_o, w_mem_o, w_out, ffn2_norm, ffn2_w_gu, ffn2_w_down, final_norm):
    depth = ffn1_norm.shape[0]
    assert depth == 1
    l = 0
    batch, seq, _ = x_prompt.shape
    sbatch, steps, _ = x_sample.shape
    row = lambda v: v.reshape(1, -1)

    lru = (conv_w[l], row(conv_b[l]), _block_diag(lru_w_a[l]), _block_diag(lru_w_i[l]),
           row(lru_b_a[l]), row(lru_b_i[l]), row(lru_lambda[l]))
    ffn1 = (row(ffn1_norm[l]), ffn1_w_gu[l].astype(BF16), ffn1_w_down[l].astype(BF16))
    ffn2 = (row(ffn2_norm[l]), ffn2_w_gu[l].astype(BF16), ffn2_w_down[l].astype(BF16))
    outw = (row(b_gate[l]), w_att_o[l].astype(BF16), w_rec_o[l].astype(BF16), w_mem_o[l].astype(BF16),
            w_out[l].astype(BF16))
    mixg = row(mix_norm[l])
    fin = row(final_norm)

    w_in_b = w_in[l].astype(BF16)
    xp1, y_rec, p_conv, p_lru = _ffn_rec(x_prompt.reshape(batch * seq, D_MODEL), *ffn1, mixg, w_in_b, lru, batch, seq)
    xp1 = xp1.reshape(batch, seq, D_MODEL)
    p_mk, p_mv, mkv = _mem_kv(mem_prompt.reshape(batch * N_MEM, D_MODEL), row(mem_norm[l]),
                              w_mem_kv[l].astype(BF16), row_tile=2 * N_MEM)
    proj = _p_proj(xp1, mixg, w_in_b)
    qkv, wins = proj[0:9], proj[9:15]
    xp2 = _p_mix(xp1, qkv, y_rec, mkv.reshape(batch, N_MEM, 2 * MEM_W), _prompt_bias(rel_bias, MIX_TILE, seq // MIX_TILE), mixg,
                 w_in_b, *outw)
    y_prompt = _ffn(xp2.reshape(batch * seq, D_MODEL), *ffn2, final_g=fin).reshape(batch, seq, D_MODEL)

    n_s = sbatch * steps
    xs1 = _ffn(x_sample.reshape(n_s, D_MODEL), *ffn1)
    z = _norm_mm(xs1, mixg, w_in_b, row_tile=n_s, col_tile=IN_COLS // 2)
    caches = [_to_pos_minor(c) for c in
              (cache_win_k0, cache_win_v0, cache_win_k1, cache_win_v1, cache_win_k2, cache_win_v2)]
    bias_c, bias_n = _sample_bias(rel_bias, steps)
    mem_rows = lambda c: c[l].reshape(sbatch, N_MEM * MEM_HEADS, MEM_HEAD_DIM)
    mix = _s_mix(z.reshape(sbatch, steps, IN_COLS), caches, state_conv[l], state_lru[l].reshape(sbatch, 1, LRU_W),
                 mem_rows(cache_mem_k), mem_rows(cache_mem_v), bias_c, bias_n, lru)
    s_att, s_rec, s_mem = (a.reshape(1, n_s, a.shape[-1]) for a in mix[0:3])
    xs2 = _merge(xs1.reshape(1, n_s, D_MODEL), s_att, s_rec, s_mem, mixg, w_in[l][:, COL_GATE:].astype(BF16),
                 *outw)
    y_sample = _ffn(xs2.reshape(n_s, D_MODEL), *ffn2, final_g=fin).reshape(sbatch, steps, D_MODEL)

    mem_state = lambda a: a.reshape(1, batch, N_MEM, MEM_HEADS, MEM_HEAD_DIM)
    return (y_prompt, y_sample, *[_from_pos_minor(w) for w in wins],
            p_conv[None], p_lru.reshape(1, batch, LRU_W), mem_state(p_mk), mem_state(p_mv),
            *[_from_pos_minor(w) for w in mix[3:9]], mix[9][None], mix[10].reshape(1, sbatch, LRU_W))
```

```python
import functools
import math

import numpy as np
import jax
import jax.numpy as jnp
from jax import lax
from jax.experimental import pallas as pl
from jax.experimental.pallas import tpu as pltpu

F32 = jnp.float32
BF16 = jnp.bfloat16

D_MODEL = 1024
D_FF = 2816
HEAD_DIM = 64
HEADS = 4
GROUPS = ((128, 1), (512, 4), (2048, 16))
N_REL = 128
ATT_W = HEADS * HEAD_DIM
ATT_SCALE = HEAD_DIM ** -0.5
NUM_BUCKETS = 32
MAX_DISTANCE = 2048
LRU_W = 768
LRU_BLOCK = 64
LRU_C = 8.0
CONV_W = 4
N_MEM = 256
MEM_HEADS = 4
MEM_HEAD_DIM = 128
MEM_W = MEM_HEADS * MEM_HEAD_DIM
MEM_SCALE = MEM_HEAD_DIM ** -0.5
RMS_EPS = 1e-6
QKV_W = 3 * len(GROUPS) * ATT_W
COL_XR = QKV_W
COL_YR = COL_XR + LRU_W
COL_QM = COL_YR + LRU_W
COL_GATE = COL_QM + MEM_W
IN_COLS = COL_GATE + 3 * D_MODEL

V7X_VMEM_LIMIT = 62 * 1024 * 1024
MXU_TILE = 256
SUBLANES = 8
LANES = 128
FFN_TILE = 512
FFN_CHUNK = 256
PROJ_TILE = 512
SCAN_SEGS = SUBLANES
SCAN_STEPS = FFN_TILE // SCAN_SEGS
SCAN_PITCH = SCAN_STEPS + SUBLANES
MIX_TILE = 512
ATTN_DEPTH = 8
FILL_COLS = 256
S_MIX_ELEMS = 2
LRU_CHUNKS = LRU_W // LANES
ATT_CHUNKS = ATT_W // LANES


def _dot(a, b):
    return jnp.dot(a, b, preferred_element_type=F32)


def _dot_nt(a, b):
    return lax.dot_general(a, b, (((1,), (1,)), ((), ())), preferred_element_type=F32)


def _rms(x, g):
    return x * lax.rsqrt(jnp.mean(x * x, axis=-1, keepdims=True) + RMS_EPS) * g


def _stack_heads(q):
    lane = lax.broadcasted_iota(jnp.int32, q.shape, 1)
    return jnp.concatenate(
        [jnp.where((lane >= h * HEAD_DIM) & (lane < (h + 1) * HEAD_DIM), q, 0.0) for h in range(HEADS)], axis=0)


def _unstack_heads(x, rows):
    low = lax.broadcasted_iota(jnp.int32, (rows, LANES), 1) < HEAD_DIM
    per = LANES // HEAD_DIM
    return [jnp.where(low,
                      x[(per * c) * rows:(per * c + 1) * rows, c * LANES:(c + 1) * LANES],
                      x[(per * c + 1) * rows:(per * c + 2) * rows, c * LANES:(c + 1) * LANES])
            for c in range(ATT_CHUNKS)]


def _linear_scan(a, u, rows):
    row = lax.broadcasted_iota(jnp.int32, (rows, 1), 0)
    d = 1
    while d < rows:
        a_s = pltpu.roll(a, d, 0)
        u_s = pltpu.roll(u, d, 0)
        keep = row >= d
        u = jnp.where(keep, a * u_s + u, u)
        a = jnp.where(keep, a * a_s, a)
        d *= 2
    return a, u


def _zero_after(x):
    bits = pltpu.bitcast(x, jnp.uint32)
    half = jnp.uint32(17)
    return pltpu.bitcast(lax.shift_right_logical(lax.shift_right_logical(bits, half), half), F32)


def _lru_preact(xc, wa_ref, wi_ref):
    xcb = xc.astype(BF16)
    ra, ia = [], []
    for j in range(LRU_W // MXU_TILE):
        blk = xcb[:, j * MXU_TILE:(j + 1) * MXU_TILE]
        ra.append(_dot(blk, wa_ref[j]))
        ia.append(_dot(blk, wi_ref[j]))
    return jnp.concatenate(ra, axis=-1), jnp.concatenate(ia, axis=-1)


def _lru_decay_input(xc, ra, ia, ba_ref, bi_ref, lam_ref):
    r = jax.nn.sigmoid(ra + ba_ref[...])
    i = jax.nn.sigmoid(ia + bi_ref[...])
    nl = -lam_ref[...]
    softplus = jnp.maximum(nl, 0.0) + jnp.log1p(jnp.exp(-jnp.abs(nl)))
    log_a = (-LRU_C * r) * softplus
    a = jnp.exp(log_a)
    u = jnp.sqrt(1.0 - a * a) * (i * xc)
    return a, u


def _lru_gates(xc, wa_ref, wi_ref, ba_ref, bi_ref, lam_ref):
    return _lru_decay_input(xc, *_lru_preact(xc, wa_ref, wi_ref), ba_ref, bi_ref, lam_ref)


def _conv_taps(cw_ref, cb_ref, taps):
    xc = cb_ref[...] + cw_ref[CONV_W - 1:CONV_W, :] * taps[0]
    for j in range(1, CONV_W):
        xc = xc + cw_ref[CONV_W - 1 - j:CONV_W - j, :] * taps[j]
    return xc


def _mem_attention(qm, mk_heads, mv_heads):
    heads = []
    for h in range(MEM_HEADS):
        q = qm[:, h * MEM_HEAD_DIM:(h + 1) * MEM_HEAD_DIM].astype(BF16)
        s = _dot_nt(q, mk_heads[h]) * MEM_SCALE
        m = jnp.max(s, axis=-1, keepdims=True)
        p = jnp.exp(s - m)
        l = jnp.sum(p, axis=-1, keepdims=True)
        heads.append(_dot(p.astype(BF16), mv_heads[h]) * (1.0 / l))
    return jnp.concatenate(heads, axis=-1)


def _ffn_body(x_ref, g_ref, wgu_ref, wd_ref, *rest, final_norm):
    if final_norm:
        fg_ref, o_ref, act_ref = rest
    else:
        o_ref, act_ref = rest
    x = x_ref[...]
    hn = _rms(x, g_ref[...]).astype(BF16)
    for c in range(D_FF // FFN_CHUNK):
        lo = c * FFN_CHUNK
        gate = _dot(hn, wgu_ref[:, lo:lo + FFN_CHUNK])
        up = _dot(hn, wgu_ref[:, D_FF + lo:D_FF + lo + FFN_CHUNK])
        act_ref[:, lo:lo + FFN_CHUNK] = (gate * jax.nn.sigmoid(gate) * up).astype(BF16)
    y = x + 0.5 * _dot(act_ref[...], wd_ref[...])
    if final_norm:
        y = _rms(y, fg_ref[...])
    o_ref[...] = y


def _norm_mm_body(x_ref, g_ref, w_ref, o_ref):
    hn = _rms(x_ref[...], g_ref[...]).astype(BF16)
    o_ref[...] = _dot(hn, w_ref[...])


def _mem_kv_body(x_ref, g_ref, w_ref, mk_ref, mv_ref, mkv_ref):
    rows = x_ref.shape[0]
    z = _dot(_rms(x_ref[...], g_ref[...]).astype(BF16), w_ref[...])
    mkv_ref[...] = z.astype(BF16)
    for h in range(MEM_HEADS):
        dst = pl.ds(h, rows, stride=MEM_HEADS)
        mk_ref[dst, :] = z[:, h * MEM_HEAD_DIM:(h + 1) * MEM_HEAD_DIM]
        mv_ref[dst, :] = z[:, MEM_W + h * MEM_HEAD_DIM:MEM_W + (h + 1) * MEM_HEAD_DIM]


def _ffn_rec_body(x_ref, g_ref, wgu_ref, wd_ref, mg_ref, wxr_ref, wyr_ref, cw_ref, cb_ref, wa_ref, wi_ref, ba_ref, bi_ref,
                  lam_ref, o_ref, yrec_ref, pconv_ref, plru_ref, act_ref, xbuf_ref, xpad_ref, hpad_ref, hc_ref,
                  *, tiles_per_seq):
    rows = FFN_TILE
    i = pl.program_id(0)
    t = lax.rem(jnp.maximum(i - 1, 0), tiles_per_seq)
    slot = lax.rem(i, 2)
    tail = slice(SCAN_STEPS - SUBLANES, SCAN_STEPS)

    @pl.when(i == 0)
    def _():
        xbuf_ref[...] = jnp.zeros(xbuf_ref.shape, F32)

    @pl.when(t == 0)
    def _():
        xpad_ref[:, tail, :] = jnp.zeros((LRU_CHUNKS, SUBLANES, LANES), F32)
        hc_ref[...] = jnp.zeros((1, LRU_W), F32)

    hn_p = _rms(xbuf_ref[1 - slot], mg_ref[...]).astype(BF16)
    xr = _dot(hn_p, wxr_ref[...])
    for c in range(LRU_CHUNKS):
        for s in range(SCAN_SEGS):
            lo = SCAN_PITCH * (s + 1)
            xpad_ref[c, lo:lo + SCAN_STEPS, :] = xr[s * SCAN_STEPS:(s + 1) * SCAN_STEPS, c * LANES:(c + 1) * LANES]

    x = x_ref[...]
    hn = _rms(x, g_ref[...]).astype(BF16)

    def ffn_chunk(c):
        lo = c * FFN_CHUNK
        gate = _dot(hn, wgu_ref[:, lo:lo + FFN_CHUNK])
        up = _dot(hn, wgu_ref[:, D_FF + lo:D_FF + lo + FFN_CHUNK])
        act_ref[:, lo:lo + FFN_CHUNK] = (gate * jax.nn.sigmoid(gate) * up).astype(BF16)
        return gate

    n_chunks = D_FF // FFN_CHUNK
    lead = 2
    for c in range(lead):
        ffn_chunk(c)

    def step_rows(j):
        start = SCAN_PITCH + j if j >= 0 else SCAN_STEPS + j
        sel = pl.ds(start, SCAN_SEGS, stride=SCAN_PITCH)
        return jnp.concatenate([xpad_ref[c, sel, :] for c in range(LRU_CHUNKS)], axis=-1)

    hist = CONV_W - 1
    xp = jnp.concatenate([step_rows(j) for j in range(-hist, SCAN_STEPS)], axis=0)
    n = SCAN_STEPS * SCAN_SEGS
    taps = [xp[(hist - j) * SCAN_SEGS:(hist - j) * SCAN_SEGS + n, :] for j in range(CONV_W)]
    xc = _conv_taps(cw_ref, cb_ref, taps)
    ra, ia = _lru_preact(xc, wa_ref, wi_ref)

    yr = _dot(hn_p, wyr_ref[...])
    a_parts, u_parts = [], []
    per = n // SCAN_SEGS
    for k in range(SCAN_SEGS):
        gate = ffn_chunk(lead + k)
        tie = jnp.concatenate([_zero_after(gate[0:per, 0:LANES])] * LRU_CHUNKS, axis=-1)
        sl = slice(k * per, (k + 1) * per)
        a_k, u_k = _lru_decay_input(xc[sl] + tie, ra[sl], ia[sl], ba_ref, bi_ref, lam_ref)
        a_parts.append(a_k)
        u_parts.append(u_k)
    a = jnp.concatenate(a_parts, axis=0)
    u = jnp.concatenate(u_parts, axis=0)
    for c in range(lead + SCAN_SEGS, n_chunks):
        ffn_chunk(c)

    def rows_of(v, j):
        return v[j * SCAN_SEGS:(j + 1) * SCAN_SEGS, :]

    h = jnp.zeros((SCAN_SEGS, LRU_W), F32)
    prod = jnp.ones((SCAN_SEGS, LRU_W), F32)
    for j in range(SCAN_STEPS):
        a_j = rows_of(a, j)
        h = a_j * h + rows_of(u, j)
        prod = a_j * prod
    prod_c, h_c = _linear_scan(prod, h, SCAN_SEGS)
    h_end = prod_c * hc_ref[...] + h_c
    seg = lax.broadcasted_iota(jnp.int32, (SCAN_SEGS, 1), 0)
    h = jnp.where(seg == 0, hc_ref[...], pltpu.roll(h_end, 1, 0))
    for j in range(SCAN_STEPS):
        h = rows_of(a, j) * h + rows_of(u, j)
        dst = pl.ds(j, SCAN_SEGS, stride=SCAN_PITCH)
        for c in range(LRU_CHUNKS):
            hpad_ref[c, dst, :] = h[:, c * LANES:(c + 1) * LANES]
    hs = jnp.concatenate(
        [jnp.concatenate([hpad_ref[c, SCAN_PITCH * s:SCAN_PITCH * s + SCAN_STEPS, :] for c in range(LRU_CHUNKS)],
                         axis=-1) for s in range(SCAN_SEGS)], axis=0)
    yrec_ref[0] = (jax.nn.gelu(yr) * hs).astype(BF16)
    hc_ref[...] = h_end[SCAN_SEGS - 1:SCAN_SEGS, :]
    plru_ref[0] = h_end[SCAN_SEGS - 1:SCAN_SEGS, :]
    pconv_ref[0] = xr[rows - hist:rows, :]
    for c in range(LRU_CHUNKS):
        xpad_ref[c, tail, :] = xr[rows - SUBLANES:rows, c * LANES:(c + 1) * LANES]

    y = x + 0.5 * _dot(act_ref[...], wd_ref[...])
    o_ref[...] = y
    xbuf_ref[slot] = y


def _p_proj_body(x_ref, g_ref, w_ref, q0, q1, q2, k0, k1, k2, v0, v1, v2, pk0, pv0, pk1, pv1, pk2, pv2, zs_ref):
    rows = PROJ_TILE
    hn = _rms(x_ref[0], g_ref[...]).astype(BF16)
    outs = ((q0, q1, q2), (k0, k1, k2), (v0, v1, v2))
    wins = (None, (pk0, pk1, pk2), (pv0, pv1, pv2))
    for g, (window, dil) in reversed(list(enumerate(GROUPS))):
        for kind in (1, 2, 0):
            col = (kind * len(GROUPS) + g) * ATT_W
            z = _dot(hn, w_ref[:, col:col + ATT_W])
            if kind == 0:
                z = z * ATT_SCALE
            else:
                keep = min(window, rows)
                wins[kind][g][0] = jnp.transpose(z[rows - keep:, :])
            def emit(r, part):
                if kind == 0:
                    qrows = min(rows // dil, N_REL)
                    part = jnp.concatenate([_stack_heads(part[lo:lo + qrows, :])
                                            for lo in range(0, rows // dil, qrows)], axis=0)
                outs[kind][g][0, r] = part.astype(BF16)

            if dil == 1:
                emit(0, z)
            else:
                stage = zs_ref.at[kind * (len(GROUPS) - 1) + g - 1]
                for c in range(ATT_CHUNKS):
                    stage[c] = z[:, c * LANES:(c + 1) * LANES]
                for r in range(dil):
                    sel = pl.ds(r, rows // dil, stride=dil)
                    emit(r, jnp.concatenate([stage[c, sel, :] for c in range(ATT_CHUNKS)], axis=-1))


def _softmax_parts(s):
    m = jnp.max(s, axis=-1, keepdims=True)
    p = jnp.exp(s - m)
    return p.astype(BF16), jnp.sum(p, axis=-1, keepdims=True), m


def _gate(hn, wc_ref, bg_ref, gate_col, idx):
    lo = idx * D_MODEL
    return jax.nn.sigmoid(_dot(hn, wc_ref[:, gate_col + lo:gate_col + lo + D_MODEL]) + bg_ref[:, lo:lo + D_MODEL])


def _p_mix_body(x_ref, q0, q1, q2, k0, k1, k2, v0, v1, v2, yr_ref, mkv_ref, b0, b1, b2, g_ref, *rest,
                tiles_per_seq):
    n_win = (IN_COLS - COL_QM) // FILL_COLS
    wc_refs = rest[:n_win]
    bg_ref, wao_ref, wro_ref, wmo_ref, wout_ref, o_ref, kh0, kh1, kh2, vh0, vh1, vh2, o_scr, l_scr = rest[n_win:]
    qm_wins = MEM_W // FILL_COLS
    tile = MIX_TILE
    t = pl.program_id(1)
    slot = lax.rem(t, 2)
    qs_, ks_, vs_ = (q0, q1, q2), (k0, k1, k2), (v0, v1, v2)
    khs, vhs, biases = (kh0, kh1, kh2), (vh0, vh1, vh2), (b0, b1, b2)
    ring = 2 * N_REL

    def in_ring(g):
        return tile // GROUPS[g][1] < N_REL

    @pl.when(t == 0)
    def _():
        for g, (window, dil) in enumerate(GROUPS):
            for h_ref in (khs[g], vhs[g]):
                if in_ring(g):
                    h_ref[...] = jnp.zeros(h_ref.shape, BF16)
                else:
                    cur = tile // dil
                    for r in range(dil):
                        h_ref[0, r * (N_REL + cur):r * (N_REL + cur) + N_REL, :] = jnp.zeros((N_REL, ATT_W), BF16)

    res = {}

    windows, carried = [], []

    def window_block(g, dil, r, lo, qrows, keys, tab, new_rows):
        def rows_of(h_ref, sel):
            return h_ref.at[sel] if in_ring(g) else h_ref.at[slot, sel]

        def scores():
            if new_rows is not None:
                rows_of(khs[g], new_rows)[...] = ks_[g][0, r]
                rows_of(vhs[g], new_rows)[...] = vs_[g][0, r]
            q = qs_[g][0, r, HEADS * lo:HEADS * (lo + qrows), :]
            return _softmax_parts(_dot_nt(q, rows_of(khs[g], keys)[...]) + tab)

        def values(parts):
            p, l, m = parts
            o = _dot(p, rows_of(vhs[g], keys)[...]) * (1.0 / l)
            lse = jnp.broadcast_to(m + jnp.log(l), o.shape)
            o_c, l_c = _unstack_heads(o, qrows), _unstack_heads(lse, qrows)
            dst = slice(lo, lo + qrows) if dil == 1 else pl.ds(lo * dil + r, qrows, stride=dil)
            for c in range(ATT_CHUNKS):
                o_scr[g * ATT_CHUNKS + c, dst, :] = o_c[c]
                l_scr[g * ATT_CHUNKS + c, dst, :] = l_c[c]
        return scores, values

    for g, (window, dil) in enumerate(GROUPS):
        cur = tile // dil
        if in_ring(g):
            slots = ring // cur
            tabs = biases[g].shape[0]
            tab = biases[g][t if tabs >= tiles_per_seq else jnp.where(t < slots, t, slots + lax.rem(t, slots))]
            newest = lax.rem(t, slots) * cur
            for r in range(dil):
                base = r * ring
                windows.append(window_block(g, dil, r, 0, cur, slice(base, base + ring), tab,
                                            pl.ds(pl.multiple_of(base + newest, cur), cur)))
        else:
            for r in range(dil):
                base = r * (N_REL + cur)
                carried.append((g, base, cur))
                for j in range(cur // N_REL):
                    tab = biases[g][jnp.where(t == 0, 1, 0)] if j == 0 else biases[g][0]
                    lo = j * N_REL
                    windows.append(window_block(g, dil, r, lo, N_REL, slice(base + lo, base + lo + ring), tab,
                                                slice(base + N_REL, base + N_REL + cur) if j == 0 else None))

    def memory_block(h):
        sl = slice(h * MEM_HEAD_DIM, (h + 1) * MEM_HEAD_DIM)

        def scores():
            return _softmax_parts(_dot_nt(res["qm"][:, sl].astype(BF16), mkv_ref[0, :, sl]) * MEM_SCALE)

        def values(parts):
            p, l, _ = parts
            res["mem", h] = _dot(p, mkv_ref[0, :, MEM_W + h * MEM_HEAD_DIM:MEM_W + (h + 1) * MEM_HEAD_DIM]) * (1.0 / l)
        return scores, values

    def normed_input():
        res["x"] = x_ref[0]
        res["hn"] = _rms(res["x"], g_ref[...]).astype(BF16)
        res["qm"] = jnp.concatenate([_dot(res["hn"], w[...]) for w in wc_refs[:qm_wins]], axis=-1)

    def gate_piece(idx, piece):
        def emit():
            lo = idx * D_MODEL + piece * FILL_COLS
            cols = slice(piece * FILL_COLS, (piece + 1) * FILL_COLS)
            gate = jax.nn.sigmoid(_dot(res["hn"], wc_refs[qm_wins + lo // FILL_COLS][...])
                                  + bg_ref[:, lo:lo + FILL_COLS])
            res["gate", idx, piece] = gate * _dot(yr_ref[0], wro_ref[:, cols]) if idx == 1 else gate
        return emit

    normed_input()
    blocks = windows + [memory_block(h) for h in range(MEM_HEADS)]
    pieces = D_MODEL // FILL_COLS
    fillers = [gate_piece(idx, piece) for idx in (1, 0, 2) for piece in range(pieces)]
    every = max(1, len(blocks) // len(fillers))

    pending = []
    for i, (scores, values) in enumerate(blocks):
        pending.append((values, scores()))
        if len(pending) > ATTN_DEPTH:
            done, parts = pending.pop(0)
            done(parts)
        if fillers and (i + 1) % every == 0:
            fillers.pop(0)()
    for done, parts in pending:
        done(parts)
    for emit in fillers:
        emit()
    gates = [jnp.concatenate([res["gate", idx, piece] for piece in range(pieces)], axis=-1) for idx in range(3)]

    mixed = []
    for c in range(ATT_CHUNKS):
        l0, l1, l2 = (l_scr[g * ATT_CHUNKS + c] for g in range(len(GROUPS)))
        o0, o1, o2 = (o_scr[g * ATT_CHUNKS + c] for g in range(len(GROUPS)))
        m = jnp.maximum(jnp.maximum(l0, l1), l2)
        w0, w1, w2 = jnp.exp(l0 - m), jnp.exp(l1 - m), jnp.exp(l2 - m)
        mixed.append((w0 * o0 + w1 * o1 + w2 * o2) / (w0 + w1 + w2))
    ya = jnp.concatenate(mixed, axis=-1).astype(BF16)
    ym = jnp.concatenate([res["mem", h] for h in range(MEM_HEADS)], axis=-1).astype(BF16)
    merged = gates[0] * _dot(ya, wao_ref[...]) + gates[1] + gates[2] * _dot(ym, wmo_ref[...])
    o_ref[0] = res["x"] + _dot(merged.astype(BF16), wout_ref[...])
    for g, base, cur in carried:
        for h_ref in (khs[g], vhs[g]):
            h_ref[1 - slot, base:base + N_REL, :] = h_ref[slot, base + cur:base + cur + N_REL, :]


def _merge_out(x, hn, branches, wc_ref, gate_col, bg_ref, wout_ref):
    merged = None
    for idx, (y, w_ref) in enumerate(branches):
        term = _gate(hn, wc_ref, bg_ref, gate_col, idx) * _dot(y, w_ref[...])
        merged = term if merged is None else merged + term
    return x + _dot(merged.astype(BF16), wout_ref[...])


def _merge_body(x_ref, ya_ref, yr_ref, ym_ref, g_ref, wc_ref, bg_ref, wao_ref, wro_ref, wmo_ref, wout_ref, o_ref):
    x = x_ref[0]
    hn = _rms(x, g_ref[...]).astype(BF16)
    o_ref[0] = _merge_out(x, hn, ((ya_ref[0], wao_ref), (yr_ref[0], wro_ref), (ym_ref[0], wmo_ref)),
                          wc_ref, 0, bg_ref, wout_ref)


def _s_mix_body(z_ref, ck0, cv0, ck1, cv1, ck2, cv2, sconv_ref, slru_ref, cmk_ref, cmv_ref,
                bc0, bc1, bc2, bn_ref, cw_ref, cb_ref, wa_ref, wi_ref, ba_ref, bi_ref, lam_ref,
                ya_ref, yrec_ref, ym_ref, ok0, ov0, ok1, ov1, ok2, ov2, oconv_ref, olru_ref,
                ext_ref, *, steps):
    cks, cvs = (ck0, ck1, ck2), (cv0, cv1, cv2)
    oks, ovs = (ok0, ok1, ok2), (ov0, ov1, ov2)
    bcs = (bc0, bc1, bc2)
    zeros = jnp.zeros((N_REL - steps, ATT_W), F32)
    new_lanes = lax.broadcasted_iota(jnp.int32, (ATT_W, LANES), 1) >= LANES - steps
    outs, lses = [], []
    for g, (window, dil) in enumerate(GROUPS):
        q = z_ref[0, :, g * ATT_W:(g + 1) * ATT_W] * ATT_SCALE
        kn = z_ref[0, :, (3 + g) * ATT_W:(4 + g) * ATT_W]
        vn = z_ref[0, :, (6 + g) * ATT_W:(7 + g) * ATT_W]
        qs = _stack_heads(q).astype(BF16)
        kt = cks[g][0]
        vt = cvs[g][0]
        s_c = _dot(qs, kt.astype(BF16)) + bcs[g][...]
        s_n = _dot_nt(qs, jnp.concatenate([kn, zeros], axis=0).astype(BF16)) + bn_ref[g]
        m = jnp.maximum(jnp.max(s_c, axis=-1, keepdims=True), jnp.max(s_n, axis=-1, keepdims=True))
        p_c = jnp.exp(s_c - m)
        p_n = jnp.exp(s_n - m)
        l = jnp.sum(p_c, axis=-1, keepdims=True) + jnp.sum(p_n, axis=-1, keepdims=True)
        o = (_dot_nt(p_c.astype(BF16), vt.astype(BF16))
             + _dot(p_n.astype(BF16), jnp.concatenate([vn, zeros], axis=0).astype(BF16))) * (1.0 / l)
        lse = jnp.broadcast_to(m + jnp.log(l), o.shape)
        outs.append(jnp.concatenate(_unstack_heads(o, steps), axis=-1))
        lses.append(jnp.concatenate(_unstack_heads(lse, steps), axis=-1))
        for old, new, dst in ((kt, kn, oks[g]), (vt, vn, ovs[g])):
            rolled = pltpu.roll(old, window - steps, 1)
            new_t = jnp.transpose(jnp.concatenate([zeros, new], axis=0))
            if window > LANES:
                dst[0, :, 0:window - LANES] = rolled[:, 0:window - LANES]
            dst[0, :, window - LANES:window] = jnp.where(new_lanes, new_t, rolled[:, window - LANES:window])
    m = jnp.maximum(jnp.maximum(lses[0], lses[1]), lses[2])
    ws = [jnp.exp(l - m) for l in lses]
    ya_ref[0] = ((ws[0] * outs[0] + ws[1] * outs[1] + ws[2] * outs[2]) / (ws[0] + ws[1] + ws[2])).astype(BF16)

    hist = CONV_W - 1
    ext_ref[0:SUBLANES, :] = jnp.zeros((SUBLANES, LRU_W), F32)
    ext_ref[SUBLANES - hist:SUBLANES, :] = sconv_ref[0]
    ext_ref[SUBLANES:SUBLANES + steps, :] = z_ref[0, :, COL_XR:COL_XR + LRU_W]
    taps = [ext_ref[SUBLANES - j:SUBLANES - j + steps, :] for j in range(CONV_W)]
    a, u = _lru_gates(_conv_taps(cw_ref, cb_ref, taps), wa_ref, wi_ref, ba_ref, bi_ref, lam_ref)
    a_c, u_c = _linear_scan(a, u, steps)
    h = a_c * slru_ref[0] + u_c
    yrec_ref[0] = (jax.nn.gelu(z_ref[0, :, COL_YR:COL_YR + LRU_W]) * h).astype(BF16)
    olru_ref[0] = h[steps - 1:steps, :]
    oconv_ref[0] = ext_ref[SUBLANES + steps - hist:SUBLANES + steps, :]

    qm = z_ref[0, :, COL_QM:COL_QM + MEM_W]
    mk = [cmk_ref[0, pl.ds(h, N_MEM, stride=MEM_HEADS), :].astype(BF16) for h in range(MEM_HEADS)]
    mv = [cmv_ref[0, pl.ds(h, N_MEM, stride=MEM_HEADS), :].astype(BF16) for h in range(MEM_HEADS)]
    ym_ref[0] = _mem_attention(qm, mk, mv).astype(BF16)


def _resident(shape):
    zeros = (0,) * len(shape)
    return pl.BlockSpec(shape, lambda *_: zeros, pipeline_mode=pl.Buffered(1))


def _col_window(rows, first_col, cols):
    assert first_col % cols == 0
    block = first_col // cols
    return pl.BlockSpec((rows, cols), lambda *_: (0, block), pipeline_mode=pl.Buffered(1))


def _params(*semantics):
    return pltpu.CompilerParams(dimension_semantics=semantics, vmem_limit_bytes=V7X_VMEM_LIMIT)


def _ffn(x, norm_g, w_gu, w_down, final_g=None):
    n = x.shape[0]
    tile = min(FFN_TILE, n)
    row = pl.BlockSpec((tile, D_MODEL), lambda i: (i, 0))
    in_specs = [row, _resident((1, D_MODEL)), _resident(w_gu.shape), _resident(w_down.shape)]
    args = [x, norm_g, w_gu, w_down]
    if final_g is not None:
        in_specs.append(_resident((1, D_MODEL)))
        args.append(final_g)
    return pl.pallas_call(
        functools.partial(_ffn_body, final_norm=final_g is not None),
        grid=(n // tile,),
        in_specs=in_specs,
        out_specs=row,
        out_shape=jax.ShapeDtypeStruct((n, D_MODEL), F32),
        scratch_shapes=[pltpu.VMEM((tile, D_FF), BF16)],
        compiler_params=_params("parallel"),
        name="ffn_final" if final_g is not None else "ffn",
    )(*args)


def _norm_mm(x, norm_g, w, row_tile, col_tile):
    n, cols = x.shape[0], w.shape[1]
    return pl.pallas_call(
        _norm_mm_body,
        grid=(n // row_tile, cols // col_tile),
        in_specs=[pl.BlockSpec((row_tile, D_MODEL), lambda i, j: (i, 0)),
                  _resident((1, D_MODEL)),
                  pl.BlockSpec((D_MODEL, col_tile), lambda i, j: (0, j))],
        out_specs=pl.BlockSpec((row_tile, col_tile), lambda i, j: (i, j)),
        out_shape=jax.ShapeDtypeStruct((n, cols), F32),
        compiler_params=_params("parallel", "parallel"),
        name="norm_mm",
    )(x, norm_g, w)


def _mem_kv(mem, norm_g, w_kv, row_tile):
    n = mem.shape[0]
    state = jax.ShapeDtypeStruct((n * MEM_HEADS, MEM_HEAD_DIM), F32)
    state_spec = pl.BlockSpec((row_tile * MEM_HEADS, MEM_HEAD_DIM), lambda i: (i, 0))
    return pl.pallas_call(
        _mem_kv_body,
        grid=(n // row_tile,),
        in_specs=[pl.BlockSpec((row_tile, D_MODEL), lambda i: (i, 0)),
                  _resident((1, D_MODEL)), _resident(w_kv.shape)],
        out_specs=[state_spec, state_spec, pl.BlockSpec((row_tile, 2 * MEM_W), lambda i: (i, 0))],
        out_shape=[state, state, jax.ShapeDtypeStruct((n, 2 * MEM_W), BF16)],
        compiler_params=_params("parallel"),
        name="mem_kv",
    )(mem, norm_g, w_kv)


def _ffn_rec(x, norm_g, w_gu, w_down, mix_g, w_in, lru, batch, seq):
    tile = FFN_TILE
    nt = seq // tile
    n = batch * nt
    last = n - 1

    def prev(i):
        return jnp.maximum(i - 1, 0)

    ffn_w = [norm_g, w_gu, w_down, mix_g]
    weights = ffn_w + [w_in, w_in] + list(lru)
    return pl.pallas_call(
        functools.partial(_ffn_rec_body, tiles_per_seq=nt),
        grid=(n + 1,),
        in_specs=[pl.BlockSpec((tile, D_MODEL), lambda i: (jnp.minimum(i, last), 0))]
        + [_resident(w.shape) for w in ffn_w]
        + [_col_window(D_MODEL, COL_XR, LRU_W), _col_window(D_MODEL, COL_YR, LRU_W)]
        + [_resident(w.shape) for w in lru],
        out_specs=[pl.BlockSpec((tile, D_MODEL), lambda i: (jnp.minimum(i, last), 0)),
                   pl.BlockSpec((1, tile, LRU_W), lambda i: (prev(i) // nt, prev(i) % nt, 0)),
                   pl.BlockSpec((1, CONV_W - 1, LRU_W), lambda i: (prev(i) // nt, 0, 0)),
                   pl.BlockSpec((1, 1, LRU_W), lambda i: (prev(i) // nt, 0, 0))],
        out_shape=[jax.ShapeDtypeStruct((n * tile, D_MODEL), F32),
                   jax.ShapeDtypeStruct((batch, seq, LRU_W), BF16),
                   jax.ShapeDtypeStruct((batch, CONV_W - 1, LRU_W), F32),
                   jax.ShapeDtypeStruct((batch, 1, LRU_W), F32)],
        scratch_shapes=[pltpu.VMEM((tile, D_FF), BF16),
                        pltpu.VMEM((2, tile, D_MODEL), F32),
                        pltpu.VMEM((LRU_CHUNKS, SCAN_PITCH * (SCAN_SEGS + 1), LANES), F32),
                        pltpu.VMEM((LRU_CHUNKS, SCAN_PITCH * SCAN_SEGS, LANES), F32),
                        pltpu.VMEM((1, LRU_W), F32)],
        compiler_params=_params("arbitrary"),
        name="ffn_rec",
    )(x, *weights)


def _p_proj(x1, norm_g, w_qkv):
    batch, seq, _ = x1.shape
    tile = PROJ_TILE
    nt = seq // tile
    qkv_shapes, qkv_specs = [], []
    for _kind in range(3):
        for window, dil in GROUPS:
            copies = HEADS if _kind == 0 else 1
            qkv_shapes.append(jax.ShapeDtypeStruct((batch, dil, copies * seq // dil, ATT_W), BF16))
            qkv_specs.append(pl.BlockSpec((1, dil, copies * tile // dil, ATT_W), lambda b, t: (b, 0, t, 0)))
    win_shapes, win_specs = [], []
    for window, dil in GROUPS:
        keep = min(window, seq)
        blk = min(keep, tile)
        first = nt - max(keep // tile, 1)
        for _kv in range(2):
            win_shapes.append(jax.ShapeDtypeStruct((batch, ATT_W, keep), F32))
            win_specs.append(pl.BlockSpec((1, ATT_W, blk),
                                          lambda b, t, first=first: (b, 0, jnp.maximum(t - first, 0))))
    return pl.pallas_call(
        _p_proj_body,
        grid=(batch, nt),
        in_specs=[pl.BlockSpec((1, tile, D_MODEL), lambda b, t: (b, t, 0)),
                  _resident((1, D_MODEL)), _col_window(D_MODEL, 0, QKV_W)],
        out_specs=qkv_specs + win_specs,
        out_shape=qkv_shapes + win_shapes,
        scratch_shapes=[pltpu.VMEM((3 * (len(GROUPS) - 1), ATT_CHUNKS, tile, LANES), F32)],
        compiler_params=_params("parallel", "arbitrary"),
        name="p_proj",
    )(x1, norm_g, w_qkv)


def _p_mix(x1, qkv, y_rec, mkv, biases, norm_g, w_c, b_gate, w_att_o, w_rec_o, w_mem_o, w_out):
    batch, seq, _ = x1.shape
    tile = MIX_TILE

    def tok(width):
        return pl.BlockSpec((1, tile, width), lambda b, t: (b, t, 0))

    q_specs, kv_specs, hist = [], [], []
    for window, dil in GROUPS:
        cur = tile // dil
        q_specs.append(pl.BlockSpec((1, dil, HEADS * cur, ATT_W), lambda b, t: (b, 0, t, 0)))
        kv_specs.append(pl.BlockSpec((1, dil, cur, ATT_W), lambda b, t: (b, 0, t, 0)))
        hist.append(pltpu.VMEM((dil * 2 * N_REL, ATT_W) if cur < N_REL else (2, dil * (N_REL + cur), ATT_W), BF16))
    windows = [_col_window(D_MODEL, col, FILL_COLS) for col in range(COL_QM, IN_COLS, FILL_COLS)]
    tail_w = [b_gate, w_att_o, w_rec_o, w_mem_o, w_out]
    weights = [norm_g] + [w_c] * len(windows) + tail_w
    in_specs = ([tok(D_MODEL)] + q_specs + kv_specs * 2
                + [tok(LRU_W), pl.BlockSpec((1, N_MEM, 2 * MEM_W), lambda b, t: (b, 0, 0))]
                + [_resident(a.shape) for a in list(biases) + [norm_g]] + windows
                + [_resident(a.shape) for a in tail_w])
    return pl.pallas_call(
        functools.partial(_p_mix_body, tiles_per_seq=seq // tile),
        grid=(batch, seq // tile),
        in_specs=in_specs,
        out_specs=tok(D_MODEL),
        out_shape=jax.ShapeDtypeStruct((batch, seq, D_MODEL), F32),
        scratch_shapes=hist * 2 + [pltpu.VMEM((len(GROUPS) * ATT_CHUNKS, tile, LANES), F32)] * 2,
        compiler_params=_params("parallel", "arbitrary"),
        name="p_mix",
    )(x1, *qkv, y_rec, mkv, *biases, *weights)


def _merge(x1, y_att, y_rec, y_mem, norm_g, w_c, b_gate, w_att_o, w_rec_o, w_mem_o, w_out):
    batch, seq, _ = x1.shape
    tile = seq

    def tok(width):
        return pl.BlockSpec((1, tile, width), lambda b, t: (b, t, 0))

    weights = [norm_g, w_c, b_gate, w_att_o, w_rec_o, w_mem_o, w_out]
    return pl.pallas_call(
        _merge_body,
        grid=(batch, seq // tile),
        in_specs=[tok(D_MODEL), tok(ATT_W), tok(LRU_W), tok(MEM_W)] + [_resident(w.shape) for w in weights],
        out_specs=tok(D_MODEL),
        out_shape=jax.ShapeDtypeStruct((batch, seq, D_MODEL), F32),
        compiler_params=_params("parallel", "parallel"),
        name="merge_s",
    )(x1, y_att, y_rec, y_mem, *weights)


def _s_mix(z, caches, state_conv, state_lru, cmk, cmv, bias_c, bias_n, lru):
    batch, steps, _ = z.shape

    elems = S_MIX_ELEMS if batch % S_MIX_ELEMS == 0 else 1

    def per_batch(shape):
        return pl.BlockSpec((elems,) + tuple(shape[1:]), lambda b: (b,) + (0,) * (len(shape) - 1))

    ins = [z] + list(caches) + [state_conv, state_lru, cmk, cmv]
    in_specs = [per_batch(a.shape) for a in ins]
    consts = list(bias_c) + [bias_n] + list(lru)
    in_specs += [_resident(a.shape) for a in consts]
    out_shapes = [jax.ShapeDtypeStruct((batch, steps, ATT_W), BF16),
                  jax.ShapeDtypeStruct((batch, steps, LRU_W), BF16),
                  jax.ShapeDtypeStruct((batch, steps, MEM_W), BF16)]
    out_shapes += [jax.ShapeDtypeStruct(c.shape, F32) for c in caches]
    out_shapes += [jax.ShapeDtypeStruct(state_conv.shape, F32), jax.ShapeDtypeStruct(state_lru.shape, F32)]

    def body(*refs):
        per_in, per_out = refs[:len(ins)], refs[len(ins) + len(consts):len(ins) + len(consts) + len(out_shapes)]
        const_refs, ext_ref = refs[len(ins):len(ins) + len(consts)], refs[-1]
        for e in range(elems):
            one = pl.ds(e, 1)
            _s_mix_body(*[r.at[one] for r in per_in], *const_refs, *[r.at[one] for r in per_out], ext_ref.at[e],
                        steps=steps)

    return pl.pallas_call(
        body,
        grid=(batch // elems,),
        in_specs=in_specs,
        out_specs=[per_batch(s.shape) for s in out_shapes],
        out_shape=out_shapes,
        scratch_shapes=[pltpu.VMEM((elems, 2 * SUBLANES, LRU_W), F32)],
        compiler_params=_params("parallel"),
        name="s_mix",
    )(*ins, *consts)


def _bucket(n):
    n = np.maximum(n, 0)
    max_exact = NUM_BUCKETS // 2
    nf = np.maximum(n, 1).astype(np.float32)
    large = max_exact + (np.log(nf / max_exact) / np.float32(math.log(MAX_DISTANCE / max_exact))
                         * (NUM_BUCKETS - max_exact)).astype(np.int32)
    return np.where(n < max_exact, n, np.minimum(large, NUM_BUCKETS - 1))


def _dist_bias(rel_bias, g, dist, valid):
    tab = rel_bias[:, g * HEADS:(g + 1) * HEADS].T[:, _bucket(dist)]
    return jnp.where(valid[None], tab, -jnp.inf).astype(F32)


def _toeplitz(u, rows, cols):
    lu = rows + cols - 1
    ue = jnp.concatenate([u, u[:, :1]], axis=1)
    m = jnp.tile(ue, (1, rows))[:, :rows * lu].reshape(-1, rows, lu)
    return m[:, :, rows - 1:rows - 1 + cols].reshape(-1, cols)


def _prompt_bias(rel_bias, tile, tiles_per_seq):
    ring = 2 * N_REL
    col = np.arange(ring)
    stacks = []
    for g, (_, dil) in enumerate(GROUPS):
        cur = tile // dil
        qrows = min(cur, N_REL)
        hist = ring - qrows
        delta = np.arange(-(qrows - 1), ring)
        v = _dist_bias(rel_bias, g, dil * delta, (delta >= 0) & (delta <= N_REL))
        base = _toeplitz(v[:, ::-1], qrows, ring)
        if cur >= N_REL:
            tabs = [base, jnp.where(col >= hist, base, -jnp.inf)]
        else:
            slots = ring // cur
            tabs = []
            for t in range(min(tiles_per_seq, 2 * slots)):
                cols = []
                for s in range(slots):
                    age = (t - s) % slots
                    blk = base[:, hist - cur * age:hist - cur * age + cur]
                    cols.append(blk if t - age >= 0 else jnp.full_like(blk, -jnp.inf))
                tabs.append(jnp.concatenate(cols, axis=1))
        stacks.append(jnp.stack(tabs))
    return stacks


def _sample_bias(rel_bias, steps):
    t = np.arange(steps)[:, None]
    cache, new = [], []
    for g, (window, dil) in enumerate(GROUPS):
        dist = np.arange(window + steps)
        tab = _dist_bias(rel_bias, g, dist, (dist % dil == 0) & (dist <= window))
        cache.append(_toeplitz(tab[:, 1:][:, ::-1], steps, window))
        j = np.arange(N_REL)[None, :]
        dist = t - j
        valid = (j < steps) & (dist >= 0) & (dist % dil == 0)
        new.append(_dist_bias(rel_bias, g, dist, valid).reshape(HEADS * steps, N_REL))
    return cache, jnp.stack(new)


def _block_diag(w):
    per = MXU_TILE // LRU_BLOCK
    w4 = w.reshape(-1, per, LRU_BLOCK, LRU_BLOCK)
    bd = jnp.einsum('jnef,nm->jnemf', w4, jnp.eye(per, dtype=w.dtype))
    return bd.reshape(-1, MXU_TILE, MXU_TILE).astype(BF16)


def _to_pos_minor(c):
    return jnp.transpose(c[0], (0, 2, 3, 1)).reshape(c.shape[1], ATT_W, c.shape[2])


def _from_pos_minor(c):
    return jnp.transpose(c.reshape(c.shape[0], HEADS, HEAD_DIM, c.shape[2]), (0, 3, 1, 2))[None]


def kernel(x_prompt, x_sample, mem_prompt, cache_win_k0, cache_win_v0, cache_win_k1, cache_win_v1, cache_win_k2, cache_win_v2, state_conv, state_lru, cache_mem_k, cache_mem_v, rel_bias, ffn1_norm, ffn1_w_gu, ffn1_w_down, mix_norm, w_in, b_gate, conv_w, conv_b, lru_w_a, lru_b_a, lru_w_i, lru_b_i, lru_lambda, mem_norm, w_mem_kv, w_att_o, w_rec_o, w_mem_o, w_out, ffn2_norm, ffn2_w_gu, ffn2_w_down, final_norm):
    depth = ffn1_norm.shape[0]
    assert depth == 1
    l = 0
    batch, seq, _ = x_prompt.shape
    sbatch, steps, _ = x_sample.shape
    row = lambda v: v.reshape(1, -1)

    lru = (conv_w[l], row(conv_b[l]), _block_diag(lru_w_a[l]), _block_diag(lru_w_i[l]),
           row(lru_b_a[l]), row(lru_b_i[l]), row(lru_lambda[l]))
    ffn1 = (row(ffn1_norm[l]), ffn1_w_gu[l].astype(BF16), ffn1_w_down[l].astype(BF16))
    ffn2 = (row(ffn2_norm[l]), ffn2_w_gu[l].astype(BF16), ffn2_w_down[l].astype(BF16))
    outw = (row(b_gate[l]), w_att_o[l].astype(BF16), w_rec_o[l].astype(BF16), w_mem_o[l].astype(BF16),
            w_out[l].astype(BF16))
    mixg = row(mix_norm[l])
    fin = row(final_norm)

    w_in_b = w_in[l].astype(BF16)
    xp1, y_rec, p_conv, p_lru = _ffn_rec(x_prompt.reshape(batch * seq, D_MODEL), *ffn1, mixg, w_in_b, lru, batch, seq)
    xp1 = xp1.reshape(batch, seq, D_MODEL)
    p_mk, p_mv, mkv = _mem_kv(mem_prompt.reshape(batch * N_MEM, D_MODEL), row(mem_norm[l]),
                              w_mem_kv[l].astype(BF16), row_tile=2 * N_MEM)
    proj = _p_proj(xp1, mixg, w_in_b)
    qkv, wins = proj[0:9], proj[9:15]
    xp2 = _p_mix(xp1, qkv, y_rec, mkv.reshape(batch, N_MEM, 2 * MEM_W), _prompt_bias(rel_bias, MIX_TILE, seq // MIX_TILE), mixg,
                 w_in_b, *outw)
    y_prompt = _ffn(xp2.reshape(batch * seq, D_MODEL), *ffn2, final_g=fin).reshape(batch, seq, D_MODEL)

    n_s = sbatch * steps
    xs1 = _ffn(x_sample.reshape(n_s, D_MODEL), *ffn1)
    z = _norm_mm(xs1, mixg, w_in_b, row_tile=n_s, col_tile=IN_COLS // 2)
    caches = [_to_pos_minor(c) for c in
              (cache_win_k0, cache_win_v0, cache_win_k1, cache_win_v1, cache_win_k2, cache_win_v2)]
    bias_c, bias_n = _sample_bias(rel_bias, steps)
    mem_rows = lambda c: c[l].reshape(sbatch, N_MEM * MEM_HEADS, MEM_HEAD_DIM)
    mix = _s_mix(z.reshape(sbatch, steps, IN_COLS), caches, state_conv[l], state_lru[l].reshape(sbatch, 1, LRU_W),
                 mem_rows(cache_mem_k), mem_rows(cache_mem_v), bias_c, bias_n, lru)
    s_att, s_rec, s_mem = (a.reshape(1, n_s, a.shape[-1]) for a in mix[0:3])
    xs2 = _merge(xs1.reshape(1, n_s, D_MODEL), s_att, s_rec, s_mem, mixg, w_in[l][:, COL_GATE:].astype(BF16),
                 *outw)
    y_sample = _ffn(xs2.reshape(n_s, D_MODEL), *ffn2, final_g=fin).reshape(sbatch, steps, D_MODEL)

    mem_state = lambda a: a.reshape(1, batch, N_MEM, MEM_HEADS, MEM_HEAD_DIM)
    return (y_prompt, y_sample, *[_from_pos_minor(w) for w in wins],
            p_conv[None], p_lru.reshape(1, batch, LRU_W), mem_state(p_mk), mem_state(p_mv),
            *[_from_pos_minor(w) for w in mix[3:9]], mix[9][None], mix[10].reshape(1, sbatch, LRU_W))
```

```python
import functools
import math

import numpy as np
import jax
import jax.numpy as jnp
from jax import lax
from jax.experimental import pallas as pl
from jax.experimental.pallas import tpu as pltpu

F32 = jnp.float32
BF16 = jnp.bfloat16

D_MODEL = 1024
D_FF = 2816
HEAD_DIM = 64
HEADS = 4
GROUPS = ((128, 1), (512, 4), (2048, 16))
N_REL = 128
ATT_W = HEADS * HEAD_DIM
ATT_SCALE = HEAD_DIM ** -0.5
NUM_BUCKETS = 32
MAX_DISTANCE = 2048
LRU_W = 768
LRU_BLOCK = 64
LRU_C = 8.0
CONV_W = 4
N_MEM = 256
MEM_HEADS = 4
MEM_HEAD_DIM = 128
MEM_W = MEM_HEADS * MEM_HEAD_DIM
MEM_SCALE = MEM_HEAD_DIM ** -0.5
RMS_EPS = 1e-6
QKV_W = 3 * len(GROUPS) * ATT_W
COL_XR = QKV_W
COL_YR = COL_XR + LRU_W
COL_QM = COL_YR + LRU_W
COL_GATE = COL_QM + MEM_W
IN_COLS = COL_GATE + 3 * D_MODEL

V7X_VMEM_LIMIT = 62 * 1024 * 1024
MXU_TILE = 256
SUBLANES = 8
LANES = 128
FFN_TILE = 512
FFN_CHUNK = 256
PROJ_TILE = 1024
SCAN_SEGS = SUBLANES
SCAN_STEPS = FFN_TILE // SCAN_SEGS
SCAN_PITCH = SCAN_STEPS + SUBLANES
MIX_TILE = 512
ATTN_DEPTH = 8
FILL_COLS = 256
S_MIX_ELEMS = 2
LRU_CHUNKS = LRU_W // LANES
ATT_CHUNKS = ATT_W // LANES


def _dot(a, b):
    return jnp.dot(a, b, preferred_element_type=F32)


def _dot_nt(a, b):
    return lax.dot_general(a, b, (((1,), (1,)), ((), ())), preferred_element_type=F32)


def _rms(x, g):
    return x * lax.rsqrt(jnp.mean(x * x, axis=-1, keepdims=True) + RMS_EPS) * g


def _stack_heads(q):
    lane = lax.broadcasted_iota(jnp.int32, q.shape, 1)
    return jnp.concatenate(
        [jnp.where((lane >= h * HEAD_DIM) & (lane < (h + 1) * HEAD_DIM), q, 0.0) for h in range(HEADS)], axis=0)


def _unstack_heads(x, rows):
    low = lax.broadcasted_iota(jnp.int32, (rows, LANES), 1) < HEAD_DIM
    per = LANES // HEAD_DIM
    return [jnp.where(low,
                      x[(per * c) * rows:(per * c + 1) * rows, c * LANES:(c + 1) * LANES],
                      x[(per * c + 1) * rows:(per * c + 2) * rows, c * LANES:(c + 1) * LANES])
            for c in range(ATT_CHUNKS)]


def _linear_scan(a, u, rows):
    row = lax.broadcasted_iota(jnp.int32, (rows, 1), 0)
    d = 1
    while d < rows:
        a_s = pltpu.roll(a, d, 0)
        u_s = pltpu.roll(u, d, 0)
        keep = row >= d
        u = jnp.where(keep, a * u_s + u, u)
        a = jnp.where(keep, a * a_s, a)
        d *= 2
    return a, u


def _zero_after(x):
    bits = pltpu.bitcast(x, jnp.uint32)
    half = jnp.uint32(17)
    return pltpu.bitcast(lax.shift_right_logical(lax.shift_right_logical(bits, half), half), F32)


def _lru_preact(xc, wa_ref, wi_ref):
    xcb = xc.astype(BF16)
    ra, ia = [], []
    for j in range(LRU_W // MXU_TILE):
        blk = xcb[:, j * MXU_TILE:(j + 1) * MXU_TILE]
        ra.append(_dot(blk, wa_ref[j]))
        ia.append(_dot(blk, wi_ref[j]))
    return jnp.concatenate(ra, axis=-1), jnp.concatenate(ia, axis=-1)


def _lru_decay_input(xc, ra, ia, ba_ref, bi_ref, lam_ref):
    r = jax.nn.sigmoid(ra + ba_ref[...])
    i = jax.nn.sigmoid(ia + bi_ref[...])
    nl = -lam_ref[...]
    softplus = jnp.maximum(nl, 0.0) + jnp.log1p(jnp.exp(-jnp.abs(nl)))
    log_a = (-LRU_C * r) * softplus
    a = jnp.exp(log_a)
    u = jnp.sqrt(1.0 - a * a) * (i * xc)
    return a, u


def _lru_gates(xc, wa_ref, wi_ref, ba_ref, bi_ref, lam_ref):
    return _lru_decay_input(xc, *_lru_preact(xc, wa_ref, wi_ref), ba_ref, bi_ref, lam_ref)


def _conv_taps(cw_ref, cb_ref, taps):
    xc = cb_ref[...] + cw_ref[CONV_W - 1:CONV_W, :] * taps[0]
    for j in range(1, CONV_W):
        xc = xc + cw_ref[CONV_W - 1 - j:CONV_W - j, :] * taps[j]
    return xc


def _mem_attention(qm, mk_heads, mv_heads):
    heads = []
    for h in range(MEM_HEADS):
        q = qm[:, h * MEM_HEAD_DIM:(h + 1) * MEM_HEAD_DIM].astype(BF16)
        s = _dot_nt(q, mk_heads[h]) * MEM_SCALE
        m = jnp.max(s, axis=-1, keepdims=True)
        p = jnp.exp(s - m)
        l = jnp.sum(p, axis=-1, keepdims=True)
        heads.append(_dot(p.astype(BF16), mv_heads[h]) * (1.0 / l))
    return jnp.concatenate(heads, axis=-1)


def _ffn_body(x_ref, g_ref, wgu_ref, wd_ref, *rest, final_norm):
    if final_norm:
        fg_ref, o_ref, act_ref = rest
    else:
        o_ref, act_ref = rest
    x = x_ref[...]
    hn = _rms(x, g_ref[...]).astype(BF16)
    for c in range(D_FF // FFN_CHUNK):
        lo = c * FFN_CHUNK
        gate = _dot(hn, wgu_ref[:, lo:lo + FFN_CHUNK])
        up = _dot(hn, wgu_ref[:, D_FF + lo:D_FF + lo + FFN_CHUNK])
        act_ref[:, lo:lo + FFN_CHUNK] = (gate * jax.nn.sigmoid(gate) * up).astype(BF16)
    y = x + 0.5 * _dot(act_ref[...], wd_ref[...])
    if final_norm:
        y = _rms(y, fg_ref[...])
    o_ref[...] = y


def _norm_mm_body(x_ref, g_ref, w_ref, o_ref):
    hn = _rms(x_ref[...], g_ref[...]).astype(BF16)
    o_ref[...] = _dot(hn, w_ref[...])


def _mem_kv_body(x_ref, g_ref, w_ref, mk_ref, mv_ref, mkv_ref):
    rows = x_ref.shape[0]
    z = _dot(_rms(x_ref[...], g_ref[...]).astype(BF16), w_ref[...])
    mkv_ref[...] = z.astype(BF16)
    for h in range(MEM_HEADS):
        dst = pl.ds(h, rows, stride=MEM_HEADS)
        mk_ref[dst, :] = z[:, h * MEM_HEAD_DIM:(h + 1) * MEM_HEAD_DIM]
        mv_ref[dst, :] = z[:, MEM_W + h * MEM_HEAD_DIM:MEM_W + (h + 1) * MEM_HEAD_DIM]


def _ffn_rec_body(x_ref, g_ref, wgu_ref, wd_ref, mg_ref, wxr_ref, wyr_ref, cw_ref, cb_ref, wa_ref, wi_ref, ba_ref, bi_ref,
                  lam_ref, o_ref, yrec_ref, pconv_ref, plru_ref, act_ref, xbuf_ref, xpad_ref, hpad_ref, hc_ref,
                  *, tiles_per_seq):
    rows = FFN_TILE
    i = pl.program_id(0)
    t = lax.rem(jnp.maximum(i - 1, 0), tiles_per_seq)
    slot = lax.rem(i, 2)
    tail = slice(SCAN_STEPS - SUBLANES, SCAN_STEPS)

    @pl.when(i == 0)
    def _():
        xbuf_ref[...] = jnp.zeros(xbuf_ref.shape, F32)

    @pl.when(t == 0)
    def _():
        xpad_ref[:, tail, :] = jnp.zeros((LRU_CHUNKS, SUBLANES, LANES), F32)
        hc_ref[...] = jnp.zeros((1, LRU_W), F32)

    hn_p = _rms(xbuf_ref[1 - slot], mg_ref[...]).astype(BF16)
    xr = _dot(hn_p, wxr_ref[...])
    for c in range(LRU_CHUNKS):
        for s in range(SCAN_SEGS):
            lo = SCAN_PITCH * (s + 1)
            xpad_ref[c, lo:lo + SCAN_STEPS, :] = xr[s * SCAN_STEPS:(s + 1) * SCAN_STEPS, c * LANES:(c + 1) * LANES]

    x = x_ref[...]
    hn = _rms(x, g_ref[...]).astype(BF16)

    def ffn_chunk(c):
        lo = c * FFN_CHUNK
        gate = _dot(hn, wgu_ref[:, lo:lo + FFN_CHUNK])
        up = _dot(hn, wgu_ref[:, D_FF + lo:D_FF + lo + FFN_CHUNK])
        act_ref[:, lo:lo + FFN_CHUNK] = (gate * jax.nn.sigmoid(gate) * up).astype(BF16)
        return gate

    n_chunks = D_FF // FFN_CHUNK
    lead = 2
    for c in range(lead):
        ffn_chunk(c)

    def step_rows(j):
        start = SCAN_PITCH + j if j >= 0 else SCAN_STEPS + j
        sel = pl.ds(start, SCAN_SEGS, stride=SCAN_PITCH)
        return jnp.concatenate([xpad_ref[c, sel, :] for c in range(LRU_CHUNKS)], axis=-1)

    hist = CONV_W - 1
    xp = jnp.concatenate([step_rows(j) for j in range(-hist, SCAN_STEPS)], axis=0)
    n = SCAN_STEPS * SCAN_SEGS
    taps = [xp[(hist - j) * SCAN_SEGS:(hist - j) * SCAN_SEGS + n, :] for j in range(CONV_W)]
    xc = _conv_taps(cw_ref, cb_ref, taps)
    ra, ia = _lru_preact(xc, wa_ref, wi_ref)

    yr = _dot(hn_p, wyr_ref[...])
    a_parts, u_parts = [], []
    per = n // SCAN_SEGS
    for k in range(SCAN_SEGS):
        gate = ffn_chunk(lead + k)
        tie = jnp.concatenate([_zero_after(gate[0:per, 0:LANES])] * LRU_CHUNKS, axis=-1)
        sl = slice(k * per, (k + 1) * per)
        a_k, u_k = _lru_decay_input(xc[sl] + tie, ra[sl], ia[sl], ba_ref, bi_ref, lam_ref)
        a_parts.append(a_k)
        u_parts.append(u_k)
    a = jnp.concatenate(a_parts, axis=0)
    u = jnp.concatenate(u_parts, axis=0)
    for c in range(lead + SCAN_SEGS, n_chunks):
        ffn_chunk(c)

    def rows_of(v, j):
        return v[j * SCAN_SEGS:(j + 1) * SCAN_SEGS, :]

    h = jnp.zeros((SCAN_SEGS, LRU_W), F32)
    prod = jnp.ones((SCAN_SEGS, LRU_W), F32)
    for j in range(SCAN_STEPS):
        a_j = rows_of(a, j)
        h = a_j * h + rows_of(u, j)
        prod = a_j * prod
    prod_c, h_c = _linear_scan(prod, h, SCAN_SEGS)
    h_end = prod_c * hc_ref[...] + h_c
    seg = lax.broadcasted_iota(jnp.int32, (SCAN_SEGS, 1), 0)
    h = jnp.where(seg == 0, hc_ref[...], pltpu.roll(h_end, 1, 0))
    for j in range(SCAN_STEPS):
        h = rows_of(a, j) * h + rows_of(u, j)
        dst = pl.ds(j, SCAN_SEGS, stride=SCAN_PITCH)
        for c in range(LRU_CHUNKS):
            hpad_ref[c, dst, :] = h[:, c * LANES:(c + 1) * LANES]
    hs = jnp.concatenate(
        [jnp.concatenate([hpad_ref[c, SCAN_PITCH * s:SCAN_PITCH * s + SCAN_STEPS, :] for c in range(LRU_CHUNKS)],
                         axis=-1) for s in range(SCAN_SEGS)], axis=0)
    yrec_ref[0] = (jax.nn.gelu(yr) * hs).astype(BF16)
    hc_ref[...] = h_end[SCAN_SEGS - 1:SCAN_SEGS, :]
    plru_ref[0] = h_end[SCAN_SEGS - 1:SCAN_SEGS, :]
    pconv_ref[0] = xr[rows - hist:rows, :]
    for c in range(LRU_CHUNKS):
        xpad_ref[c, tail, :] = xr[rows - SUBLANES:rows, c * LANES:(c + 1) * LANES]

    y = x + 0.5 * _dot(act_ref[...], wd_ref[...])
    o_ref[...] = y
    xbuf_ref[slot] = y


def _p_proj_body(x_ref, g_ref, w_ref, q0, q1, q2, k0, k1, k2, v0, v1, v2, pk0, pv0, pk1, pv1, pk2, pv2, zs_ref):
    rows = PROJ_TILE
    hn = _rms(x_ref[0], g_ref[...]).astype(BF16)
    outs = ((q0, q1, q2), (k0, k1, k2), (v0, v1, v2))
    wins = (None, (pk0, pk1, pk2), (pv0, pv1, pv2))
    for g, (window, dil) in reversed(list(enumerate(GROUPS))):
        for kind in (1, 2, 0):
            col = (kind * len(GROUPS) + g) * ATT_W
            z = _dot(hn, w_ref[:, col:col + ATT_W])
            if kind == 0:
                z = z * ATT_SCALE
            else:
                keep = min(window, rows)
                wins[kind][g][0] = jnp.transpose(z[rows - keep:, :])
            def emit(r, part):
                if kind == 0:
                    qrows = min(MIX_TILE // dil, N_REL)
                    part = jnp.concatenate([_stack_heads(part[lo:lo + qrows, :])
                                            for lo in range(0, rows // dil, qrows)], axis=0)
                outs[kind][g][0, r] = part.astype(BF16)

            if dil == 1:
                emit(0, z)
            else:
                stage = zs_ref.at[kind * (len(GROUPS) - 1) + g - 1]
                for c in range(ATT_CHUNKS):
                    stage[c] = z[:, c * LANES:(c + 1) * LANES]
                for r in range(dil):
                    sel = pl.ds(r, rows // dil, stride=dil)
                    emit(r, jnp.concatenate([stage[c, sel, :] for c in range(ATT_CHUNKS)], axis=-1))


def _softmax_parts(s):
    m = jnp.max(s, axis=-1, keepdims=True)
    p = jnp.exp(s - m)
    return p.astype(BF16), jnp.sum(p, axis=-1, keepdims=True), m


def _gate(hn, wc_ref, bg_ref, gate_col, idx):
    lo = idx * D_MODEL
    return jax.nn.sigmoid(_dot(hn, wc_ref[:, gate_col + lo:gate_col + lo + D_MODEL]) + bg_ref[:, lo:lo + D_MODEL])


def _p_mix_body(x_ref, q0, q1, q2, k0, k1, k2, v0, v1, v2, yr_ref, mkv_ref, b0, b1, b2, g_ref, *rest,
                tiles_per_seq):
    n_win = (IN_COLS - COL_QM) // FILL_COLS
    wc_refs = rest[:n_win]
    bg_ref, wao_ref, wro_ref, wmo_ref, wout_ref, o_ref, kh0, kh1, kh2, vh0, vh1, vh2, o_scr, l_scr = rest[n_win:]
    qm_wins = MEM_W // FILL_COLS
    tile = MIX_TILE
    t = pl.program_id(1)
    slot = lax.rem(t, 2)
    qs_, ks_, vs_ = (q0, q1, q2), (k0, k1, k2), (v0, v1, v2)
    khs, vhs, biases = (kh0, kh1, kh2), (vh0, vh1, vh2), (b0, b1, b2)
    ring = 2 * N_REL

    def in_ring(g):
        return tile // GROUPS[g][1] < N_REL

    @pl.when(t == 0)
    def _():
        for g, (window, dil) in enumerate(GROUPS):
            for h_ref in (khs[g], vhs[g]):
                if in_ring(g):
                    h_ref[...] = jnp.zeros(h_ref.shape, BF16)
                else:
                    cur = tile // dil
                    for r in range(dil):
                        h_ref[0, r * (N_REL + cur):r * (N_REL + cur) + N_REL, :] = jnp.zeros((N_REL, ATT_W), BF16)

    res = {}

    windows, carried = [], []

    def window_block(g, dil, r, lo, qrows, keys, tab, new_rows):
        def rows_of(h_ref, sel):
            return h_ref.at[sel] if in_ring(g) else h_ref.at[slot, sel]

        def scores():
            if new_rows is not None:
                rows_of(khs[g], new_rows)[...] = ks_[g][0, r]
                rows_of(vhs[g], new_rows)[...] = vs_[g][0, r]
            q = qs_[g][0, r, HEADS * lo:HEADS * (lo + qrows), :]
            return _softmax_parts(_dot_nt(q, rows_of(khs[g], keys)[...]) + tab)

        def values(parts):
            p, l, m = parts
            o = _dot(p, rows_of(vhs[g], keys)[...]) * (1.0 / l)
            lse = jnp.broadcast_to(m + jnp.log(l), o.shape)
            o_c, l_c = _unstack_heads(o, qrows), _unstack_heads(lse, qrows)
            dst = slice(lo, lo + qrows) if dil == 1 else pl.ds(lo * dil + r, qrows, stride=dil)
            for c in range(ATT_CHUNKS):
                o_scr[g * ATT_CHUNKS + c, dst, :] = o_c[c]
                l_scr[g * ATT_CHUNKS + c, dst, :] = l_c[c]
        return scores, values

    for g, (window, dil) in enumerate(GROUPS):
        cur = tile // dil
        if in_ring(g):
            slots = ring // cur
            tabs = biases[g].shape[0]
            tab = biases[g][t if tabs >= tiles_per_seq else jnp.where(t < slots, t, slots + lax.rem(t, slots))]
            newest = lax.rem(t, slots) * cur
            for r in range(dil):
                base = r * ring
                windows.append(window_block(g, dil, r, 0, cur, slice(base, base + ring), tab,
                                            pl.ds(pl.multiple_of(base + newest, cur), cur)))
        else:
            for r in range(dil):
                base = r * (N_REL + cur)
                carried.append((g, base, cur))
                for j in range(cur // N_REL):
                    tab = biases[g][jnp.where(t == 0, 1, 0)] if j == 0 else biases[g][0]
                    lo = j * N_REL
                    windows.append(window_block(g, dil, r, lo, N_REL, slice(base + lo, base + lo + ring), tab,
                                                slice(base + N_REL, base + N_REL + cur) if j == 0 else None))

    def memory_block(h):
        sl = slice(h * MEM_HEAD_DIM, (h + 1) * MEM_HEAD_DIM)

        def scores():
            return _softmax_parts(_dot_nt(res["qm"][:, sl].astype(BF16), mkv_ref[0, :, sl]) * MEM_SCALE)

        def values(parts):
            p, l, _ = parts
            res["mem", h] = _dot(p, mkv_ref[0, :, MEM_W + h * MEM_HEAD_DIM:MEM_W + (h + 1) * MEM_HEAD_DIM]) * (1.0 / l)
        return scores, values

    def normed_input():
        res["x"] = x_ref[0]
        res["hn"] = _rms(res["x"], g_ref[...]).astype(BF16)
        res["qm"] = jnp.concatenate([_dot(res["hn"], w[...]) for w in wc_refs[:qm_wins]], axis=-1)

    def gate_piece(idx, piece):
        def emit():
            lo = idx * D_MODEL + piece * FILL_COLS
            cols = slice(piece * FILL_COLS, (piece + 1) * FILL_COLS)
            gate = jax.nn.sigmoid(_dot(res["hn"], wc_refs[qm_wins + lo // FILL_COLS][...])
                                  + bg_ref[:, lo:lo + FILL_COLS])
            res["gate", idx, piece] = gate * _dot(yr_ref[0], wro_ref[:, cols]) if idx == 1 else gate
        return emit

    normed_input()
    blocks = windows + [memory_block(h) for h in range(MEM_HEADS)]
    pieces = D_MODEL // FILL_COLS
    fillers = [gate_piece(idx, piece) for idx in (1, 0, 2) for piece in range(pieces)]
    every = max(1, len(blocks) // len(fillers))

    pending = []
    for i, (scores, values) in enumerate(blocks):
        pending.append((values, scores()))
        if len(pending) > ATTN_DEPTH:
            done, parts = pending.pop(0)
            done(parts)
        if fillers and (i + 1) % every == 0:
            fillers.pop(0)()
    for done, parts in pending:
        done(parts)
    for emit in fillers:
        emit()
    gates = [jnp.concatenate([res["gate", idx, piece] for piece in range(pieces)], axis=-1) for idx in range(3)]

    mixed = []
    for c in range(ATT_CHUNKS):
        l0, l1, l2 = (l_scr[g * ATT_CHUNKS + c] for g in range(len(GROUPS)))
        o0, o1, o2 = (o_scr[g * ATT_CHUNKS + c] for g in range(len(GROUPS)))
        m = jnp.maximum(jnp.maximum(l0, l1), l2)
        w0, w1, w2 = jnp.exp(l0 - m), jnp.exp(l1 - m), jnp.exp(l2 - m)
        mixed.append((w0 * o0 + w1 * o1 + w2 * o2) / (w0 + w1 + w2))
    ya = jnp.concatenate(mixed, axis=-1).astype(BF16)
    ym = jnp.concatenate([res["mem", h] for h in range(MEM_HEADS)], axis=-1).astype(BF16)
    merged = gates[0] * _dot(ya, wao_ref[...]) + gates[1] + gates[2] * _dot(ym, wmo_ref[...])
    o_ref[0] = res["x"] + _dot(merged.astype(BF16), wout_ref[...])
    for g, base, cur in carried:
        for h_ref in (khs[g], vhs[g]):
            h_ref[1 - slot, base:base + N_REL, :] = h_ref[slot, base + cur:base + cur + N_REL, :]


def _merge_out(x, hn, branches, wc_ref, gate_col, bg_ref, wout_ref):
    merged = None
    for idx, (y, w_ref) in enumerate(branches):
        term = _gate(hn, wc_ref, bg_ref, gate_col, idx) * _dot(y, w_ref[...])
        merged = term if merged is None else merged + term
    return x + _dot(merged.astype(BF16), wout_ref[...])


def _merge_body(x_ref, ya_ref, yr_ref, ym_ref, g_ref, wc_ref, bg_ref, wao_ref, wro_ref, wmo_ref, wout_ref, o_ref):
    x = x_ref[0]
    hn = _rms(x, g_ref[...]).astype(BF16)
    o_ref[0] = _merge_out(x, hn, ((ya_ref[0], wao_ref), (yr_ref[0], wro_ref), (ym_ref[0], wmo_ref)),
                          wc_ref, 0, bg_ref, wout_ref)


def _s_mix_body(z_ref, ck0, cv0, ck1, cv1, ck2, cv2, sconv_ref, slru_ref, cmk_ref, cmv_ref,
                bc0, bc1, bc2, bn_ref, cw_ref, cb_ref, wa_ref, wi_ref, ba_ref, bi_ref, lam_ref,
                ya_ref, yrec_ref, ym_ref, ok0, ov0, ok1, ov1, ok2, ov2, oconv_ref, olru_ref,
                ext_ref, *, steps):
    cks, cvs = (ck0, ck1, ck2), (cv0, cv1, cv2)
    oks, ovs = (ok0, ok1, ok2), (ov0, ov1, ov2)
    bcs = (bc0, bc1, bc2)
    zeros = jnp.zeros((N_REL - steps, ATT_W), F32)
    new_lanes = lax.broadcasted_iota(jnp.int32, (ATT_W, LANES), 1) >= LANES - steps
    outs, lses = [], []
    for g, (window, dil) in enumerate(GROUPS):
        q = z_ref[0, :, g * ATT_W:(g + 1) * ATT_W] * ATT_SCALE
        kn = z_ref[0, :, (3 + g) * ATT_W:(4 + g) * ATT_W]
        vn = z_ref[0, :, (6 + g) * ATT_W:(7 + g) * ATT_W]
        qs = _stack_heads(q).astype(BF16)
        kt = cks[g][0]
        vt = cvs[g][0]
        s_c = _dot(qs, kt.astype(BF16)) + bcs[g][...]
        s_n = _dot_nt(qs, jnp.concatenate([kn, zeros], axis=0).astype(BF16)) + bn_ref[g]
        m = jnp.maximum(jnp.max(s_c, axis=-1, keepdims=True), jnp.max(s_n, axis=-1, keepdims=True))
        p_c = jnp.exp(s_c - m)
        p_n = jnp.exp(s_n - m)
        l = jnp.sum(p_c, axis=-1, keepdims=True) + jnp.sum(p_n, axis=-1, keepdims=True)
        o = (_dot_nt(p_c.astype(BF16), vt.astype(BF16))
             + _dot(p_n.astype(BF16), jnp.concatenate([vn, zeros], axis=0).astype(BF16))) * (1.0 / l)
        lse = jnp.broadcast_to(m + jnp.log(l), o.shape)
        outs.append(jnp.concatenate(_unstack_heads(o, steps), axis=-1))
        lses.append(jnp.concatenate(_unstack_heads(lse, steps), axis=-1))
        for old, new, dst in ((kt, kn, oks[g]), (vt, vn, ovs[g])):
            rolled = pltpu.roll(old, window - steps, 1)
            new_t = jnp.transpose(jnp.concatenate([zeros, new], axis=0))
            if window > LANES:
                dst[0, :, 0:window - LANES] = rolled[:, 0:window - LANES]
            dst[0, :, window - LANES:window] = jnp.where(new_lanes, new_t, rolled[:, window - LANES:window])
    m = jnp.maximum(jnp.maximum(lses[0], lses[1]), lses[2])
    ws = [jnp.exp(l - m) for l in lses]
    ya_ref[0] = ((ws[0] * outs[0] + ws[1] * outs[1] + ws[2] * outs[2]) / (ws[0] + ws[1] + ws[2])).astype(BF16)

    hist = CONV_W - 1
    ext_ref[0:SUBLANES, :] = jnp.zeros((SUBLANES, LRU_W), F32)
    ext_ref[SUBLANES - hist:SUBLANES, :] = sconv_ref[0]
    ext_ref[SUBLANES:SUBLANES + steps, :] = z_ref[0, :, COL_XR:COL_XR + LRU_W]
    taps = [ext_ref[SUBLANES - j:SUBLANES - j + steps, :] for j in range(CONV_W)]
    a, u = _lru_gates(_conv_taps(cw_ref, cb_ref, taps), wa_ref, wi_ref, ba_ref, bi_ref, lam_ref)
    a_c, u_c = _linear_scan(a, u, steps)
    h = a_c * slru_ref[0] + u_c
    yrec_ref[0] = (jax.nn.gelu(z_ref[0, :, COL_YR:COL_YR + LRU_W]) * h).astype(BF16)
    olru_ref[0] = h[steps - 1:steps, :]
    oconv_ref[0] = ext_ref[SUBLANES + steps - hist:SUBLANES + steps, :]

    qm = z_ref[0, :, COL_QM:COL_QM + MEM_W]
    mk = [cmk_ref[0, pl.ds(h, N_MEM, stride=MEM_HEADS), :].astype(BF16) for h in range(MEM_HEADS)]
    mv = [cmv_ref[0, pl.ds(h, N_MEM, stride=MEM_HEADS), :].astype(BF16) for h in range(MEM_HEADS)]
    ym_ref[0] = _mem_attention(qm, mk, mv).astype(BF16)


def _resident(shape):
    zeros = (0,) * len(shape)
    return pl.BlockSpec(shape, lambda *_: zeros, pipeline_mode=pl.Buffered(1))


def _col_window(rows, first_col, cols):
    assert first_col % cols == 0
    block = first_col // cols
    return pl.BlockSpec((rows, cols), lambda *_: (0, block), pipeline_mode=pl.Buffered(1))


def _params(*semantics):
    return pltpu.CompilerParams(dimension_semantics=semantics, vmem_limit_bytes=V7X_VMEM_LIMIT)


def _ffn(x, norm_g, w_gu, w_down, final_g=None):
    n = x.shape[0]
    tile = min(FFN_TILE, n)
    row = pl.BlockSpec((tile, D_MODEL), lambda i: (i, 0))
    in_specs = [row, _resident((1, D_MODEL)), _resident(w_gu.shape), _resident(w_down.shape)]
    args = [x, norm_g, w_gu, w_down]
    if final_g is not None:
        in_specs.append(_resident((1, D_MODEL)))
        args.append(final_g)
    return pl.pallas_call(
        functools.partial(_ffn_body, final_norm=final_g is not None),
        grid=(n // tile,),
        in_specs=in_specs,
        out_specs=row,
        out_shape=jax.ShapeDtypeStruct((n, D_MODEL), F32),
        scratch_shapes=[pltpu.VMEM((tile, D_FF), BF16)],
        compiler_params=_params("parallel"),
        name="ffn_final" if final_g is not None else "ffn",
    )(*args)


def _norm_mm(x, norm_g, w, row_tile, col_tile):
    n, cols = x.shape[0], w.shape[1]
    return pl.pallas_call(
        _norm_mm_body,
        grid=(n // row_tile, cols // col_tile),
        in_specs=[pl.BlockSpec((row_tile, D_MODEL), lambda i, j: (i, 0)),
                  _resident((1, D_MODEL)),
                  pl.BlockSpec((D_MODEL, col_tile), lambda i, j: (0, j))],
        out_specs=pl.BlockSpec((row_tile, col_tile), lambda i, j: (i, j)),
        out_shape=jax.ShapeDtypeStruct((n, cols), F32),
        compiler_params=_params("parallel", "parallel"),
        name="norm_mm",
    )(x, norm_g, w)


def _mem_kv(mem, norm_g, w_kv, row_tile):
    n = mem.shape[0]
    state = jax.ShapeDtypeStruct((n * MEM_HEADS, MEM_HEAD_DIM), F32)
    state_spec = pl.BlockSpec((row_tile * MEM_HEADS, MEM_HEAD_DIM), lambda i: (i, 0))
    return pl.pallas_call(
        _mem_kv_body,
        grid=(n // row_tile,),
        in_specs=[pl.BlockSpec((row_tile, D_MODEL), lambda i: (i, 0)),
                  _resident((1, D_MODEL)), _resident(w_kv.shape)],
        out_specs=[state_spec, state_spec, pl.BlockSpec((row_tile, 2 * MEM_W), lambda i: (i, 0))],
        out_shape=[state, state, jax.ShapeDtypeStruct((n, 2 * MEM_W), BF16)],
        compiler_params=_params("parallel"),
        name="mem_kv",
    )(mem, norm_g, w_kv)


def _ffn_rec(x, norm_g, w_gu, w_down, mix_g, w_in, lru, batch, seq):
    tile = FFN_TILE
    nt = seq // tile
    n = batch * nt
    last = n - 1

    def prev(i):
        return jnp.maximum(i - 1, 0)

    ffn_w = [norm_g, w_gu, w_down, mix_g]
    weights = ffn_w + [w_in, w_in] + list(lru)
    return pl.pallas_call(
        functools.partial(_ffn_rec_body, tiles_per_seq=nt),
        grid=(n + 1,),
        in_specs=[pl.BlockSpec((tile, D_MODEL), lambda i: (jnp.minimum(i, last), 0))]
        + [_resident(w.shape) for w in ffn_w]
        + [_col_window(D_MODEL, COL_XR, LRU_W), _col_window(D_MODEL, COL_YR, LRU_W)]
        + [_resident(w.shape) for w in lru],
        out_specs=[pl.BlockSpec((tile, D_MODEL), lambda i: (jnp.minimum(i, last), 0)),
                   pl.BlockSpec((1, tile, LRU_W), lambda i: (prev(i) // nt, prev(i) % nt, 0)),
                   pl.BlockSpec((1, CONV_W - 1, LRU_W), lambda i: (prev(i) // nt, 0, 0)),
                   pl.BlockSpec((1, 1, LRU_W), lambda i: (prev(i) // nt, 0, 0))],
        out_shape=[jax.ShapeDtypeStruct((n * tile, D_MODEL), F32),
                   jax.ShapeDtypeStruct((batch, seq, LRU_W), BF16),
                   jax.ShapeDtypeStruct((batch, CONV_W - 1, LRU_W), F32),
                   jax.ShapeDtypeStruct((batch, 1, LRU_W), F32)],
        scratch_shapes=[pltpu.VMEM((tile, D_FF), BF16),
                        pltpu.VMEM((2, tile, D_MODEL), F32),
                        pltpu.VMEM((LRU_CHUNKS, SCAN_PITCH * (SCAN_SEGS + 1), LANES), F32),
                        pltpu.VMEM((LRU_CHUNKS, SCAN_PITCH * SCAN_SEGS, LANES), F32),
                        pltpu.VMEM((1, LRU_W), F32)],
        compiler_params=_params("arbitrary"),
        name="ffn_rec",
    )(x, *weights)


def _p_proj(x1, norm_g, w_qkv):
    batch, seq, _ = x1.shape
    tile = PROJ_TILE
    nt = seq // tile
    qkv_shapes, qkv_specs = [], []
    for _kind in range(3):
        for window, dil in GROUPS:
            copies = HEADS if _kind == 0 else 1
            qkv_shapes.append(jax.ShapeDtypeStruct((batch, dil, copies * seq // dil, ATT_W), BF16))
            qkv_specs.append(pl.BlockSpec((1, dil, copies * tile // dil, ATT_W), lambda b, t: (b, 0, t, 0)))
    win_shapes, win_specs = [], []
    for window, dil in GROUPS:
        keep = min(window, seq)
        blk = min(keep, tile)
        first = nt - max(keep // tile, 1)
        for _kv in range(2):
            win_shapes.append(jax.ShapeDtypeStruct((batch, ATT_W, keep), F32))
            win_specs.append(pl.BlockSpec((1, ATT_W, blk),
                                          lambda b, t, first=first: (b, 0, jnp.maximum(t - first, 0))))
    return pl.pallas_call(
        _p_proj_body,
        grid=(batch, nt),
        in_specs=[pl.BlockSpec((1, tile, D_MODEL), lambda b, t: (b, t, 0)),
                  _resident((1, D_MODEL)), _col_window(D_MODEL, 0, QKV_W)],
        out_specs=qkv_specs + win_specs,
        out_shape=qkv_shapes + win_shapes,
        scratch_shapes=[pltpu.VMEM((3 * (len(GROUPS) - 1), ATT_CHUNKS, tile, LANES), F32)],
        compiler_params=_params("parallel", "arbitrary"),
        name="p_proj",
    )(x1, norm_g, w_qkv)


def _p_mix(x1, qkv, y_rec, mkv, biases, norm_g, w_c, b_gate, w_att_o, w_rec_o, w_mem_o, w_out):
    batch, seq, _ = x1.shape
    tile = MIX_TILE

    def tok(width):
        return pl.BlockSpec((1, tile, width), lambda b, t: (b, t, 0))

    q_specs, kv_specs, hist = [], [], []
    for window, dil in GROUPS:
        cur = tile // dil
        q_specs.append(pl.BlockSpec((1, dil, HEADS * cur, ATT_W), lambda b, t: (b, 0, t, 0)))
        kv_specs.append(pl.BlockSpec((1, dil, cur, ATT_W), lambda b, t: (b, 0, t, 0)))
        hist.append(pltpu.VMEM((dil * 2 * N_REL, ATT_W) if cur < N_REL else (2, dil * (N_REL + cur), ATT_W), BF16))
    windows = [_col_window(D_MODEL, col, FILL_COLS) for col in range(COL_QM, IN_COLS, FILL_COLS)]
    tail_w = [b_gate, w_att_o, w_rec_o, w_mem_o, w_out]
    weights = [norm_g] + [w_c] * len(windows) + tail_w
    in_specs = ([tok(D_MODEL)] + q_specs + kv_specs * 2
                + [tok(LRU_W), pl.BlockSpec((1, N_MEM, 2 * MEM_W), lambda b, t: (b, 0, 0))]
                + [_resident(a.shape) for a in list(biases) + [norm_g]] + windows
                + [_resident(a.shape) for a in tail_w])
    return pl.pallas_call(
        functools.partial(_p_mix_body, tiles_per_seq=seq // tile),
        grid=(batch, seq // tile),
        in_specs=in_specs,
        out_specs=tok(D_MODEL),
        out_shape=jax.ShapeDtypeStruct((batch, seq, D_MODEL), F32),
        scratch_shapes=hist * 2 + [pltpu.VMEM((len(GROUPS) * ATT_CHUNKS, tile, LANES), F32)] * 2,
        compiler_params=_params("parallel", "arbitrary"),
        name="p_mix",
    )(x1, *qkv, y_rec, mkv, *biases, *weights)


def _merge(x1, y_att, y_rec, y_mem, norm_g, w_c, b_gate, w_att_o, w_rec_o, w_mem_o, w_out):
    batch, seq, _ = x1.shape
    tile = seq

    def tok(width):
        return pl.BlockSpec((1, tile, width), lambda b, t: (b, t, 0))

    weights = [norm_g, w_c, b_gate, w_att_o, w_rec_o, w_mem_o, w_out]
    return pl.pallas_call(
        _merge_body,
        grid=(batch, seq // tile),
        in_specs=[tok(D_MODEL), tok(ATT_W), tok(LRU_W), tok(MEM_W)] + [_resident(w.shape) for w in weights],
        out_specs=tok(D_MODEL),
        out_shape=jax.ShapeDtypeStruct((batch, seq, D_MODEL), F32),
        compiler_params=_params("parallel", "parallel"),
        name="merge_s",
    )(x1, y_att, y_rec, y_mem, *weights)


def _s_mix(z, caches, state_conv, state_lru, cmk, cmv, bias_c, bias_n, lru):
    batch, steps, _ = z.shape

    elems = S_MIX_ELEMS if batch % S_MIX_ELEMS == 0 else 1

    def per_batch(shape):
        return pl.BlockSpec((elems,) + tuple(shape[1:]), lambda b: (b,) + (0,) * (len(shape) - 1))

    ins = [z] + list(caches) + [state_conv, state_lru, cmk, cmv]
    in_specs = [per_batch(a.shape) for a in ins]
    consts = list(bias_c) + [bias_n] + list(lru)
    in_specs += [_resident(a.shape) for a in consts]
    out_shapes = [jax.ShapeDtypeStruct((batch, steps, ATT_W), BF16),
                  jax.ShapeDtypeStruct((batch, steps, LRU_W), BF16),
                  jax.ShapeDtypeStruct((batch, steps, MEM_W), BF16)]
    out_shapes += [jax.ShapeDtypeStruct(c.shape, F32) for c in caches]
    out_shapes += [jax.ShapeDtypeStruct(state_conv.shape, F32), jax.ShapeDtypeStruct(state_lru.shape, F32)]

    def body(*refs):
        per_in, per_out = refs[:len(ins)], refs[len(ins) + len(consts):len(ins) + len(consts) + len(out_shapes)]
        const_refs, ext_ref = refs[len(ins):len(ins) + len(consts)], refs[-1]
        for e in range(elems):
            one = pl.ds(e, 1)
            _s_mix_body(*[r.at[one] for r in per_in], *const_refs, *[r.at[one] for r in per_out], ext_ref.at[e],
                        steps=steps)

    return pl.pallas_call(
        body,
        grid=(batch // elems,),
        in_specs=in_specs,
        out_specs=[per_batch(s.shape) for s in out_shapes],
        out_shape=out_shapes,
        scratch_shapes=[pltpu.VMEM((elems, 2 * SUBLANES, LRU_W), F32)],
        compiler_params=_params("parallel"),
        name="s_mix",
    )(*ins, *consts)


def _bucket(n):
    n = np.maximum(n, 0)
    max_exact = NUM_BUCKETS // 2
    nf = np.maximum(n, 1).astype(np.float32)
    large = max_exact + (np.log(nf / max_exact) / np.float32(math.log(MAX_DISTANCE / max_exact))
                         * (NUM_BUCKETS - max_exact)).astype(np.int32)
    return np.where(n < max_exact, n, np.minimum(large, NUM_BUCKETS - 1))


def _dist_bias(rel_bias, g, dist, valid):
    tab = rel_bias[:, g * HEADS:(g + 1) * HEADS].T[:, _bucket(dist)]
    return jnp.where(valid[None], tab, -jnp.inf).astype(F32)


def _toeplitz(u, rows, cols):
    lu = rows + cols - 1
    ue = jnp.concatenate([u, u[:, :1]], axis=1)
    m = jnp.tile(ue, (1, rows))[:, :rows * lu].reshape(-1, rows, lu)
    return m[:, :, rows - 1:rows - 1 + cols].reshape(-1, cols)


def _prompt_bias(rel_bias, tile, tiles_per_seq):
    ring = 2 * N_REL
    col = np.arange(ring)
    stacks = []
    for g, (_, dil) in enumerate(GROUPS):
        cur = tile // dil
        qrows = min(cur, N_REL)
        hist = ring - qrows
        delta = np.arange(-(qrows - 1), ring)
        v = _dist_bias(rel_bias, g, dil * delta, (delta >= 0) & (delta <= N_REL))
        base = _toeplitz(v[:, ::-1], qrows, ring)
        if cur >= N_REL:
            tabs = [base, jnp.where(col >= hist, base, -jnp.inf)]
        else:
            slots = ring // cur
            tabs = []
            for t in range(min(tiles_per_seq, 2 * slots)):
                cols = []
                for s in range(slots):
                    age = (t - s) % slots
                    blk = base[:, hist - cur * age:hist - cur * age + cur]
                    cols.append(blk if t - age >= 0 else jnp.full_like(blk, -jnp.inf))
                tabs.append(jnp.concatenate(cols, axis=1))
        stacks.append(jnp.stack(tabs))
    return stacks


def _sample_bias(rel_bias, steps):
    t = np.arange(steps)[:, None]
    cache, new = [], []
    for g, (window, dil) in enumerate(GROUPS):
        dist = np.arange(window + steps)
        tab = _dist_bias(rel_bias, g, dist, (dist % dil == 0) & (dist <= window))
        cache.append(_toeplitz(tab[:, 1:][:, ::-1], steps, window))
        j = np.arange(N_REL)[None, :]
        dist = t - j
        valid = (j < steps) & (dist >= 0) & (dist % dil == 0)
        new.append(_dist_bias(rel_bias, g, dist, valid).reshape(HEADS * steps, N_REL))
    return cache, jnp.stack(new)


def _block_diag(w):
    per = MXU_TILE // LRU_BLOCK
    w4 = w.reshape(-1, per, LRU_BLOCK, LRU_BLOCK)
    bd = jnp.einsum('jnef,nm->jnemf', w4, jnp.eye(per, dtype=w.dtype))
    return bd.reshape(-1, MXU_TILE, MXU_TILE).astype(BF16)


def _to_pos_minor(c):
    return jnp.transpose(c[0], (0, 2, 3, 1)).reshape(c.shape[1], ATT_W, c.shape[2])


def _from_pos_minor(c):
    return jnp.transpose(c.reshape(c.shape[0], HEADS, HEAD_DIM, c.shape[2]), (0, 3, 1, 2))[None]


def kernel(x_prompt, x_sample, mem_prompt, cache_win_k0, cache_win_v0, cache_win_k1, cache_win_v1, cache_win_k2, cache_win_v2, state_conv, state_lru, cache_mem_k, cache_mem_v, rel_bias, ffn1_norm, ffn1_w_gu, ffn1_w_down, mix_norm, w_in, b_gate, conv_w, conv_b, lru_w_a, lru_b_a, lru_w_i, lru_b_i, lru_lambda, mem_norm, w_mem_kv, w_att_o, w_rec_o, w_mem_o, w_out, ffn2_norm, ffn2_w_gu, ffn2_w_down, final_norm):
    depth = ffn1_norm.shape[0]
    assert depth == 1
    l = 0
    batch, seq, _ = x_prompt.shape
    sbatch, steps, _ = x_sample.shape
    row = lambda v: v.reshape(1, -1)

    lru = (conv_w[l], row(conv_b[l]), _block_diag(lru_w_a[l]), _block_diag(lru_w_i[l]),
           row(lru_b_a[l]), row(lru_b_i[l]), row(lru_lambda[l]))
    ffn1 = (row(ffn1_norm[l]), ffn1_w_gu[l].astype(BF16), ffn1_w_down[l].astype(BF16))
    ffn2 = (row(ffn2_norm[l]), ffn2_w_gu[l].astype(BF16), ffn2_w_down[l].astype(BF16))
    outw = (row(b_gate[l]), w_att_o[l].astype(BF16), w_rec_o[l].astype(BF16), w_mem_o[l].astype(BF16),
            w_out[l].astype(BF16))
    mixg = row(mix_norm[l])
    fin = row(final_norm)

    w_in_b = w_in[l].astype(BF16)
    xp1, y_rec, p_conv, p_lru = _ffn_rec(x_prompt.reshape(batch * seq, D_MODEL), *ffn1, mixg, w_in_b, lru, batch, seq)
    xp1 = xp1.reshape(batch, seq, D_MODEL)
    p_mk, p_mv, mkv = _mem_kv(mem_prompt.reshape(batch * N_MEM, D_MODEL), row(mem_norm[l]),
                              w_mem_kv[l].astype(BF16), row_tile=2 * N_MEM)
    proj = _p_proj(xp1, mixg, w_in_b)
    qkv, wins = proj[0:9], proj[9:15]
    xp2 = _p_mix(xp1, qkv, y_rec, mkv.reshape(batch, N_MEM, 2 * MEM_W), _prompt_bias(rel_bias, MIX_TILE, seq // MIX_TILE), mixg,
                 w_in_b, *outw)
    y_prompt = _ffn(xp2.reshape(batch * seq, D_MODEL), *ffn2, final_g=fin).reshape(batch, seq, D_MODEL)

    n_s = sbatch * steps
    xs1 = _ffn(x_sample.reshape(n_s, D_MODEL), *ffn1)
    z = _norm_mm(xs1, mixg, w_in_b, row_tile=n_s, col_tile=IN_COLS // 2)
    caches = [_to_pos_minor(c) for c in
              (cache_win_k0, cache_win_v0, cache_win_k1, cache_win_v1, cache_win_k2, cache_win_v2)]
    bias_c, bias_n = _sample_bias(rel_bias, steps)
    mem_rows = lambda c: c[l].reshape(sbatch, N_MEM * MEM_HEADS, MEM_HEAD_DIM)
    mix = _s_mix(z.reshape(sbatch, steps, IN_COLS), caches, state_conv[l], state_lru[l].reshape(sbatch, 1, LRU_W),
                 mem_rows(cache_mem_k), mem_rows(cache_mem_v), bias_c, bias_n, lru)
    s_att, s_rec, s_mem = (a.reshape(1, n_s, a.shape[-1]) for a in mix[0:3])
    xs2 = _merge(xs1.reshape(1, n_s, D_MODEL), s_att, s_rec, s_mem, mixg, w_in[l][:, COL_GATE:].astype(BF16),
                 *outw)
    y_sample = _ffn(xs2.reshape(n_s, D_MODEL), *ffn2, final_g=fin).reshape(sbatch, steps, D_MODEL)

    mem_state = lambda a: a.reshape(1, batch, N_MEM, MEM_HEADS, MEM_HEAD_DIM)
    return (y_prompt, y_sample, *[_from_pos_minor(w) for w in wins],
            p_conv[None], p_lru.reshape(1, batch, LRU_W), mem_state(p_mk), mem_state(p_mv),
            *[_from_pos_minor(w) for w in mix[3:9]], mix[9][None], mix[10].reshape(1, sbatch, LRU_W))
```

```python
import functools
import math

import numpy as np
import jax
import jax.numpy as jnp
from jax import lax
from jax.experimental import pallas as pl
from jax.experimental.pallas import tpu as pltpu

F32 = jnp.float32
BF16 = jnp.bfloat16

D_MODEL = 1024
D_FF = 2816
HEAD_DIM = 64
HEADS = 4
GROUPS = ((128, 1), (512, 4), (2048, 16))
N_REL = 128
ATT_W = HEADS * HEAD_DIM
ATT_SCALE = HEAD_DIM ** -0.5
NUM_BUCKETS = 32
MAX_DISTANCE = 2048
LRU_W = 768
LRU_BLOCK = 64
LRU_C = 8.0
CONV_W = 4
N_MEM = 256
MEM_HEADS = 4
MEM_HEAD_DIM = 128
MEM_W = MEM_HEADS * MEM_HEAD_DIM
MEM_SCALE = MEM_HEAD_DIM ** -0.5
RMS_EPS = 1e-6
QKV_W = 3 * len(GROUPS) * ATT_W
COL_XR = QKV_W
COL_YR = COL_XR + LRU_W
COL_QM = COL_YR + LRU_W
COL_GATE = COL_QM + MEM_W
IN_COLS = COL_GATE + 3 * D_MODEL

V7X_VMEM_LIMIT = 62 * 1024 * 1024
MXU_TILE = 256
SUBLANES = 8
LANES = 128
FFN_TILE = 512
FFN_PLAIN_TILE = 1024
FFN_CHUNK = 256
PROJ_TILE = 1024
SCAN_SEGS = SUBLANES
SCAN_STEPS = FFN_TILE // SCAN_SEGS
SCAN_PITCH = SCAN_STEPS + SUBLANES
MIX_TILE = 512
ATTN_DEPTH = 8
FILL_COLS = 256
S_MIX_ELEMS = 2
LRU_CHUNKS = LRU_W // LANES
ATT_CHUNKS = ATT_W // LANES


def _dot(a, b):
    return jnp.dot(a, b, preferred_element_type=F32)


def _dot_nt(a, b):
    return lax.dot_general(a, b, (((1,), (1,)), ((), ())), preferred_element_type=F32)


def _rms(x, g):
    return x * lax.rsqrt(jnp.mean(x * x, axis=-1, keepdims=True) + RMS_EPS) * g


def _stack_heads(q):
    lane = lax.broadcasted_iota(jnp.int32, q.shape, 1)
    return jnp.concatenate(
        [jnp.where((lane >= h * HEAD_DIM) & (lane < (h + 1) * HEAD_DIM), q, 0.0) for h in range(HEADS)], axis=0)


def _unstack_heads(x, rows):
    low = lax.broadcasted_iota(jnp.int32, (rows, LANES), 1) < HEAD_DIM
    per = LANES // HEAD_DIM
    return [jnp.where(low,
                      x[(per * c) * rows:(per * c + 1) * rows, c * LANES:(c + 1) * LANES],
                      x[(per * c + 1) * rows:(per * c + 2) * rows, c * LANES:(c + 1) * LANES])
            for c in range(ATT_CHUNKS)]


def _linear_scan(a, u, rows):
    row = lax.broadcasted_iota(jnp.int32, (rows, 1), 0)
    d = 1
    while d < rows:
        a_s = pltpu.roll(a, d, 0)
        u_s = pltpu.roll(u, d, 0)
        keep = row >= d
        u = jnp.where(keep, a * u_s + u, u)
        a = jnp.where(keep, a * a_s, a)
        d *= 2
    return a, u


def _zero_after(x):
    bits = pltpu.bitcast(x, jnp.uint32)
    half = jnp.uint32(17)
    return pltpu.bitcast(lax.shift_right_logical(lax.shift_right_logical(bits, half), half), F32)


def _lru_preact(xc, wa_ref, wi_ref):
    xcb = xc.astype(BF16)
    ra, ia = [], []
    for j in range(LRU_W // MXU_TILE):
        blk = xcb[:, j * MXU_TILE:(j + 1) * MXU_TILE]
        ra.append(_dot(blk, wa_ref[j]))
        ia.append(_dot(blk, wi_ref[j]))
    return jnp.concatenate(ra, axis=-1), jnp.concatenate(ia, axis=-1)


def _lru_decay_input(xc, ra, ia, ba_ref, bi_ref, lam_ref):
    r = jax.nn.sigmoid(ra + ba_ref[...])
    i = jax.nn.sigmoid(ia + bi_ref[...])
    nl = -lam_ref[...]
    softplus = jnp.maximum(nl, 0.0) + jnp.log1p(jnp.exp(-jnp.abs(nl)))
    log_a = (-LRU_C * r) * softplus
    a = jnp.exp(log_a)
    u = jnp.sqrt(1.0 - a * a) * (i * xc)
    return a, u


def _lru_gates(xc, wa_ref, wi_ref, ba_ref, bi_ref, lam_ref):
    return _lru_decay_input(xc, *_lru_preact(xc, wa_ref, wi_ref), ba_ref, bi_ref, lam_ref)


def _conv_taps(cw_ref, cb_ref, taps):
    xc = cb_ref[...] + cw_ref[CONV_W - 1:CONV_W, :] * taps[0]
    for j in range(1, CONV_W):
        xc = xc + cw_ref[CONV_W - 1 - j:CONV_W - j, :] * taps[j]
    return xc


def _mem_attention(qm, mk_heads, mv_heads):
    heads = []
    for h in range(MEM_HEADS):
        q = qm[:, h * MEM_HEAD_DIM:(h + 1) * MEM_HEAD_DIM].astype(BF16)
        s = _dot_nt(q, mk_heads[h]) * MEM_SCALE
        m = jnp.max(s, axis=-1, keepdims=True)
        p = jnp.exp(s - m)
        l = jnp.sum(p, axis=-1, keepdims=True)
        heads.append(_dot(p.astype(BF16), mv_heads[h]) * (1.0 / l))
    return jnp.concatenate(heads, axis=-1)


def _ffn_body(x_ref, g_ref, wgu_ref, wd_ref, *rest, final_norm):
    if final_norm:
        fg_ref, o_ref, act_ref = rest
    else:
        o_ref, act_ref = rest
    x = x_ref[...]
    hn = _rms(x, g_ref[...]).astype(BF16)
    for c in range(D_FF // FFN_CHUNK):
        lo = c * FFN_CHUNK
        gate = _dot(hn, wgu_ref[:, lo:lo + FFN_CHUNK])
        up = _dot(hn, wgu_ref[:, D_FF + lo:D_FF + lo + FFN_CHUNK])
        act_ref[:, lo:lo + FFN_CHUNK] = (gate * jax.nn.sigmoid(gate) * up).astype(BF16)
    y = x + 0.5 * _dot(act_ref[...], wd_ref[...])
    if final_norm:
        y = _rms(y, fg_ref[...])
    o_ref[...] = y


def _norm_mm_body(x_ref, g_ref, w_ref, o_ref):
    hn = _rms(x_ref[...], g_ref[...]).astype(BF16)
    o_ref[...] = _dot(hn, w_ref[...])


def _mem_kv_body(x_ref, g_ref, w_ref, mk_ref, mv_ref, mkv_ref):
    rows = x_ref.shape[0]
    z = _dot(_rms(x_ref[...], g_ref[...]).astype(BF16), w_ref[...])
    mkv_ref[...] = z.astype(BF16)
    for h in range(MEM_HEADS):
        dst = pl.ds(h, rows, stride=MEM_HEADS)
        mk_ref[dst, :] = z[:, h * MEM_HEAD_DIM:(h + 1) * MEM_HEAD_DIM]
        mv_ref[dst, :] = z[:, MEM_W + h * MEM_HEAD_DIM:MEM_W + (h + 1) * MEM_HEAD_DIM]


def _ffn_rec_body(x_ref, g_ref, wgu_ref, wd_ref, mg_ref, wxr_ref, wyr_ref, cw_ref, cb_ref, wa_ref, wi_ref, ba_ref, bi_ref,
                  lam_ref, o_ref, yrec_ref, pconv_ref, plru_ref, act_ref, xbuf_ref, xpad_ref, hpad_ref, hc_ref,
                  *, tiles_per_seq):
    rows = FFN_TILE
    i = pl.program_id(0)
    t = lax.rem(jnp.maximum(i - 1, 0), tiles_per_seq)
    slot = lax.rem(i, 2)
    tail = slice(SCAN_STEPS - SUBLANES, SCAN_STEPS)

    @pl.when(i == 0)
    def _():
        xbuf_ref[...] = jnp.zeros(xbuf_ref.shape, F32)

    @pl.when(t == 0)
    def _():
        xpad_ref[:, tail, :] = jnp.zeros((LRU_CHUNKS, SUBLANES, LANES), F32)
        hc_ref[...] = jnp.zeros((1, LRU_W), F32)

    hn_p = _rms(xbuf_ref[1 - slot], mg_ref[...]).astype(BF16)
    xr = _dot(hn_p, wxr_ref[...])
    for c in range(LRU_CHUNKS):
        for s in range(SCAN_SEGS):
            lo = SCAN_PITCH * (s + 1)
            xpad_ref[c, lo:lo + SCAN_STEPS, :] = xr[s * SCAN_STEPS:(s + 1) * SCAN_STEPS, c * LANES:(c + 1) * LANES]

    x = x_ref[...]
    hn = _rms(x, g_ref[...]).astype(BF16)

    def ffn_chunk(c):
        lo = c * FFN_CHUNK
        gate = _dot(hn, wgu_ref[:, lo:lo + FFN_CHUNK])
        up = _dot(hn, wgu_ref[:, D_FF + lo:D_FF + lo + FFN_CHUNK])
        act_ref[:, lo:lo + FFN_CHUNK] = (gate * jax.nn.sigmoid(gate) * up).astype(BF16)
        return gate

    n_chunks = D_FF // FFN_CHUNK
    lead = 2
    for c in range(lead):
        ffn_chunk(c)

    def step_rows(j):
        start = SCAN_PITCH + j if j >= 0 else SCAN_STEPS + j
        sel = pl.ds(start, SCAN_SEGS, stride=SCAN_PITCH)
        return jnp.concatenate([xpad_ref[c, sel, :] for c in range(LRU_CHUNKS)], axis=-1)

    hist = CONV_W - 1
    xp = jnp.concatenate([step_rows(j) for j in range(-hist, SCAN_STEPS)], axis=0)
    n = SCAN_STEPS * SCAN_SEGS
    taps = [xp[(hist - j) * SCAN_SEGS:(hist - j) * SCAN_SEGS + n, :] for j in range(CONV_W)]
    xc = _conv_taps(cw_ref, cb_ref, taps)
    ra, ia = _lru_preact(xc, wa_ref, wi_ref)

    yr = _dot(hn_p, wyr_ref[...])
    a_parts, u_parts = [], []
    per = n // SCAN_SEGS
    for k in range(SCAN_SEGS):
        gate = ffn_chunk(lead + k)
        tie = jnp.concatenate([_zero_after(gate[0:per, 0:LANES])] * LRU_CHUNKS, axis=-1)
        sl = slice(k * per, (k + 1) * per)
        a_k, u_k = _lru_decay_input(xc[sl] + tie, ra[sl], ia[sl], ba_ref, bi_ref, lam_ref)
        a_parts.append(a_k)
        u_parts.append(u_k)
    a = jnp.concatenate(a_parts, axis=0)
    u = jnp.concatenate(u_parts, axis=0)
    for c in range(lead + SCAN_SEGS, n_chunks):
        ffn_chunk(c)

    def rows_of(v, j):
        return v[j * SCAN_SEGS:(j + 1) * SCAN_SEGS, :]

    h = jnp.zeros((SCAN_SEGS, LRU_W), F32)
    prod = jnp.ones((SCAN_SEGS, LRU_W), F32)
    for j in range(SCAN_STEPS):
        a_j = rows_of(a, j)
        h = a_j * h + rows_of(u, j)
        prod = a_j * prod
    prod_c, h_c = _linear_scan(prod, h, SCAN_SEGS)
    h_end = prod_c * hc_ref[...] + h_c
    seg = lax.broadcasted_iota(jnp.int32, (SCAN_SEGS, 1), 0)
    h = jnp.where(seg == 0, hc_ref[...], pltpu.roll(h_end, 1, 0))
    for j in range(SCAN_STEPS):
        h = rows_of(a, j) * h + rows_of(u, j)
        dst = pl.ds(j, SCAN_SEGS, stride=SCAN_PITCH)
        for c in range(LRU_CHUNKS):
            hpad_ref[c, dst, :] = h[:, c * LANES:(c + 1) * LANES]
    hs = jnp.concatenate(
        [jnp.concatenate([hpad_ref[c, SCAN_PITCH * s:SCAN_PITCH * s + SCAN_STEPS, :] for c in range(LRU_CHUNKS)],
                         axis=-1) for s in range(SCAN_SEGS)], axis=0)
    yrec_ref[0] = (jax.nn.gelu(yr) * hs).astype(BF16)
    hc_ref[...] = h_end[SCAN_SEGS - 1:SCAN_SEGS, :]
    plru_ref[0] = h_end[SCAN_SEGS - 1:SCAN_SEGS, :]
    pconv_ref[0] = xr[rows - hist:rows, :]
    for c in range(LRU_CHUNKS):
        xpad_ref[c, tail, :] = xr[rows - SUBLANES:rows, c * LANES:(c + 1) * LANES]

    y = x + 0.5 * _dot(act_ref[...], wd_ref[...])
    o_ref[...] = y
    xbuf_ref[slot] = y


def _p_proj_body(x_ref, g_ref, w_ref, q0, q1, q2, k0, k1, k2, v0, v1, v2, pk0, pv0, pk1, pv1, pk2, pv2, zs_ref):
    rows = PROJ_TILE
    hn = _rms(x_ref[0], g_ref[...]).astype(BF16)
    outs = ((q0, q1, q2), (k0, k1, k2), (v0, v1, v2))
    wins = (None, (pk0, pk1, pk2), (pv0, pv1, pv2))
    for g, (window, dil) in reversed(list(enumerate(GROUPS))):
        for kind in (1, 2, 0):
            col = (kind * len(GROUPS) + g) * ATT_W
            z = _dot(hn, w_ref[:, col:col + ATT_W])
            if kind == 0:
                z = z * ATT_SCALE
            else:
                keep = min(window, rows)
                wins[kind][g][0] = jnp.transpose(z[rows - keep:, :])
            def emit(r, part):
                if kind == 0:
                    qrows = min(MIX_TILE // dil, N_REL)
                    part = jnp.concatenate([_stack_heads(part[lo:lo + qrows, :])
                                            for lo in range(0, rows // dil, qrows)], axis=0)
                outs[kind][g][0, r] = part.astype(BF16)

            if dil == 1:
                emit(0, z)
            else:
                stage = zs_ref.at[kind * (len(GROUPS) - 1) + g - 1]
                for c in range(ATT_CHUNKS):
                    stage[c] = z[:, c * LANES:(c + 1) * LANES]
                for r in range(dil):
                    sel = pl.ds(r, rows // dil, stride=dil)
                    emit(r, jnp.concatenate([stage[c, sel, :] for c in range(ATT_CHUNKS)], axis=-1))


def _softmax_parts(s):
    m = jnp.max(s, axis=-1, keepdims=True)
    p = jnp.exp(s - m)
    return p.astype(BF16), jnp.sum(p, axis=-1, keepdims=True), m


def _gate(hn, wc_ref, bg_ref, gate_col, idx):
    lo = idx * D_MODEL
    return jax.nn.sigmoid(_dot(hn, wc_ref[:, gate_col + lo:gate_col + lo + D_MODEL]) + bg_ref[:, lo:lo + D_MODEL])


def _p_mix_body(x_ref, q0, q1, q2, k0, k1, k2, v0, v1, v2, yr_ref, mkv_ref, b0, b1, b2, g_ref, *rest,
                tiles_per_seq):
    n_win = (IN_COLS - COL_QM) // FILL_COLS
    wc_refs = rest[:n_win]
    bg_ref, wao_ref, wro_ref, wmo_ref, wout_ref, o_ref, kh0, kh1, kh2, vh0, vh1, vh2, o_scr, l_scr = rest[n_win:]
    qm_wins = MEM_W // FILL_COLS
    tile = MIX_TILE
    t = pl.program_id(1)
    slot = lax.rem(t, 2)
    qs_, ks_, vs_ = (q0, q1, q2), (k0, k1, k2), (v0, v1, v2)
    khs, vhs, biases = (kh0, kh1, kh2), (vh0, vh1, vh2), (b0, b1, b2)
    ring = 2 * N_REL

    def in_ring(g):
        return tile // GROUPS[g][1] < N_REL

    @pl.when(t == 0)
    def _():
        for g, (window, dil) in enumerate(GROUPS):
            for h_ref in (khs[g], vhs[g]):
                if in_ring(g):
                    h_ref[...] = jnp.zeros(h_ref.shape, BF16)
                else:
                    cur = tile // dil
                    for r in range(dil):
                        h_ref[0, r * (N_REL + cur):r * (N_REL + cur) + N_REL, :] = jnp.zeros((N_REL, ATT_W), BF16)

    res = {}

    windows, carried = [], []

    def window_block(g, dil, r, lo, qrows, keys, tab, new_rows):
        def rows_of(h_ref, sel):
            return h_ref.at[sel] if in_ring(g) else h_ref.at[slot, sel]

        def scores():
            if new_rows is not None:
                rows_of(khs[g], new_rows)[...] = ks_[g][0, r]
                rows_of(vhs[g], new_rows)[...] = vs_[g][0, r]
            q = qs_[g][0, r, HEADS * lo:HEADS * (lo + qrows), :]
            return _softmax_parts(_dot_nt(q, rows_of(khs[g], keys)[...]) + tab)

        def values(parts):
            p, l, m = parts
            o = _dot(p, rows_of(vhs[g], keys)[...]) * (1.0 / l)
            lse = jnp.broadcast_to(m + jnp.log(l), o.shape)
            o_c, l_c = _unstack_heads(o, qrows), _unstack_heads(lse, qrows)
            dst = slice(lo, lo + qrows) if dil == 1 else pl.ds(lo * dil + r, qrows, stride=dil)
            for c in range(ATT_CHUNKS):
                o_scr[g * ATT_CHUNKS + c, dst, :] = o_c[c]
                l_scr[g * ATT_CHUNKS + c, dst, :] = l_c[c]
        return scores, values

    for g, (window, dil) in enumerate(GROUPS):
        cur = tile // dil
        if in_ring(g):
            slots = ring // cur
            tabs = biases[g].shape[0]
            tab = biases[g][t if tabs >= tiles_per_seq else jnp.where(t < slots, t, slots + lax.rem(t, slots))]
            newest = lax.rem(t, slots) * cur
            for r in range(dil):
                base = r * ring
                windows.append(window_block(g, dil, r, 0, cur, slice(base, base + ring), tab,
                                            pl.ds(pl.multiple_of(base + newest, cur), cur)))
        else:
            for r in range(dil):
                base = r * (N_REL + cur)
                carried.append((g, base, cur))
                for j in range(cur // N_REL):
                    tab = biases[g][jnp.where(t == 0, 1, 0)] if j == 0 else biases[g][0]
                    lo = j * N_REL
                    windows.append(window_block(g, dil, r, lo, N_REL, slice(base + lo, base + lo + ring), tab,
                                                slice(base + N_REL, base + N_REL + cur) if j == 0 else None))

    def memory_block(h):
        sl = slice(h * MEM_HEAD_DIM, (h + 1) * MEM_HEAD_DIM)

        def scores():
            return _softmax_parts(_dot_nt(res["qm"][:, sl].astype(BF16), mkv_ref[0, :, sl]) * MEM_SCALE)

        def values(parts):
            p, l, _ = parts
            res["mem", h] = _dot(p, mkv_ref[0, :, MEM_W + h * MEM_HEAD_DIM:MEM_W + (h + 1) * MEM_HEAD_DIM]) * (1.0 / l)
        return scores, values

    def normed_input():
        res["x"] = x_ref[0]
        res["hn"] = _rms(res["x"], g_ref[...]).astype(BF16)
        res["qm"] = jnp.concatenate([_dot(res["hn"], w[...]) for w in wc_refs[:qm_wins]], axis=-1)

    def gate_piece(idx, piece):
        def emit():
            lo = idx * D_MODEL + piece * FILL_COLS
            cols = slice(piece * FILL_COLS, (piece + 1) * FILL_COLS)
            gate = jax.nn.sigmoid(_dot(res["hn"], wc_refs[qm_wins + lo // FILL_COLS][...])
                                  + bg_ref[:, lo:lo + FILL_COLS])
            res["gate", idx, piece] = gate * _dot(yr_ref[0], wro_ref[:, cols]) if idx == 1 else gate
        return emit

    normed_input()
    blocks = windows + [memory_block(h) for h in range(MEM_HEADS)]
    pieces = D_MODEL // FILL_COLS
    fillers = [gate_piece(idx, piece) for idx in (1, 0, 2) for piece in range(pieces)]
    every = max(1, len(blocks) // len(fillers))

    pending = []
    for i, (scores, values) in enumerate(blocks):
        pending.append((values, scores()))
        if len(pending) > ATTN_DEPTH:
            done, parts = pending.pop(0)
            done(parts)
        if fillers and (i + 1) % every == 0:
            fillers.pop(0)()
    for done, parts in pending:
        done(parts)
    for emit in fillers:
        emit()
    gates = [jnp.concatenate([res["gate", idx, piece] for piece in range(pieces)], axis=-1) for idx in range(3)]

    mixed = []
    for c in range(ATT_CHUNKS):
        l0, l1, l2 = (l_scr[g * ATT_CHUNKS + c] for g in range(len(GROUPS)))
        o0, o1, o2 = (o_scr[g * ATT_CHUNKS + c] for g in range(len(GROUPS)))
        m = jnp.maximum(jnp.maximum(l0, l1), l2)
        w0, w1, w2 = jnp.exp(l0 - m), jnp.exp(l1 - m), jnp.exp(l2 - m)
        mixed.append((w0 * o0 + w1 * o1 + w2 * o2) / (w0 + w1 + w2))
    ya = jnp.concatenate(mixed, axis=-1).astype(BF16)
    ym = jnp.concatenate([res["mem", h] for h in range(MEM_HEADS)], axis=-1).astype(BF16)
    merged = gates[0] * _dot(ya, wao_ref[...]) + gates[1] + gates[2] * _dot(ym, wmo_ref[...])
    o_ref[0] = res["x"] + _dot(merged.astype(BF16), wout_ref[...])
    for g, base, cur in carried:
        for h_ref in (khs[g], vhs[g]):
            h_ref[1 - slot, base:base + N_REL, :] = h_ref[slot, base + cur:base + cur + N_REL, :]


def _merge_out(x, hn, branches, wc_ref, gate_col, bg_ref, wout_ref):
    merged = None
    for idx, (y, w_ref) in enumerate(branches):
        term = _gate(hn, wc_ref, bg_ref, gate_col, idx) * _dot(y, w_ref[...])
        merged = term if merged is None else merged + term
    return x + _dot(merged.astype(BF16), wout_ref[...])


def _merge_body(x_ref, ya_ref, yr_ref, ym_ref, g_ref, wc_ref, bg_ref, wao_ref, wro_ref, wmo_ref, wout_ref, o_ref):
    x = x_ref[0]
    hn = _rms(x, g_ref[...]).astype(BF16)
    o_ref[0] = _merge_out(x, hn, ((ya_ref[0], wao_ref), (yr_ref[0], wro_ref), (ym_ref[0], wmo_ref)),
                          wc_ref, 0, bg_ref, wout_ref)


def _s_mix_body(z_ref, ck0, cv0, ck1, cv1, ck2, cv2, sconv_ref, slru_ref, cmk_ref, cmv_ref,
                bc0, bc1, bc2, bn_ref, cw_ref, cb_ref, wa_ref, wi_ref, ba_ref, bi_ref, lam_ref,
                ya_ref, yrec_ref, ym_ref, ok0, ov0, ok1, ov1, ok2, ov2, oconv_ref, olru_ref,
                ext_ref, *, steps):
    cks, cvs = (ck0, ck1, ck2), (cv0, cv1, cv2)
    oks, ovs = (ok0, ok1, ok2), (ov0, ov1, ov2)
    bcs = (bc0, bc1, bc2)
    zeros = jnp.zeros((N_REL - steps, ATT_W), F32)
    new_lanes = lax.broadcasted_iota(jnp.int32, (ATT_W, LANES), 1) >= LANES - steps
    outs, lses = [], []
    for g, (window, dil) in enumerate(GROUPS):
        q = z_ref[0, :, g * ATT_W:(g + 1) * ATT_W] * ATT_SCALE
        kn = z_ref[0, :, (3 + g) * ATT_W:(4 + g) * ATT_W]
        vn = z_ref[0, :, (6 + g) * ATT_W:(7 + g) * ATT_W]
        qs = _stack_heads(q).astype(BF16)
        kt = cks[g][0]
        vt = cvs[g][0]
        s_c = _dot(qs, kt.astype(BF16)) + bcs[g][...]
        s_n = _dot_nt(qs, jnp.concatenate([kn, zeros], axis=0).astype(BF16)) + bn_ref[g]
        m = jnp.maximum(jnp.max(s_c, axis=-1, keepdims=True), jnp.max(s_n, axis=-1, keepdims=True))
        p_c = jnp.exp(s_c - m)
        p_n = jnp.exp(s_n - m)
        l = jnp.sum(p_c, axis=-1, keepdims=True) + jnp.sum(p_n, axis=-1, keepdims=True)
        o = (_dot_nt(p_c.astype(BF16), vt.astype(BF16))
             + _dot(p_n.astype(BF16), jnp.concatenate([vn, zeros], axis=0).astype(BF16))) * (1.0 / l)
        lse = jnp.broadcast_to(m + jnp.log(l), o.shape)
        outs.append(jnp.concatenate(_unstack_heads(o, steps), axis=-1))
        lses.append(jnp.concatenate(_unstack_heads(lse, steps), axis=-1))
        for old, new, dst in ((kt, kn, oks[g]), (vt, vn, ovs[g])):
            rolled = pltpu.roll(old, window - steps, 1)
            new_t = jnp.transpose(jnp.concatenate([zeros, new], axis=0))
            if window > LANES:
                dst[0, :, 0:window - LANES] = rolled[:, 0:window - LANES]
            dst[0, :, window - LANES:window] = jnp.where(new_lanes, new_t, rolled[:, window - LANES:window])
    m = jnp.maximum(jnp.maximum(lses[0], lses[1]), lses[2])
    ws = [jnp.exp(l - m) for l in lses]
    ya_ref[0] = ((ws[0] * outs[0] + ws[1] * outs[1] + ws[2] * outs[2]) / (ws[0] + ws[1] + ws[2])).astype(BF16)

    hist = CONV_W - 1
    ext_ref[0:SUBLANES, :] = jnp.zeros((SUBLANES, LRU_W), F32)
    ext_ref[SUBLANES - hist:SUBLANES, :] = sconv_ref[0]
    ext_ref[SUBLANES:SUBLANES + steps, :] = z_ref[0, :, COL_XR:COL_XR + LRU_W]
    taps = [ext_ref[SUBLANES - j:SUBLANES - j + steps, :] for j in range(CONV_W)]
    a, u = _lru_gates(_conv_taps(cw_ref, cb_ref, taps), wa_ref, wi_ref, ba_ref, bi_ref, lam_ref)
    a_c, u_c = _linear_scan(a, u, steps)
    h = a_c * slru_ref[0] + u_c
    yrec_ref[0] = (jax.nn.gelu(z_ref[0, :, COL_YR:COL_YR + LRU_W]) * h).astype(BF16)
    olru_ref[0] = h[steps - 1:steps, :]
    oconv_ref[0] = ext_ref[SUBLANES + steps - hist:SUBLANES + steps, :]

    qm = z_ref[0, :, COL_QM:COL_QM + MEM_W]
    mk = [cmk_ref[0, pl.ds(h, N_MEM, stride=MEM_HEADS), :].astype(BF16) for h in range(MEM_HEADS)]
    mv = [cmv_ref[0, pl.ds(h, N_MEM, stride=MEM_HEADS), :].astype(BF16) for h in range(MEM_HEADS)]
    ym_ref[0] = _mem_attention(qm, mk, mv).astype(BF16)


def _resident(shape):
    zeros = (0,) * len(shape)
    return pl.BlockSpec(shape, lambda *_: zeros, pipeline_mode=pl.Buffered(1))


def _col_window(rows, first_col, cols):
    assert first_col % cols == 0
    block = first_col // cols
    return pl.BlockSpec((rows, cols), lambda *_: (0, block), pipeline_mode=pl.Buffered(1))


def _params(*semantics):
    return pltpu.CompilerParams(dimension_semantics=semantics, vmem_limit_bytes=V7X_VMEM_LIMIT)


def _ffn(x, norm_g, w_gu, w_down, final_g=None):
    n = x.shape[0]
    tile = min(FFN_PLAIN_TILE, n)
    row = pl.BlockSpec((tile, D_MODEL), lambda i: (i, 0))
    in_specs = [row, _resident((1, D_MODEL)), _resident(w_gu.shape), _resident(w_down.shape)]
    args = [x, norm_g, w_gu, w_down]
    if final_g is not None:
        in_specs.append(_resident((1, D_MODEL)))
        args.append(final_g)
    return pl.pallas_call(
        functools.partial(_ffn_body, final_norm=final_g is not None),
        grid=(n // tile,),
        in_specs=in_specs,
        out_specs=row,
        out_shape=jax.ShapeDtypeStruct((n, D_MODEL), F32),
        scratch_shapes=[pltpu.VMEM((tile, D_FF), BF16)],
        compiler_params=_params("parallel"),
        name="ffn_final" if final_g is not None else "ffn",
    )(*args)


def _norm_mm(x, norm_g, w, row_tile, col_tile):
    n, cols = x.shape[0], w.shape[1]
    return pl.pallas_call(
        _norm_mm_body,
        grid=(n // row_tile, cols // col_tile),
        in_specs=[pl.BlockSpec((row_tile, D_MODEL), lambda i, j: (i, 0)),
                  _resident((1, D_MODEL)),
                  pl.BlockSpec((D_MODEL, col_tile), lambda i, j: (0, j))],
        out_specs=pl.BlockSpec((row_tile, col_tile), lambda i, j: (i, j)),
        out_shape=jax.ShapeDtypeStruct((n, cols), F32),
        compiler_params=_params("parallel", "parallel"),
        name="norm_mm",
    )(x, norm_g, w)


def _mem_kv(mem, norm_g, w_kv, row_tile):
    n = mem.shape[0]
    state = jax.ShapeDtypeStruct((n * MEM_HEADS, MEM_HEAD_DIM), F32)
    state_spec = pl.BlockSpec((row_tile * MEM_HEADS, MEM_HEAD_DIM), lambda i: (i, 0))
    return pl.pallas_call(
        _mem_kv_body,
        grid=(n // row_tile,),
        in_specs=[pl.BlockSpec((row_tile, D_MODEL), lambda i: (i, 0)),
                  _resident((1, D_MODEL)), _resident(w_kv.shape)],
        out_specs=[state_spec, state_spec, pl.BlockSpec((row_tile, 2 * MEM_W), lambda i: (i, 0))],
        out_shape=[state, state, jax.ShapeDtypeStruct((n, 2 * MEM_W), BF16)],
        compiler_params=_params("parallel"),
        name="mem_kv",
    )(mem, norm_g, w_kv)


def _ffn_rec(x, norm_g, w_gu, w_down, mix_g, w_in, lru, batch, seq):
    tile = FFN_TILE
    nt = seq // tile
    n = batch * nt
    last = n - 1

    def prev(i):
        return jnp.maximum(i - 1, 0)

    ffn_w = [norm_g, w_gu, w_down, mix_g]
    weights = ffn_w + [w_in, w_in] + list(lru)
    return pl.pallas_call(
        functools.partial(_ffn_rec_body, tiles_per_seq=nt),
        grid=(n + 1,),
        in_specs=[pl.BlockSpec((tile, D_MODEL), lambda i: (jnp.minimum(i, last), 0))]
        + [_resident(w.shape) for w in ffn_w]
        + [_col_window(D_MODEL, COL_XR, LRU_W), _col_window(D_MODEL, COL_YR, LRU_W)]
        + [_resident(w.shape) for w in lru],
        out_specs=[pl.BlockSpec((tile, D_MODEL), lambda i: (jnp.minimum(i, last), 0)),
                   pl.BlockSpec((1, tile, LRU_W), lambda i: (prev(i) // nt, prev(i) % nt, 0)),
                   pl.BlockSpec((1, CONV_W - 1, LRU_W), lambda i: (prev(i) // nt, 0, 0)),
                   pl.BlockSpec((1, 1, LRU_W), lambda i: (prev(i) // nt, 0, 0))],
        out_shape=[jax.ShapeDtypeStruct((n * tile, D_MODEL), F32),
                   jax.ShapeDtypeStruct((batch, seq, LRU_W), BF16),
                   jax.ShapeDtypeStruct((batch, CONV_W - 1, LRU_W), F32),
                   jax.ShapeDtypeStruct((batch, 1, LRU_W), F32)],
        scratch_shapes=[pltpu.VMEM((tile, D_FF), BF16),
                        pltpu.VMEM((2, tile, D_MODEL), F32),
                        pltpu.VMEM((LRU_CHUNKS, SCAN_PITCH * (SCAN_SEGS + 1), LANES), F32),
                        pltpu.VMEM((LRU_CHUNKS, SCAN_PITCH * SCAN_SEGS, LANES), F32),
                        pltpu.VMEM((1, LRU_W), F32)],
        compiler_params=_params("arbitrary"),
        name="ffn_rec",
    )(x, *weights)


def _p_proj(x1, norm_g, w_qkv):
    batch, seq, _ = x1.shape
    tile = PROJ_TILE
    nt = seq // tile
    qkv_shapes, qkv_specs = [], []
    for _kind in range(3):
        for window, dil in GROUPS:
            copies = HEADS if _kind == 0 else 1
            qkv_shapes.append(jax.ShapeDtypeStruct((batch, dil, copies * seq // dil, ATT_W), BF16))
            qkv_specs.append(pl.BlockSpec((1, dil, copies * tile // dil, ATT_W), lambda b, t: (b, 0, t, 0)))
    win_shapes, win_specs = [], []
    for window, dil in GROUPS:
        keep = min(window, seq)
        blk = min(keep, tile)
        first = nt - max(keep // tile, 1)
        for _kv in range(2):
            win_shapes.append(jax.ShapeDtypeStruct((batch, ATT_W, keep), F32))
            win_specs.append(pl.BlockSpec((1, ATT_W, blk),
                                          lambda b, t, first=first: (b, 0, jnp.maximum(t - first, 0))))
    return pl.pallas_call(
        _p_proj_body,
        grid=(batch, nt),
        in_specs=[pl.BlockSpec((1, tile, D_MODEL), lambda b, t: (b, t, 0)),
                  _resident((1, D_MODEL)), _col_window(D_MODEL, 0, QKV_W)],
        out_specs=qkv_specs + win_specs,
        out_shape=qkv_shapes + win_shapes,
        scratch_shapes=[pltpu.VMEM((3 * (len(GROUPS) - 1), ATT_CHUNKS, tile, LANES), F32)],
        compiler_params=_params("parallel", "arbitrary"),
        name="p_proj",
    )(x1, norm_g, w_qkv)


def _p_mix(x1, qkv, y_rec, mkv, biases, norm_g, w_c, b_gate, w_att_o, w_rec_o, w_mem_o, w_out):
    batch, seq, _ = x1.shape
    tile = MIX_TILE

    def tok(width):
        return pl.BlockSpec((1, tile, width), lambda b, t: (b, t, 0))

    q_specs, kv_specs, hist = [], [], []
    for window, dil in GROUPS:
        cur = tile // dil
        q_specs.append(pl.BlockSpec((1, dil, HEADS * cur, ATT_W), lambda b, t: (b, 0, t, 0)))
        kv_specs.append(pl.BlockSpec((1, dil, cur, ATT_W), lambda b, t: (b, 0, t, 0)))
        hist.append(pltpu.VMEM((dil * 2 * N_REL, ATT_W) if cur < N_REL else (2, dil * (N_REL + cur), ATT_W), BF16))
    windows = [_col_window(D_MODEL, col, FILL_COLS) for col in range(COL_QM, IN_COLS, FILL_COLS)]
    tail_w = [b_gate, w_att_o, w_rec_o, w_mem_o, w_out]
    weights = [norm_g] + [w_c] * len(windows) + tail_w
    in_specs = ([tok(D_MODEL)] + q_specs + kv_specs * 2
                + [tok(LRU_W), pl.BlockSpec((1, N_MEM, 2 * MEM_W), lambda b, t: (b, 0, 0))]
                + [_resident(a.shape) for a in list(biases) + [norm_g]] + windows
                + [_resident(a.shape) for a in tail_w])
    return pl.pallas_call(
        functools.partial(_p_mix_body, tiles_per_seq=seq // tile),
        grid=(batch, seq // tile),
        in_specs=in_specs,
        out_specs=tok(D_MODEL),
        out_shape=jax.ShapeDtypeStruct((batch, seq, D_MODEL), F32),
        scratch_shapes=hist * 2 + [pltpu.VMEM((len(GROUPS) * ATT_CHUNKS, tile, LANES), F32)] * 2,
        compiler_params=_params("parallel", "arbitrary"),
        name="p_mix",
    )(x1, *qkv, y_rec, mkv, *biases, *weights)


def _merge(x1, y_att, y_rec, y_mem, norm_g, w_c, b_gate, w_att_o, w_rec_o, w_mem_o, w_out):
    batch, seq, _ = x1.shape
    tile = seq

    def tok(width):
        return pl.BlockSpec((1, tile, width), lambda b, t: (b, t, 0))

    weights = [norm_g, w_c, b_gate, w_att_o, w_rec_o, w_mem_o, w_out]
    return pl.pallas_call(
        _merge_body,
        grid=(batch, seq // tile),
        in_specs=[tok(D_MODEL), tok(ATT_W), tok(LRU_W), tok(MEM_W)] + [_resident(w.shape) for w in weights],
        out_specs=tok(D_MODEL),
        out_shape=jax.ShapeDtypeStruct((batch, seq, D_MODEL), F32),
        compiler_params=_params("parallel", "parallel"),
        name="merge_s",
    )(x1, y_att, y_rec, y_mem, *weights)


def _s_mix(z, caches, state_conv, state_lru, cmk, cmv, bias_c, bias_n, lru):
    batch, steps, _ = z.shape

    elems = S_MIX_ELEMS if batch % S_MIX_ELEMS == 0 else 1

    def per_batch(shape):
        return pl.BlockSpec((elems,) + tuple(shape[1:]), lambda b: (b,) + (0,) * (len(shape) - 1))

    ins = [z] + list(caches) + [state_conv, state_lru, cmk, cmv]
    in_specs = [per_batch(a.shape) for a in ins]
    consts = list(bias_c) + [bias_n] + list(lru)
    in_specs += [_resident(a.shape) for a in consts]
    out_shapes = [jax.ShapeDtypeStruct((batch, steps, ATT_W), BF16),
                  jax.ShapeDtypeStruct((batch, steps, LRU_W), BF16),
                  jax.ShapeDtypeStruct((batch, steps, MEM_W), BF16)]
    out_shapes += [jax.ShapeDtypeStruct(c.shape, F32) for c in caches]
    out_shapes += [jax.ShapeDtypeStruct(state_conv.shape, F32), jax.ShapeDtypeStruct(state_lru.shape, F32)]

    def body(*refs):
        per_in, per_out = refs[:len(ins)], refs[len(ins) + len(consts):len(ins) + len(consts) + len(out_shapes)]
        const_refs, ext_ref = refs[len(ins):len(ins) + len(consts)], refs[-1]
        for e in range(elems):
            one = pl.ds(e, 1)
            _s_mix_body(*[r.at[one] for r in per_in], *const_refs, *[r.at[one] for r in per_out], ext_ref.at[e],
                        steps=steps)

    return pl.pallas_call(
        body,
        grid=(batch // elems,),
        in_specs=in_specs,
        out_specs=[per_batch(s.shape) for s in out_shapes],
        out_shape=out_shapes,
        scratch_shapes=[pltpu.VMEM((elems, 2 * SUBLANES, LRU_W), F32)],
        compiler_params=_params("parallel"),
        name="s_mix",
    )(*ins, *consts)


def _bucket(n):
    n = np.maximum(n, 0)
    max_exact = NUM_BUCKETS // 2
    nf = np.maximum(n, 1).astype(np.float32)
    large = max_exact + (np.log(nf / max_exact) / np.float32(math.log(MAX_DISTANCE / max_exact))
                         * (NUM_BUCKETS - max_exact)).astype(np.int32)
    return np.where(n < max_exact, n, np.minimum(large, NUM_BUCKETS - 1))


def _dist_bias(rel_bias, g, dist, valid):
    tab = rel_bias[:, g * HEADS:(g + 1) * HEADS].T[:, _bucket(dist)]
    return jnp.where(valid[None], tab, -jnp.inf).astype(F32)


def _toeplitz(u, rows, cols):
    lu = rows + cols - 1
    ue = jnp.concatenate([u, u[:, :1]], axis=1)
    m = jnp.tile(ue, (1, rows))[:, :rows * lu].reshape(-1, rows, lu)
    return m[:, :, rows - 1:rows - 1 + cols].reshape(-1, cols)


def _prompt_bias(rel_bias, tile, tiles_per_seq):
    ring = 2 * N_REL
    col = np.arange(ring)
    stacks = []
    for g, (_, dil) in enumerate(GROUPS):
        cur = tile // dil
        qrows = min(cur, N_REL)
        hist = ring - qrows
        delta = np.arange(-(qrows - 1), ring)
        v = _dist_bias(rel_bias, g, dil * delta, (delta >= 0) & (delta <= N_REL))
        base = _toeplitz(v[:, ::-1], qrows, ring)
        if cur >= N_REL:
            tabs = [base, jnp.where(col >= hist, base, -jnp.inf)]
        else:
            slots = ring // cur
            tabs = []
            for t in range(min(tiles_per_seq, 2 * slots)):
                cols = []
                for s in range(slots):
                    age = (t - s) % slots
                    blk = base[:, hist - cur * age:hist - cur * age + cur]
                    cols.append(blk if t - age >= 0 else jnp.full_like(blk, -jnp.inf))
                tabs.append(jnp.concatenate(cols, axis=1))
        stacks.append(jnp.stack(tabs))
    return stacks


def _sample_bias(rel_bias, steps):
    t = np.arange(steps)[:, None]
    cache, new = [], []
    for g, (window, dil) in enumerate(GROUPS):
        dist = np.arange(window + steps)
        tab = _dist_bias(rel_bias, g, dist, (dist % dil == 0) & (dist <= window))
        cache.append(_toeplitz(tab[:, 1:][:, ::-1], steps, window))
        j = np.arange(N_REL)[None, :]
        dist = t - j
        valid = (j < steps) & (dist >= 0) & (dist % dil == 0)
        new.append(_dist_bias(rel_bias, g, dist, valid).reshape(HEADS * steps, N_REL))
    return cache, jnp.stack(new)


def _block_diag(w):
    per = MXU_TILE // LRU_BLOCK
    w4 = w.reshape(-1, per, LRU_BLOCK, LRU_BLOCK)
    bd = jnp.einsum('jnef,nm->jnemf', w4, jnp.eye(per, dtype=w.dtype))
    return bd.reshape(-1, MXU_TILE, MXU_TILE).astype(BF16)


def _to_pos_minor(c):
    return jnp.transpose(c[0], (0, 2, 3, 1)).reshape(c.shape[1], ATT_W, c.shape[2])


def _from_pos_minor(c):
    return jnp.transpose(c.reshape(c.shape[0], HEADS, HEAD_DIM, c.shape[2]), (0, 3, 1, 2))[None]


def kernel(x_prompt, x_sample, mem_prompt, cache_win_k0, cache_win_v0, cache_win_k1, cache_win_v1, cache_win_k2, cache_win_v2, state_conv, state_lru, cache_mem_k, cache_mem_v, rel_bias, ffn1_norm, ffn1_w_gu, ffn1_w_down, mix_norm, w_in, b_gate, conv_w, conv_b, lru_w_a, lru_b_a, lru_w_i, lru_b_i, lru_lambda, mem_norm, w_mem_kv, w_att_o, w_rec_o, w_mem_o, w_out, ffn2_norm, ffn2_w_gu, ffn2_w_down, final_norm):
    depth = ffn1_norm.shape[0]
    assert depth == 1
    l = 0
    batch, seq, _ = x_prompt.shape
    sbatch, steps, _ = x_sample.shape
    row = lambda v: v.reshape(1, -1)

    lru = (conv_w[l], row(conv_b[l]), _block_diag(lru_w_a[l]), _block_diag(lru_w_i[l]),
           row(lru_b_a[l]), row(lru_b_i[l]), row(lru_lambda[l]))
    ffn1 = (row(ffn1_norm[l]), ffn1_w_gu[l].astype(BF16), ffn1_w_down[l].astype(BF16))
    ffn2 = (row(ffn2_norm[l]), ffn2_w_gu[l].astype(BF16), ffn2_w_down[l].astype(BF16))
    outw = (row(b_gate[l]), w_att_o[l].astype(BF16), w_rec_o[l].astype(BF16), w_mem_o[l].astype(BF16),
            w_out[l].astype(BF16))
    mixg = row(mix_norm[l])
    fin = row(final_norm)

    w_in_b = w_in[l].astype(BF16)
    xp1, y_rec, p_conv, p_lru = _ffn_rec(x_prompt.reshape(batch * seq, D_MODEL), *ffn1, mixg, w_in_b, lru, batch, seq)
    xp1 = xp1.reshape(batch, seq, D_MODEL)
    p_mk, p_mv, mkv = _mem_kv(mem_prompt.reshape(batch * N_MEM, D_MODEL), row(mem_norm[l]),
                              w_mem_kv[l].astype(BF16), row_tile=2 * N_MEM)
    proj = _p_proj(xp1, mixg, w_in_b)
    qkv, wins = proj[0:9], proj[9:15]
    xp2 = _p_mix(xp1, qkv, y_rec, mkv.reshape(batch, N_MEM, 2 * MEM_W), _prompt_bias(rel_bias, MIX_TILE, seq // MIX_TILE), mixg,
                 w_in_b, *outw)
    y_prompt = _ffn(xp2.reshape(batch * seq, D_MODEL), *ffn2, final_g=fin).reshape(batch, seq, D_MODEL)

    n_s = sbatch * steps
    xs1 = _ffn(x_sample.reshape(n_s, D_MODEL), *ffn1)
    z = _norm_mm(xs1, mixg, w_in_b, row_tile=n_s, col_tile=IN_COLS // 2)
    caches = [_to_pos_minor(c) for c in
              (cache_win_k0, cache_win_v0, cache_win_k1, cache_win_v1, cache_win_k2, cache_win_v2)]
    bias_c, bias_n = _sample_bias(rel_bias, steps)
    mem_rows = lambda c: c[l].reshape(sbatch, N_MEM * MEM_HEADS, MEM_HEAD_DIM)
    mix = _s_mix(z.reshape(sbatch, steps, IN_COLS), caches, state_conv[l], state_lru[l].reshape(sbatch, 1, LRU_W),
                 mem_rows(cache_mem_k), mem_rows(cache_mem_v), bias_c, bias_n, lru)
    s_att, s_rec, s_mem = (a.reshape(1, n_s, a.shape[-1]) for a in mix[0:3])
    xs2 = _merge(xs1.reshape(1, n_s, D_MODEL), s_att, s_rec, s_mem, mixg, w_in[l][:, COL_GATE:].astype(BF16),
                 *outw)
    y_sample = _ffn(xs2.reshape(n_s, D_MODEL), *ffn2, final_g=fin).reshape(sbatch, steps, D_MODEL)

    mem_state = lambda a: a.reshape(1, batch, N_MEM, MEM_HEADS, MEM_HEAD_DIM)
    return (y_prompt, y_sample, *[_from_pos_minor(w) for w in wins],
            p_conv[None], p_lru.reshape(1, batch, LRU_W), mem_state(p_mk), mem_state(p_mv),
            *[_from_pos_minor(w) for w in mix[3:9]], mix[9][None], mix[10].reshape(1, sbatch, LRU_W))
```
